```python
import math
import jax
import jax.numpy as jnp
from jax import lax
import numpy as np

D_MODEL = 1024
BATCH = 2
SEQ = 8192
DEPTH = 2

RWKV_HEADS = 8
RWKV_HEAD_DIM = 64
RWKV_WIDTH = RWKV_HEADS * RWKV_HEAD_DIM
DECAY_LORA = 64
AAA_LORA = 64
GATE_LORA = 160
RWKV_SIZES = (RWKV_WIDTH, RWKV_WIDTH, RWKV_WIDTH, DECAY_LORA, AAA_LORA, GATE_LORA)
RWKV_COLS = sum(RWKV_SIZES)
RWKV_GN_EPS = 64e-5
LRU_WIDTH = 512
LRU_BLOCKS = 8
LRU_BLOCK_DIM = LRU_WIDTH // LRU_BLOCKS
CONV_WIDTH = 4
LRU_C = 8.0
ATTN_HEAD_DIM = 64
DILATION_PAIRS = ((128, 1), (512, 4), (2048, 16))
N_GROUPS = len(DILATION_PAIRS)
HEADS_PER_GROUP = 4
ATTN_HEADS = N_GROUPS * HEADS_PER_GROUP
ATTN_WIDTH = ATTN_HEADS * ATTN_HEAD_DIM
ATTN_OUT_WIDTH = HEADS_PER_GROUP * ATTN_HEAD_DIM
BLK = 128
N_BUCKETS = 32
MAX_DISTANCE = 2048
NEG_INF = -1e30
N_BRANCHES = 3
D_FF = 4 * D_MODEL
RMS_EPS = 1e-6
IN_SIZES = (RWKV_COLS, LRU_WIDTH, LRU_WIDTH, ATTN_WIDTH, ATTN_WIDTH, ATTN_WIDTH, N_BRANCHES * D_MODEL)
D_IN = sum(IN_SIZES)

kernel_name = 'hybrid_rwkv7_rglru_dilated_attn_block'


def split_cols(z, sizes):
    parts, off = [], 0
    for s in sizes:
        parts.append(z[..., off:off + s])
        off += s
    return parts


def rms_norm(x, g):
    xf = x.astype(jnp.float32)
    y = xf * lax.rsqrt(jnp.mean(xf * xf, axis=-1, keepdims=True) + RMS_EPS)
    return (y * g.astype(jnp.float32)).astype(x.dtype)


def token_shift(t):
    return jnp.pad(t, ((0, 0), (1, 0), (0, 0)))[:, :-1]


def rwkv7_scan(r, decay, k, v, a, b):
    bsz, _, nh, n = r.shape

    def step(state, inp):
        r_t, w_t, k_t, v_t, a_t, b_t = inp
        sa = jnp.einsum('bhij,bhj->bhi', state, a_t)
        state = (state * w_t[:, :, None, :] + sa[..., None] * b_t[:, :, None, :]
                 + v_t[..., None] * k_t[:, :, None, :])
        return state, jnp.einsum('bhij,bhj->bhi', state, r_t)

    seq_major = tuple(jnp.swapaxes(t, 0, 1) for t in (r, decay, k, v, a, b))
    _, y = lax.scan(step, jnp.zeros((bsz, nh, n, n), jnp.float32), seq_major)
    return jnp.swapaxes(y, 0, 1)


def rwkv7_mixer(feats, mu, w0, w_up, a0, a_up, g_up, k_k, k_a, r_k, ln_g, ln_b):
    f32 = jnp.float32
    f = feats.astype(f32)
    f = f + (token_shift(f) - f) * mu.astype(f32)
    r, k, v, xw, xa, xg = split_cols(f, RWKV_SIZES)
    bsz, seq, _ = r.shape
    w = -jax.nn.softplus(-(w0.astype(f32) + jnp.tanh(xw) @ w_up.astype(f32))) - 0.5
    decay = jnp.exp(-jnp.exp(w))
    a = jax.nn.sigmoid(a0.astype(f32) + xa @ a_up.astype(f32))
    g = jax.nn.sigmoid(xg) @ g_up.astype(f32)
    heads = lambda t: t.reshape(bsz, seq, RWKV_HEADS, RWKV_HEAD_DIM)
    kk = heads(k * k_k.astype(f32))
    kk = kk / jnp.maximum(jnp.sqrt(jnp.sum(kk * kk, axis=-1, keepdims=True)), 1e-12)
    k = k * (1.0 + (a - 1.0) * k_a.astype(f32))
    r_h, k_h, v_h, w_h, a_h = heads(r), heads(k), heads(v), heads(decay), heads(a)
    y = rwkv7_scan(r_h, w_h, k_h, v_h, -kk, kk * a_h)
    mean = jnp.mean(y, axis=-1, keepdims=True)
    var = jnp.mean(jnp.square(y - mean), axis=-1, keepdims=True)
    y = ((y - mean) * lax.rsqrt(var + RWKV_GN_EPS)).reshape(bsz, seq, RWKV_WIDTH)
    y = y * ln_g.astype(f32) + ln_b.astype(f32)
    bonus = jnp.sum(r_h * k_h * r_k.astype(f32), axis=-1, keepdims=True) * v_h
    return (y + bonus.reshape(bsz, seq, RWKV_WIDTH)) * g


def linear_scan(a, b):
    def combine(left, right):
        a_l, b_l = left
        a_r, b_r = right
        return a_l * a_r, a_r * b_l + b_r
    _, h = lax.associative_scan(combine, (a, b), axis=1)
    return h


def rglru_mixer(xb, yb, conv_w, conv_b, wa, ba, wx, bx, lam):
    f32 = jnp.float32
    xb = xb.astype(f32)
    bsz, seq, _ = xb.shape
    xc = lax.conv_general_dilated(xb, conv_w.astype(f32)[:, None, :], window_strides=(1,),
                                  padding=((CONV_WIDTH - 1, 0),),
                                  dimension_numbers=('NWC', 'WIO', 'NWC'),
                                  feature_group_count=LRU_WIDTH) + conv_b.astype(f32)
    xblk = xc.reshape(bsz, seq, LRU_BLOCKS, LRU_BLOCK_DIM)
    gate_a = jax.nn.sigmoid(jnp.einsum('bsni,nij->bsnj', xblk, wa.astype(f32)).reshape(bsz, seq, LRU_WIDTH) + ba.astype(f32))
    gate_x = jax.nn.sigmoid(jnp.einsum('bsni,nij->bsnj', xblk, wx.astype(f32)).reshape(bsz, seq, LRU_WIDTH) + bx.astype(f32))
    log_a = -LRU_C * gate_a * jax.nn.softplus(-lam.astype(f32))
    a = jnp.exp(log_a)
    mult = jnp.sqrt(jnp.maximum(-jnp.expm1(2.0 * log_a), 0.0))
    mult = jnp.where((jnp.arange(seq) == 0)[None, :, None], 1.0, mult)
    h = linear_scan(a, xc * gate_x * mult)
    return h * jax.nn.gelu(yb.astype(f32), approximate=True)


def t5_bucket(dist):
    max_exact = N_BUCKETS // 2
    d = jnp.maximum(dist, 0)
    large = max_exact + (jnp.log(jnp.maximum(d, 1).astype(jnp.float32) / max_exact)
                         / math.log(MAX_DISTANCE / max_exact) * (N_BUCKETS - max_exact)).astype(jnp.int32)
    large = jnp.minimum(large, N_BUCKETS - 1)
    return jnp.where(d < max_exact, d, large)


def dilated_group_attention(q, k, v, bias_table, window, dilation):
    bsz, seq, hg, dh = q.shape
    span = dilation * BLK
    s_pad = -(-seq // span) * span
    nb = s_pad // span

    def to_sub(t):
        t = jnp.pad(t, ((0, 0), (0, s_pad - seq), (0, 0), (0, 0)))
        return t.reshape(bsz, nb, BLK, dilation, hg, dh).transpose(0, 3, 1, 2, 4, 5)

    def with_prev(t):
        prev = jnp.pad(t, ((0, 0), (0, 0), (1, 0), (0, 0), (0, 0), (0, 0)))[:, :, :-1]
        return jnp.concatenate([prev, t], axis=3)

    qs = to_sub(q)
    kb, vb = with_prev(to_sub(k)), with_prev(to_sub(v))
    n_keys = window // dilation
    kj = jnp.arange(2 * BLK)[None, :]
    rel = (jnp.arange(BLK)[:, None] + BLK) - kj
    band = (rel >= 0) & (rel <= n_keys)
    valid = band[None] & ((jnp.arange(nb)[:, None, None] > 0) | (kj >= BLK)[None])
    bias = jnp.moveaxis(bias_table.astype(jnp.float32)[t5_bucket(rel * dilation)], -1, 0)
    logits = jnp.einsum('brnqhe,brnkhe->brnhqk', qs, kb) + bias
    logits = jnp.where(valid[None, None, :, None], logits, NEG_INF)
    m = jnp.max(logits, axis=-1, keepdims=True)
    p = jnp.exp(logits - m)
    l = jnp.sum(p, axis=-1)
    o = jnp.einsum('brnhqk,brnkhe->brnqhe', p, vb) / jnp.swapaxes(l, -1, -2)[..., None]
    lse = jnp.swapaxes(m[..., 0] + jnp.log(l), -1, -2)

    def from_sub(t):
        t = jnp.moveaxis(t, 1, 3)
        return t.reshape((bsz, s_pad) + t.shape[4:])[:, :seq]

    return from_sub(o), from_sub(lse)


def dilated_attention_mixer(zq, zk, zv, q_g, k_g, rel_bias):
    f32 = jnp.float32
    bsz, seq, _ = zq.shape
    shp = (bsz, seq, ATTN_HEADS, ATTN_HEAD_DIM)
    q = rms_norm(zq.astype(f32).reshape(shp), q_g) * (ATTN_HEAD_DIM ** -0.5)
    k = rms_norm(zk.astype(f32).reshape(shp), k_g)
    v = zv.astype(f32).reshape(shp)
    outs, lses = [], []
    for gi, (window, dil) in enumerate(DILATION_PAIRS):
        hs = slice(gi * HEADS_PER_GROUP, (gi + 1) * HEADS_PER_GROUP)
        o, lse = dilated_group_attention(q[:, :, hs], k[:, :, hs], v[:, :, hs], rel_bias[:, hs], window, dil)
        outs.append(o)
        lses.append(lse)
    wts = jax.nn.softmax(jnp.stack(lses, axis=0), axis=0)
    o = jnp.sum(jnp.stack(outs, axis=0) * wts[..., None], axis=0)
    return o.reshape(bsz, seq, ATTN_OUT_WIDTH)


def setup_inputs(seed: int = 0) -> dict:
    key = jax.random.key(seed)
    ks = jax.random.split(key, 32)
    L = DEPTH
    nrm = lambda kk, shape, s: jax.random.normal(kk, shape, jnp.float32) * s
    u = jax.random.uniform(ks[22], (L, LRU_WIDTH), jnp.float32, 0.9, 0.999)
    s = u ** (1.0 / LRU_C)
    return {
        'x': nrm(ks[0], (BATCH, SEQ, D_MODEL), 1.0),
        'rel_bias': nrm(ks[1], (N_BUCKETS, ATTN_HEADS), 0.5),
        'norm_mix_g': 1.0 + nrm(ks[2], (L, D_MODEL), 0.02),
        'w_in': nrm(ks[3], (L, D_MODEL, D_IN), D_MODEL ** -0.5),
        'rwkv_mu': jax.random.uniform(ks[4], (L, RWKV_COLS), jnp.float32),
        'rwkv_w0': jax.random.uniform(ks[5], (L, RWKV_WIDTH), jnp.float32, -6.0, -1.0),
        'rwkv_w_up': nrm(ks[6], (L, DECAY_LORA, RWKV_WIDTH), 0.1 * DECAY_LORA ** -0.5),
        'rwkv_a0': nrm(ks[7], (L, RWKV_WIDTH), 0.1),
        'rwkv_a_up': nrm(ks[8], (L, AAA_LORA, RWKV_WIDTH), AAA_LORA ** -0.5),
        'rwkv_g_up': nrm(ks[9], (L, GATE_LORA, RWKV_WIDTH), GATE_LORA ** -0.5),
        'rwkv_k_k': 0.85 + nrm(ks[10], (L, RWKV_WIDTH), 0.02),
        'rwkv_k_a': 1.0 + nrm(ks[11], (L, RWKV_WIDTH), 0.02),
        'rwkv_r_k': nrm(ks[12], (L, RWKV_HEADS, RWKV_HEAD_DIM), 0.1),
        'rwkv_ln_g': 1.0 + nrm(ks[13], (L, RWKV_WIDTH), 0.02),
        'rwkv_ln_b': nrm(ks[14], (L, RWKV_WIDTH), 0.02),
        'proj_a': nrm(ks[15], (L, RWKV_WIDTH, D_MODEL), RWKV_WIDTH ** -0.5),
        'conv_w': nrm(ks[16], (L, CONV_WIDTH, LRU_WIDTH), CONV_WIDTH ** -0.5),
        'conv_b': nrm(ks[17], (L, LRU_WIDTH), 0.02),
        'lru_wa': nrm(ks[18], (L, LRU_BLOCKS, LRU_BLOCK_DIM, LRU_BLOCK_DIM), LRU_BLOCK_DIM ** -0.5),
        'lru_ba': nrm(ks[19], (L, LRU_WIDTH), 0.02),
        'lru_wx': nrm(ks[20], (L, LRU_BLOCKS, LRU_BLOCK_DIM, LRU_BLOCK_DIM), LRU_BLOCK_DIM ** -0.5),
        'lru_bx': nrm(ks[21], (L, LRU_WIDTH), 0.02),
        'lru_lambda': jnp.log(s) - jnp.log1p(-s),
        'proj_b': nrm(ks[23], (L, LRU_WIDTH, D_MODEL), LRU_WIDTH ** -0.5),
        'q_norm_g': 1.0 + nrm(ks[24], (L, ATTN_HEAD_DIM), 0.02),
        'k_norm_g': 1.0 + nrm(ks[25], (L, ATTN_HEAD_DIM), 0.02),
        'proj_c': nrm(ks[26], (L, ATTN_OUT_WIDTH, D_MODEL), ATTN_OUT_WIDTH ** -0.5),
        'w_out': nrm(ks[27], (L, D_MODEL, D_MODEL), D_MODEL ** -0.5),
        'norm_mlp_g': 1.0 + nrm(ks[28], (L, D_MODEL), 0.02),
        'mlp_up': nrm(ks[29], (L, D_MODEL, D_FF), D_MODEL ** -0.5),
        'mlp_down': nrm(ks[30], (L, D_FF, D_MODEL), D_FF ** -0.5),
    }


def reference(x, rel_bias, norm_mix_g, w_in, rwkv_mu, rwkv_w0, rwkv_w_up, rwkv_a0, rwkv_a_up,
              rwkv_g_up, rwkv_k_k, rwkv_k_a, rwkv_r_k, rwkv_ln_g, rwkv_ln_b, proj_a, conv_w, conv_b,
              lru_wa, lru_ba, lru_wx, lru_bx, lru_lambda, proj_b, q_norm_g, k_norm_g, proj_c, w_out,
              norm_mlp_g, mlp_up, mlp_down):
    dt = x.dtype
    bsz, seq, _ = x.shape
    for l in range(DEPTH):
        h = rms_norm(x, norm_mix_g[l])
        z_a, z_bx, z_by, z_q, z_k, z_v, z_g = split_cols(h @ w_in[l], IN_SIZES)
        y_a = rwkv7_mixer(z_a, rwkv_mu[l], rwkv_w0[l], rwkv_w_up[l], rwkv_a0[l], rwkv_a_up[l],
                          rwkv_g_up[l], rwkv_k_k[l], rwkv_k_a[l], rwkv_r_k[l], rwkv_ln_g[l],
                          rwkv_ln_b[l]).astype(dt) @ proj_a[l]
        y_b = rglru_mixer(z_bx, z_by, conv_w[l], conv_b[l], lru_wa[l], lru_ba[l], lru_wx[l],
                          lru_bx[l], lru_lambda[l]).astype(dt) @ proj_b[l]
        y_c = dilated_attention_mixer(z_q, z_k, z_v, q_norm_g[l], k_norm_g[l],
                                      rel_bias).astype(dt) @ proj_c[l]
        gates = jax.nn.sigmoid(z_g.reshape(bsz, seq, N_BRANCHES, D_MODEL))
        merged = gates[:, :, 0] * y_a + gates[:, :, 1] * y_b + gates[:, :, 2] * y_c
        x = x + merged @ w_out[l]
        u = rms_norm(x, norm_mlp_g[l]) @ mlp_up[l]
        x = x + jnp.square(jax.nn.relu(u)) @ mlp_down[l]
    return x
```

```python
import functools
import math

import jax
import jax.numpy as jnp
from jax import lax
from jax.experimental import pallas as pl
from jax.experimental.pallas import tpu as pltpu

F32 = jnp.float32
BF16 = jnp.bfloat16

N_HEADS_A = 8
HEAD = 64
PAIR = 2 * HEAD
WIDTH_A = N_HEADS_A * HEAD
N_PAIRS = WIDTH_A // PAIR
CHUNK = 64
LORA_W, LORA_A, LORA_G = 64, 64, 160
LORA_PAD = 384
GN_EPS = 64e-5
WIDTH_B = 512
LRU_BLOCK = 64
CONV_TAPS = 4
LRU_C = 8.0
GROUPS = ((128, 1), (512, 4), (2048, 16))
HEADS_PER_GROUP = 4
GROUP_W = HEADS_PER_GROUP * HEAD
WIDTH_C = len(GROUPS) * GROUP_W
QBLK = 128
N_BUCKETS = 32
MAX_DISTANCE = 2048
NEG_INF = -1e30
RMS_EPS = 1e-6
VMEM_LIMIT = 56 * 1024 * 1024

HI = lax.Precision.HIGHEST


def _cparams(sem):
    return pltpu.CompilerParams(dimension_semantics=sem, vmem_limit_bytes=VMEM_LIMIT)


def _dot(a, b, precision=None):
    return jnp.dot(a, b, preferred_element_type=F32, precision=precision)


def _dot_nt(a, b, precision=None):
    return lax.dot_general(a, b, (((1,), (1,)), ((), ())), preferred_element_type=F32, precision=precision)


def _dot_tn(a, b, precision=None):
    return lax.dot_general(a, b, (((0,), (0,)), ((), ())), preferred_element_type=F32, precision=precision)


def _sigmoid(x):
    return 1.0 / (1.0 + jnp.exp(-x))


def _softplus(x):
    return jnp.maximum(x, 0.0) + jnp.log1p(jnp.exp(-jnp.abs(x)))


def _head_ones(width):
    i = jnp.arange(width) // HEAD
    return (i[:, None] == i[None, :]).astype(BF16)


def _norm_matmul_kernel(x_ref, g_ref, w_ref, o_ref, h_ref):
    @pl.when(pl.program_id(1) == 0)
    def _():
        x = x_ref[...]
        ms = jnp.mean(x * x, axis=-1, keepdims=True)
        h_ref[...] = (x * lax.rsqrt(ms + RMS_EPS) * g_ref[...]).astype(BF16)

    o_ref[...] = _dot(h_ref[...], w_ref[...]).astype(o_ref.dtype)


def _norm_matmul(x2d, g, w, tm, tn):
    m, d = x2d.shape
    n = w.shape[1]
    return pl.pallas_call(
        _norm_matmul_kernel,
        grid=(m // tm, n // tn),
        in_specs=[pl.BlockSpec((tm, d), lambda i, j: (i, 0)),
                  pl.BlockSpec((1, d), lambda i, j: (0, 0)),
                  pl.BlockSpec((d, tn), lambda i, j: (0, j))],
        out_specs=pl.BlockSpec((tm, tn), lambda i, j: (i, j)),
        out_shape=jax.ShapeDtypeStruct((m, n), F32),
        scratch_shapes=[pltpu.VMEM((tm, d), BF16)],
        compiler_params=_cparams(("parallel", "arbitrary")),
        name="norm_matmul",
    )(x2d, g, w)


def _shift_rows(cur, prev8, first):
    prev_row = jnp.where(first, 0.0, prev8[7:8, :])
    rolled = pltpu.roll(cur, 1, axis=0)
    row = lax.broadcasted_iota(jnp.int32, cur.shape, 0)
    return jnp.where(row == 0, prev_row, rolled)


def _rwkv_prep_kernel(zr_ref, zrp_ref, zl_ref, zlp_ref, mur_ref, mul_ref, w0_ref, wup_ref, a0_ref,
                      aup_ref, gup_ref, kk_ref, ka_ref, rk_ref, ones_ref,
                      r_out, k_out, v_out, lw_out, as_out, bs_out, g_out, bonus_out):
    first = pl.program_id(1) == 0
    zr = zr_ref[0]
    zl = zl_ref[0]
    fr = zr + (_shift_rows(zr, zrp_ref[0], first) - zr) * mur_ref[...]
    fl = zl + (_shift_rows(zl, zlp_ref[0], first) - zl) * mul_ref[...]
    r = fr[:, 0:WIDTH_A]
    k = fr[:, WIDTH_A:2 * WIDTH_A]
    v = fr[:, 2 * WIDTH_A:3 * WIDTH_A]
    x_wa = fl[:, 0:LORA_W + LORA_A]
    x_g = fl[:, LORA_W + LORA_A:LORA_PAD]
    ones = ones_ref[...]

    w = -_softplus(-(w0_ref[...] + _dot(jnp.tanh(x_wa).astype(BF16), wup_ref[...]))) - 0.5
    a = _sigmoid(a0_ref[...] + _dot(x_wa.astype(BF16), aup_ref[...]))
    g = _dot(_sigmoid(x_g).astype(BF16), gup_ref[...])
    kk = k * kk_ref[...]
    kk = kk / jnp.maximum(jnp.sqrt(_dot((kk * kk).astype(BF16), ones)), 1e-12)
    k2 = k * (1.0 + (a - 1.0) * ka_ref[...])
    bonus = _dot((r * k2 * rk_ref[...]).astype(BF16), ones) * v

    r_out[0] = r
    k_out[0] = k2
    v_out[0] = v
    lw_out[0] = -jnp.exp(w)
    as_out[0] = -kk
    bs_out[0] = kk * a
    g_out[0] = g
    bonus_out[0] = bonus


def _rwkv_prep(zr, zl, p, ts):
    b, s, _ = zr.shape
    prev = lambda bi, i: (bi, jnp.maximum(i * (ts // 8) - 1, 0), 0)
    cur = lambda bi, i: (bi, i, 0)
    const = lambda bi, i: (0, 0)
    wr, wl = 3 * WIDTH_A, LORA_PAD
    vec = pl.BlockSpec((1, WIDTH_A), const)
    out = jax.ShapeDtypeStruct((b, s, WIDTH_A), F32)
    return pl.pallas_call(
        _rwkv_prep_kernel,
        grid=(b, s // ts),
        in_specs=[pl.BlockSpec((1, ts, wr), cur), pl.BlockSpec((1, 8, wr), prev),
                  pl.BlockSpec((1, ts, wl), cur), pl.BlockSpec((1, 8, wl), prev),
                  pl.BlockSpec((1, wr), const), pl.BlockSpec((1, wl), const),
                  vec, pl.BlockSpec((LORA_W + LORA_A, WIDTH_A), const),
                  vec, pl.BlockSpec((LORA_W + LORA_A, WIDTH_A), const),
                  pl.BlockSpec((LORA_PAD - LORA_W - LORA_A, WIDTH_A), const),
                  vec, vec, vec, pl.BlockSpec((WIDTH_A, WIDTH_A), const)],
        out_specs=[pl.BlockSpec((1, ts, WIDTH_A), cur)] * 8,
        out_shape=[out] * 8,
        compiler_params=_cparams(("parallel", "arbitrary")),
        name="rwkv_prep",
    )(zr, zr, zl, zl, p["mu_r"], p["mu_l"], p["w0"], p["w_up"], p["a0"], p["a_up"], p["g_up"],
      p["k_k"], p["k_a"], p["r_k"], p["ones"])


def _stack_heads(x):
    lo = lax.broadcasted_iota(jnp.int32, x.shape, 1) < HEAD
    return jnp.concatenate([jnp.where(lo, x, 0.0), jnp.where(lo, 0.0, x)], axis=0)


def _unstack_heads(x):
    return x[0:CHUNK, :] + x[CHUNK:2 * CHUNK, :]


def _split3(x):
    hi = x.astype(BF16)
    r1 = x - hi.astype(F32)
    mid = r1.astype(BF16)
    lo = (r1 - mid.astype(F32)).astype(BF16)
    return hi, mid, lo


def _rwkv_local_kernel(r_ref, k_ref, v_ref, lw_ref, as_ref, bs_ref, t_out, g_out, rh_out, yh_out, *, n_chunks):
    two_c = 2 * CHUNK
    row = lax.broadcasted_iota(jnp.int32, (two_c, two_c), 0)
    col = lax.broadcasted_iota(jnp.int32, (two_c, two_c), 1)
    strict = col < row
    incl = col <= row
    eye = (col == row).astype(F32)
    tri = (lax.broadcasted_iota(jnp.int32, (CHUNK, CHUNK), 1)
           <= lax.broadcasted_iota(jnp.int32, (CHUNK, CHUNK), 0)).astype(BF16)

    def chunk_body(c, carry):
        rows = pl.ds(pl.multiple_of(c * CHUNK, CHUNK), CHUNK)
        for p in range(N_PAIRS):
            lanes = slice(p * PAIR, (p + 1) * PAIR)
            lw = lw_ref[0, rows, lanes]
            hi, mid, lo = _split3(lw)
            lg = _dot(tri, hi) + _dot(tri, mid) + _dot(tri, lo)
            lg_end = lg[CHUNK - 1:CHUNK, :]
            e_pos = jnp.exp(lg)
            e_neg = jnp.exp(-lg)
            e_end = jnp.exp(lg_end - lg)
            a_s = as_ref[0, rows, lanes]
            b_s = bs_ref[0, rows, lanes]
            kk = k_ref[0, rows, lanes]
            a_t = _stack_heads(a_s * jnp.exp(lg - lw))
            r_t = _stack_heads(r_ref[0, rows, lanes] * e_pos)
            b_t = _stack_heads(b_s * e_neg)
            k_t = _stack_heads(kk * e_neg)
            b_p = _stack_heads(b_s * e_end)
            k_p = _stack_heads(kk * e_end)
            v_s = _stack_heads(v_ref[0, rows, lanes])

            l_ab = jnp.where(strict, _dot_nt(a_t, b_t, HI), 0.0)
            l_ak = jnp.where(strict, _dot_nt(a_t, k_t, HI), 0.0)
            l_rb = jnp.where(incl, _dot_nt(r_t, b_t, HI), 0.0)
            l_rk = jnp.where(incl, _dot_nt(r_t, k_t, HI), 0.0)

            inv = eye + l_ab
            pw = l_ab
            for _ in range(int(math.log2(CHUNK)) - 1):
                pw = _dot(pw, pw, HI)
                inv = inv + _dot(inv, pw, HI)

            a_h = _dot(inv, a_t, HI)
            w_h = _dot(inv, _dot(l_ak, v_s, HI), HI)
            r_h = r_t + _dot(l_rb, a_h, HI)
            y_h = _dot(l_rb, w_h, HI) + _dot(l_rk, v_s, HI)
            t_out[0, c, p] = eye * jnp.exp(lg_end) + _dot_tn(a_h, b_p, HI)
            g_out[0, c, p] = _dot_tn(w_h, b_p, HI) + _dot_tn(v_s, k_p, HI)
            rh_out[0, rows, lanes] = _unstack_heads(r_h)
            yh_out[0, rows, lanes] = _unstack_heads(y_h)
        return carry

    lax.fori_loop(0, n_chunks, chunk_body, 0)


def _rwkv_local(r, k, v, lw, a_s, b_s, ts):
    b, s, _ = r.shape
    n_chunks = ts // CHUNK
    cur = lambda bi, i: (bi, i, 0)
    mat = lambda bi, i: (bi, i, 0, 0, 0)
    seq_spec = pl.BlockSpec((1, ts, WIDTH_A), cur)
    mat_spec = pl.BlockSpec((1, n_chunks, N_PAIRS, PAIR, PAIR), mat)
    seq_shape = jax.ShapeDtypeStruct((b, s, WIDTH_A), F32)
    mat_shape = jax.ShapeDtypeStruct((b, s // CHUNK, N_PAIRS, PAIR, PAIR), F32)
    return pl.pallas_call(
        functools.partial(_rwkv_local_kernel, n_chunks=n_chunks),
        grid=(b, s // ts),
        in_specs=[seq_spec] * 6,
        out_specs=[mat_spec, mat_spec, seq_spec, seq_spec],
        out_shape=[mat_shape, mat_shape, seq_shape, seq_shape],
        compiler_params=_cparams(("parallel", "parallel")),
        name="rwkv_local",
    )(r, k, v, lw, a_s, b_s)


def _rwkv_state_kernel(t_ref, gm_ref, rh_ref, yh_ref, g_ref, bonus_ref, lng_ref, lnb_ref, ones_ref,
                       y_out, s_ref, *, n_chunks):
    @pl.when(pl.program_id(1) == 0)
    def _():
        s_ref[...] = jnp.zeros_like(s_ref)

    ones = ones_ref[...]

    def chunk_body(c, carry):
        rows = pl.ds(pl.multiple_of(c * CHUNK, CHUNK), CHUNK)
        ys = []
        for p in range(N_PAIRS):
            lanes = slice(p * PAIR, (p + 1) * PAIR)
            s0 = s_ref[p]
            ys.append(_dot_nt(rh_ref[0, rows, lanes], s0, HI) + yh_ref[0, rows, lanes])
            s_ref[p] = _dot(s0, t_ref[0, c, p], HI) + gm_ref[0, c, p]
        y = jnp.concatenate(ys, axis=1)
        mean = _dot(y.astype(BF16), ones) * (1.0 / HEAD)
        yc = y - mean
        var = _dot((yc * yc).astype(BF16), ones) * (1.0 / HEAD)
        yn = yc * lax.rsqrt(var + GN_EPS) * lng_ref[...] + lnb_ref[...]
        y_out[0, rows, :] = (yn + bonus_ref[0, rows, :]) * g_ref[0, rows, :]
        return carry

    lax.fori_loop(0, n_chunks, chunk_body, 0)


def _rwkv_state(t, gm, rh, yh, g, bonus, p, ts):
    b, s, _ = rh.shape
    n_chunks = ts // CHUNK
    cur = lambda bi, i: (bi, i, 0)
    mat = lambda bi, i: (bi, i, 0, 0, 0)
    const = lambda bi, i: (0, 0)
    seq_spec = pl.BlockSpec((1, ts, WIDTH_A), cur)
    mat_spec = pl.BlockSpec((1, n_chunks, N_PAIRS, PAIR, PAIR), mat)
    vec = pl.BlockSpec((1, WIDTH_A), const)
    return pl.pallas_call(
        functools.partial(_rwkv_state_kernel, n_chunks=n_chunks),
        grid=(b, s // ts),
        in_specs=[mat_spec, mat_spec, seq_spec, seq_spec, seq_spec, seq_spec, vec, vec,
                  pl.BlockSpec((WIDTH_A, WIDTH_A), const)],
        out_specs=seq_spec,
        out_shape=jax.ShapeDtypeStruct((b, s, WIDTH_A), F32),
        scratch_shapes=[pltpu.VMEM((N_PAIRS, PAIR, PAIR), F32)],
        compiler_params=_cparams(("parallel", "arbitrary")),
        name="rwkv_state",
    )(t, gm, rh, yh, g, bonus, p["ln_g"], p["ln_b"], p["ones"])


def _rwkv_mixer(zr, zl, p):
    r, k, v, lw, a_s, b_s, g, bonus = _rwkv_prep(zr, zl, p, ts=256)
    t, gm, rh, yh = _rwkv_local(r, k, v, lw, a_s, b_s, ts=256)
    return _rwkv_state(t, gm, rh, yh, g, bonus, p, ts=256)


def _lru_kernel(z_ref, zp_ref, cw_ref, cb_ref, wa_ref, ba_ref, wx_ref, bx_ref, lam_ref,
                o_ref, a_scr, b_scr, carry_ref, *, ts):
    i = pl.program_id(1)

    @pl.when(i == 0)
    def _():
        carry_ref[...] = jnp.zeros_like(carry_ref)

    x = z_ref[0, :, 0:WIDTH_B]
    yb = z_ref[0, :, WIDTH_B:2 * WIDTH_B]
    prev8 = jnp.where(i == 0, 0.0, zp_ref[0])
    ext = jnp.concatenate([prev8, x], axis=0)
    xc = x * cw_ref[CONV_TAPS - 1:CONV_TAPS, :] + cb_ref[...]
    for back in range(1, CONV_TAPS):
        tap = CONV_TAPS - 1 - back
        xc = xc + pltpu.roll(ext, back, axis=0)[8:, :] * cw_ref[tap:tap + 1, :]

    half = WIDTH_B // 2
    xcb = xc.astype(BF16)
    ga = jnp.concatenate([_dot(xcb[:, j * half:(j + 1) * half], wa_ref[j]) for j in range(2)], axis=1)
    gx = jnp.concatenate([_dot(xcb[:, j * half:(j + 1) * half], wx_ref[j]) for j in range(2)], axis=1)
    gate_a = _sigmoid(ga + ba_ref[...])
    gate_x = _sigmoid(gx + bx_ref[...])
    log_a = -LRU_C * gate_a * _softplus(-lam_ref[...])
    a = jnp.exp(log_a)
    mult = jnp.sqrt(jnp.maximum(-jnp.tanh(log_a) * (1.0 + a * a), 0.0))
    t_glob = i * ts + lax.broadcasted_iota(jnp.int32, (ts, WIDTH_B), 0)
    mult = jnp.where(t_glob == 0, 1.0, mult)
    a_scr[...] = a
    b_scr[...] = xc * gate_x * mult

    row8 = lax.broadcasted_iota(jnp.int32, (8, WIDTH_B), 0)

    def group_body(gi, h_prev):
        rows = pl.ds(pl.multiple_of(gi * 8, 8), 8)
        a8 = a_scr[rows, :]
        b8 = b_scr[rows, :]
        for sh in (1, 2, 4):
            ar = pltpu.roll(a8, sh, axis=0)
            br = pltpu.roll(b8, sh, axis=0)
            m = row8 >= sh
            b8 = jnp.where(m, a8 * br + b8, b8)
            a8 = jnp.where(m, a8 * ar, a8)
        h8 = a8 * h_prev + b8
        b_scr[rows, :] = h8
        return h8[7:8, :]

    carry_ref[0:1, :] = lax.fori_loop(0, ts // 8, group_body, carry_ref[0:1, :])
    h = b_scr[...]
    gelu = 0.5 * yb * (1.0 + jnp.tanh(math.sqrt(2.0 / math.pi) * (yb + 0.044715 * (yb * yb * yb))))
    o_ref[0] = h * gelu


def _lru_mixer(zb, p, ts):
    b, s, _ = zb.shape
    cur = lambda bi, i: (bi, i, 0)
    prev = lambda bi, i: (bi, jnp.maximum(i * (ts // 8) - 1, 0), 0)
    const2 = lambda bi, i: (0, 0)
    const3 = lambda bi, i: (0, 0, 0)
    vec = pl.BlockSpec((1, WIDTH_B), const2)
    half = WIDTH_B // 2
    return pl.pallas_call(
        functools.partial(_lru_kernel, ts=ts),
        grid=(b, s // ts),
        in_specs=[pl.BlockSpec((1, ts, 2 * WIDTH_B), cur), pl.BlockSpec((1, 8, WIDTH_B), prev),
                  pl.BlockSpec((CONV_TAPS, WIDTH_B), const2), vec,
                  pl.BlockSpec((2, half, half), const3), vec,
                  pl.BlockSpec((2, half, half), const3), vec, vec],
        out_specs=pl.BlockSpec((1, ts, WIDTH_B), cur),
        out_shape=jax.ShapeDtypeStruct((b, s, WIDTH_B), F32),
        scratch_shapes=[pltpu.VMEM((ts, WIDTH_B), F32), pltpu.VMEM((ts, WIDTH_B), F32),
                        pltpu.VMEM((8, WIDTH_B), F32)],
        compiler_params=_cparams(("parallel", "arbitrary")),
        name="lru_mixer",
    )(zb, zb, p["conv_w"], p["conv_b"], p["wa"], p["ba"], p["wx"], p["bx"], p["lam"])


def _attn_kernel(q_ref, kc_ref, kp_ref, vc_ref, vp_ref, bias_ref, qg_ref, kg_ref, ones_ref, o_ref, lse_ref):
    n = pl.program_id(2)
    ones = ones_ref[...]

    def head_norm(x, gain):
        ms = _dot((x * x).astype(BF16), ones) * (1.0 / HEAD)
        return x * lax.rsqrt(ms + RMS_EPS) * gain

    q = head_norm(q_ref[0], qg_ref[...]) * (HEAD ** -0.5)
    kk = head_norm(jnp.concatenate([kp_ref[0], kc_ref[0]], axis=0), kg_ref[...]).astype(BF16)
    vv = jnp.concatenate([vp_ref[0], vc_ref[0]], axis=0).astype(BF16)
    lane = lax.broadcasted_iota(jnp.int32, (QBLK, GROUP_W), 1)
    key_ok = (lax.broadcasted_iota(jnp.int32, (QBLK, 2 * QBLK), 1) >= QBLK) | (n > 0)
    out = jnp.zeros((QBLK, GROUP_W), F32)
    lse = jnp.zeros((QBLK, GROUP_W), F32)
    for h in range(HEADS_PER_GROUP):
        in_head = (lane >= h * HEAD) & (lane < (h + 1) * HEAD)
        qh = jnp.where(in_head, q, 0.0).astype(BF16)
        logits = jnp.where(key_ok, _dot_nt(qh, kk) + bias_ref[h], NEG_INF)
        m = jnp.max(logits, axis=-1, keepdims=True)
        pr = jnp.exp(logits - m)
        l = jnp.sum(pr, axis=-1, keepdims=True)
        oh = _dot(pr.astype(BF16), vv) / l
        out = jnp.where(in_head, oh, out)
        lse = jnp.where(in_head, m + jnp.log(l), lse)
    o_ref[0] = out
    lse_ref[0] = lse


def _attn_group(zqkv, bias, qg, kg, ones, gi, dil):
    b, s, w = zqkv.shape
    sd = s // dil
    nb = sd // QBLK
    zv = zqkv.reshape(b, sd, dil * w)
    per_row = w // GROUP_W
    col = lambda part: (lambda bi, r, n: (bi, n, r * per_row + part * len(GROUPS) + gi))
    colp = lambda part: (lambda bi, r, n: (bi, jnp.maximum(n - 1, 0), r * per_row + part * len(GROUPS) + gi))
    blk = lambda f: pl.BlockSpec((1, QBLK, GROUP_W), f)
    const2 = lambda bi, r, n: (0, 0)
    out_map = lambda bi, r, n: (bi, n, r)
    o, lse = pl.pallas_call(
        _attn_kernel,
        grid=(b, dil, nb),
        in_specs=[blk(col(0)), blk(col(1)), blk(colp(1)), blk(col(2)), blk(colp(2)),
                  pl.BlockSpec((HEADS_PER_GROUP, QBLK, 2 * QBLK), lambda bi, r, n: (0, 0, 0)),
                  pl.BlockSpec((1, GROUP_W), const2), pl.BlockSpec((1, GROUP_W), const2),
                  pl.BlockSpec((GROUP_W, GROUP_W), const2)],
        out_specs=[blk(out_map), blk(out_map)],
        out_shape=[jax.ShapeDtypeStruct((b, sd, dil * GROUP_W), F32)] * 2,
        compiler_params=_cparams(("parallel", "parallel", "arbitrary")),
        name=f"dilated_attn_g{gi}",
    )(zv, zv, zv, zv, zv, bias, qg, kg, ones)
    return o.reshape(b, s, GROUP_W), lse.reshape(b, s, GROUP_W)


def _t5_bucket(dist):
    max_exact = N_BUCKETS // 2
    d = jnp.maximum(dist, 0)
    large = max_exact + (jnp.log(jnp.maximum(d, 1).astype(F32) / max_exact)
                         / math.log(MAX_DISTANCE / max_exact) * (N_BUCKETS - max_exact)).astype(jnp.int32)
    large = jnp.minimum(large, N_BUCKETS - 1)
    return jnp.where(d < max_exact, d, large)


def _attn_bias_tiles(rel_bias):
    tiles = []
    kj = jnp.arange(2 * QBLK)[None, :]
    rel = (jnp.arange(QBLK)[:, None] + QBLK) - kj
    for gi, (window, dil) in enumerate(GROUPS):
        band = (rel >= 0) & (rel <= window // dil)
        tab = rel_bias.astype(F32)[:, gi * HEADS_PER_GROUP:(gi + 1) * HEADS_PER_GROUP]
        bias = jnp.moveaxis(tab[_t5_bucket(rel * dil)], -1, 0)
        tiles.append(jnp.where(band[None], bias, NEG_INF))
    return tiles


def _merge_kernel(x_ref, ya_ref, yb_ref, o0_ref, o1_ref, o2_ref, l0_ref, l1_ref, l2_ref, zg_ref,
                  pa_ref, pb_ref, pc_ref, wo_ref, out_ref):
    d = x_ref.shape[-1]
    l0, l1, l2 = l0_ref[...], l1_ref[...], l2_ref[...]
    m = jnp.maximum(jnp.maximum(l0, l1), l2)
    e0, e1, e2 = jnp.exp(l0 - m), jnp.exp(l1 - m), jnp.exp(l2 - m)
    yc = (o0_ref[...] * e0 + o1_ref[...] * e1 + o2_ref[...] * e2) / (e0 + e1 + e2)
    merged = (_sigmoid(zg_ref[:, 0:d]) * _dot(ya_ref[...].astype(BF16), pa_ref[...])
              + _sigmoid(zg_ref[:, d:2 * d]) * _dot(yb_ref[...].astype(BF16), pb_ref[...])
              + _sigmoid(zg_ref[:, 2 * d:3 * d]) * _dot(yc.astype(BF16), pc_ref[...]))
    out_ref[...] = x_ref[...] + _dot(merged.astype(BF16), wo_ref[...])


def _merge(x2d, ya, yb, os_, ls_, zg, p, tm):
    m, d = x2d.shape
    row = lambda w: pl.BlockSpec((tm, w), lambda i: (i, 0))
    full = lambda a: pl.BlockSpec(a.shape, lambda i: (0, 0))
    return pl.pallas_call(
        _merge_kernel,
        grid=(m // tm,),
        in_specs=[row(d), row(WIDTH_A), row(WIDTH_B)] + [row(GROUP_W)] * 6 + [row(3 * d),
                  full(p["proj_a"]), full(p["proj_b"]), full(p["proj_c"]), full(p["w_out"])],
        out_specs=row(d),
        out_shape=jax.ShapeDtypeStruct((m, d), F32),
        compiler_params=_cparams(("parallel",)),
        name="merge",
    )(x2d, ya, yb, *os_, *ls_, zg, p["proj_a"], p["proj_b"], p["proj_c"], p["w_out"])


def _mlp_kernel(x_ref, g_ref, wu_ref, wd_ref, o_ref, h_ref, acc_ref):
    j = pl.program_id(1)

    @pl.when(j == 0)
    def _():
        x = x_ref[...]
        ms = jnp.mean(x * x, axis=-1, keepdims=True)
        h_ref[...] = (x * lax.rsqrt(ms + RMS_EPS) * g_ref[...]).astype(BF16)
        acc_ref[...] = jnp.zeros_like(acc_ref)

    u = jnp.maximum(_dot(h_ref[...], wu_ref[...]), 0.0)
    acc_ref[...] += _dot((u * u).astype(BF16), wd_ref[...])

    @pl.when(j == pl.num_programs(1) - 1)
    def _():
        o_ref[...] = x_ref[...] + acc_ref[...]


def _mlp(x2d, g, wu, wd, tm, tf):
    m, d = x2d.shape
    f = wu.shape[1]
    return pl.pallas_call(
        _mlp_kernel,
        grid=(m // tm, f // tf),
        in_specs=[pl.BlockSpec((tm, d), lambda i, j: (i, 0)),
                  pl.BlockSpec((1, d), lambda i, j: (0, 0)),
                  pl.BlockSpec((d, tf), lambda i, j: (0, j)),
                  pl.BlockSpec((tf, d), lambda i, j: (j, 0))],
        out_specs=pl.BlockSpec((tm, d), lambda i, j: (i, 0)),
        out_shape=jax.ShapeDtypeStruct((m, d), F32),
        scratch_shapes=[pltpu.VMEM((tm, d), BF16), pltpu.VMEM((tm, d), F32)],
        compiler_params=_cparams(("parallel", "arbitrary")),
        name="mlp",
    )(x2d, g, wu, wd)


def _pad_rows(w, lo, total):
    return jnp.pad(w, ((lo, total - lo - w.shape[0]), (0, 0)))


def _block_diag_halves(w):
    n, bd, _ = w.shape
    per = n // 2
    out = jnp.zeros((2, per * bd, per * bd), w.dtype)
    for i in range(n):
        j, q = divmod(i, per)
        out = out.at[j, q * bd:(q + 1) * bd, q * bd:(q + 1) * bd].set(w[i])
    return out.astype(BF16)


def _layer(x, l, bias_tiles, prm):
    (norm_mix_g, w_in, rwkv_mu, rwkv_w0, rwkv_w_up, rwkv_a0, rwkv_a_up, rwkv_g_up, rwkv_k_k, rwkv_k_a,
     rwkv_r_k, rwkv_ln_g, rwkv_ln_b, proj_a, conv_w, conv_b, lru_wa, lru_ba, lru_wx, lru_bx, lru_lambda,
     proj_b, q_norm_g, k_norm_g, proj_c, w_out, norm_mlp_g, mlp_up, mlp_down) = [t[l] for t in prm]
    b, s, d = x.shape
    x2d = x.reshape(b * s, d)
    row = lambda t: t.reshape(1, -1).astype(F32)

    c_rkv, c_lora = 3 * WIDTH_A, LORA_W + LORA_A + LORA_G
    o_l = c_rkv
    o_b = o_l + c_lora
    o_c = o_b + 2 * WIDTH_B
    o_g = o_c + 3 * WIDTH_C
    wb = w_in.astype(BF16)
    w_l = jnp.pad(wb[:, o_l:o_b], ((0, 0), (0, LORA_PAD - c_lora)))
    g_mix = row(norm_mix_g)
    zr = _norm_matmul(x2d, g_mix, wb[:, 0:o_l], 1024, 768).reshape(b, s, c_rkv)
    zl = _norm_matmul(x2d, g_mix, w_l, 1024, LORA_PAD).reshape(b, s, LORA_PAD)
    zb = _norm_matmul(x2d, g_mix, wb[:, o_b:o_c], 1024, 512).reshape(b, s, 2 * WIDTH_B)
    zc = _norm_matmul(x2d, g_mix, wb[:, o_c:o_g], 1024, 768).reshape(b, s, 3 * WIDTH_C)
    zg = _norm_matmul(x2d, g_mix, wb[:, o_g:], 1024, 768)

    ones_a = _head_ones(WIDTH_A)
    pa = dict(
        mu_r=row(rwkv_mu[0:c_rkv]),
        mu_l=jnp.pad(row(rwkv_mu[c_rkv:]), ((0, 0), (0, LORA_PAD - c_lora))),
        w0=row(rwkv_w0), a0=row(rwkv_a0),
        w_up=_pad_rows(rwkv_w_up, 0, LORA_W + LORA_A).astype(BF16),
        a_up=_pad_rows(rwkv_a_up, LORA_W, LORA_W + LORA_A).astype(BF16),
        g_up=_pad_rows(rwkv_g_up, 0, LORA_PAD - LORA_W - LORA_A).astype(BF16),
        k_k=row(rwkv_k_k), k_a=row(rwkv_k_a), r_k=row(rwkv_r_k),
        ln_g=row(rwkv_ln_g), ln_b=row(rwkv_ln_b), ones=ones_a)
    ya = _rwkv_mixer(zr, zl, pa)

    pb = dict(conv_w=conv_w.astype(F32), conv_b=row(conv_b), wa=_block_diag_halves(lru_wa), ba=row(lru_ba),
              wx=_block_diag_halves(lru_wx), bx=row(lru_bx), lam=row(lru_lambda))
    yb = _lru_mixer(zb, pb, ts=512)

    qg = jnp.tile(row(q_norm_g), (1, HEADS_PER_GROUP))
    kg = jnp.tile(row(k_norm_g), (1, HEADS_PER_GROUP))
    ones_c = _head_ones(GROUP_W)
    os_, ls_ = [], []
    for gi, (_, dil) in enumerate(GROUPS):
        o, lse = _attn_group(zc, bias_tiles[gi], qg, kg, ones_c, gi, dil)
        os_.append(o.reshape(b * s, GROUP_W))
        ls_.append(lse.reshape(b * s, GROUP_W))

    pm = dict(proj_a=proj_a.astype(BF16), proj_b=proj_b.astype(BF16), proj_c=proj_c.astype(BF16),
              w_out=w_out.astype(BF16))
    x1 = _merge(x2d, ya.reshape(b * s, WIDTH_A), yb.reshape(b * s, WIDTH_B), os_, ls_, zg, pm, 512)
    x2 = _mlp(x1, row(norm_mlp_g), mlp_up.astype(BF16), mlp_down.astype(BF16), 1024, 512)
    return x2.reshape(b, s, d)


def kernel(x, rel_bias, norm_mix_g, w_in, rwkv_mu, rwkv_w0, rwkv_w_up, rwkv_a0, rwkv_a_up, rwkv_g_up, rwkv_k_k, rwkv_k_a, rwkv_r_k, rwkv_ln_g, rwkv_ln_b, proj_a, conv_w, conv_b, lru_wa, lru_ba, lru_wx, lru_bx, lru_lambda, proj_b, q_norm_g, k_norm_g, proj_c, w_out, norm_mlp_g, mlp_up, mlp_down):
    prm = (norm_mix_g, w_in, rwkv_mu, rwkv_w0, rwkv_w_up, rwkv_a0, rwkv_a_up, rwkv_g_up, rwkv_k_k, rwkv_k_a,
           rwkv_r_k, rwkv_ln_g, rwkv_ln_b, proj_a, conv_w, conv_b, lru_wa, lru_ba, lru_wx, lru_bx, lru_lambda,
           proj_b, q_norm_g, k_norm_g, proj_c, w_out, norm_mlp_g, mlp_up, mlp_down)
    bias_tiles = _attn_bias_tiles(rel_bias)
    x = x.astype(F32)
    for l in range(norm_mix_g.shape[0]):
        x = _layer(x, l, bias_tiles, prm)
    return x
```

```python
import functools
import math

import jax
import jax.numpy as jnp
from jax import lax
from jax.experimental import pallas as pl
from jax.experimental.pallas import tpu as pltpu

F32 = jnp.float32
BF16 = jnp.bfloat16

N_HEADS_A = 8
HEAD = 64
PAIR = 2 * HEAD
WIDTH_A = N_HEADS_A * HEAD
N_PAIRS = WIDTH_A // PAIR
CHUNK = 64
LOCAL_WAVE = 2
LORA_W, LORA_A, LORA_G = 64, 64, 160
LORA_PAD = 384
GN_EPS = 64e-5
WIDTH_B = 512
LRU_BLOCK = 64
CONV_TAPS = 4
LRU_C = 8.0
GROUPS = ((128, 1), (512, 4), (2048, 16))
HEADS_PER_GROUP = 4
GROUP_W = HEADS_PER_GROUP * HEAD
WIDTH_C = len(GROUPS) * GROUP_W
QBLK = 128
N_BUCKETS = 32
MAX_DISTANCE = 2048
NEG_INF = -1e30
RMS_EPS = 1e-6
VMEM_LIMIT = 56 * 1024 * 1024


def _cparams(sem):
    return pltpu.CompilerParams(dimension_semantics=sem, vmem_limit_bytes=VMEM_LIMIT)


def _dot(a, b):
    return jnp.dot(a, b, preferred_element_type=F32)


def _dot_nt(a, b):
    return lax.dot_general(a, b, (((1,), (1,)), ((), ())), preferred_element_type=F32)


_NN = (((1,), (0,)), ((), ()))
_NT = (((1,), (1,)), ((), ()))
_TN = (((0,), (0,)), ((), ()))


def _mm(a, b, dims=_NN):
    return lax.dot_general(a.astype(BF16), b.astype(BF16), dims, preferred_element_type=F32)


def _sigmoid(x):
    return 1.0 / (1.0 + jnp.exp(-x))


def _softplus(x):
    return jnp.maximum(x, 0.0) + jnp.log1p(jnp.exp(-jnp.abs(x)))


def _head_ones(width):
    i = jnp.arange(width) // HEAD
    return (i[:, None] == i[None, :]).astype(BF16)


def _norm_matmul_kernel(x_ref, g_ref, w_ref, o_ref, h_ref):
    @pl.when(pl.program_id(1) == 0)
    def _():
        x = x_ref[...]
        ms = jnp.mean(x * x, axis=-1, keepdims=True)
        h_ref[...] = (x * lax.rsqrt(ms + RMS_EPS) * g_ref[...]).astype(BF16)

    o_ref[...] = _dot(h_ref[...], w_ref[...]).astype(o_ref.dtype)


def _norm_matmul(x2d, g, w, tm, tn):
    m, d = x2d.shape
    n = w.shape[1]
    return pl.pallas_call(
        _norm_matmul_kernel,
        grid=(m // tm, n // tn),
        in_specs=[pl.BlockSpec((tm, d), lambda i, j: (i, 0)),
                  pl.BlockSpec((1, d), lambda i, j: (0, 0)),
                  pl.BlockSpec((d, tn), lambda i, j: (0, j))],
        out_specs=pl.BlockSpec((tm, tn), lambda i, j: (i, j)),
        out_shape=jax.ShapeDtypeStruct((m, n), F32),
        scratch_shapes=[pltpu.VMEM((tm, d), BF16)],
        compiler_params=_cparams(("parallel", "arbitrary")),
        name="norm_matmul",
    )(x2d, g, w)


def _shift_rows(cur, prev8, first):
    prev_row = jnp.where(first, 0.0, prev8[7:8, :])
    rolled = pltpu.roll(cur, 1, axis=0)
    row = lax.broadcasted_iota(jnp.int32, cur.shape, 0)
    return jnp.where(row == 0, prev_row, rolled)


def _rwkv_prep_kernel(zr_ref, zrp_ref, zl_ref, zlp_ref, mur_ref, mul_ref, w0_ref, wup_ref, a0_ref,
                      aup_ref, gup_ref, kk_ref, ka_ref, rk_ref, ones_ref,
                      r_out, k_out, v_out, lw_out, as_out, bs_out, g_out, bonus_out):
    first = pl.program_id(1) == 0
    zr = zr_ref[0]
    zl = zl_ref[0]
    fr = zr + (_shift_rows(zr, zrp_ref[0], first) - zr) * mur_ref[...]
    fl = zl + (_shift_rows(zl, zlp_ref[0], first) - zl) * mul_ref[...]
    r = fr[:, 0:WIDTH_A]
    k = fr[:, WIDTH_A:2 * WIDTH_A]
    v = fr[:, 2 * WIDTH_A:3 * WIDTH_A]
    x_wa = fl[:, 0:LORA_W + LORA_A]
    x_g = fl[:, LORA_W + LORA_A:LORA_PAD]
    ones = ones_ref[...]

    w = -_softplus(-(w0_ref[...] + _dot(jnp.tanh(x_wa).astype(BF16), wup_ref[...]))) - 0.5
    a = _sigmoid(a0_ref[...] + _dot(x_wa.astype(BF16), aup_ref[...]))
    g = _dot(_sigmoid(x_g).astype(BF16), gup_ref[...])
    kk = k * kk_ref[...]
    kk = kk / jnp.maximum(jnp.sqrt(_dot((kk * kk).astype(BF16), ones)), 1e-12)
    k2 = k * (1.0 + (a - 1.0) * ka_ref[...])
    bonus = _dot((r * k2 * rk_ref[...]).astype(BF16), ones) * v

    r_out[0] = r
    k_out[0] = k2
    v_out[0] = v
    lw_out[0] = -jnp.exp(w)
    as_out[0] = -kk
    bs_out[0] = kk * a
    g_out[0] = g
    bonus_out[0] = bonus


def _rwkv_prep(zr, zl, p, ts):
    b, s, _ = zr.shape
    prev = lambda bi, i: (bi, jnp.maximum(i * (ts // 8) - 1, 0), 0)
    cur = lambda bi, i: (bi, i, 0)
    const = lambda bi, i: (0, 0)
    wr, wl = 3 * WIDTH_A, LORA_PAD
    vec = pl.BlockSpec((1, WIDTH_A), const)
    out = jax.ShapeDtypeStruct((b, s, WIDTH_A), F32)
    return pl.pallas_call(
        _rwkv_prep_kernel,
        grid=(b, s // ts),
        in_specs=[pl.BlockSpec((1, ts, wr), cur), pl.BlockSpec((1, 8, wr), prev),
                  pl.BlockSpec((1, ts, wl), cur), pl.BlockSpec((1, 8, wl), prev),
                  pl.BlockSpec((1, wr), const), pl.BlockSpec((1, wl), const),
                  vec, pl.BlockSpec((LORA_W + LORA_A, WIDTH_A), const),
                  vec, pl.BlockSpec((LORA_W + LORA_A, WIDTH_A), const),
                  pl.BlockSpec((LORA_PAD - LORA_W - LORA_A, WIDTH_A), const),
                  vec, vec, vec, pl.BlockSpec((WIDTH_A, WIDTH_A), const)],
        out_specs=[pl.BlockSpec((1, ts, WIDTH_A), cur)] * 8,
        out_shape=[out] * 8,
        compiler_params=_cparams(("parallel", "arbitrary")),
        name="rwkv_prep",
    )(zr, zr, zl, zl, p["mu_r"], p["mu_l"], p["w0"], p["w_up"], p["a0"], p["a_up"], p["g_up"],
      p["k_k"], p["k_a"], p["r_k"], p["ones"])


def _stack_heads(x):
    lo = lax.broadcasted_iota(jnp.int32, x.shape, 1) < HEAD
    return jnp.concatenate([jnp.where(lo, x, 0.0), jnp.where(lo, 0.0, x)], axis=0)


def _unstack_heads(x):
    return x[0:CHUNK, :] + x[CHUNK:2 * CHUNK, :]


def _split3(x):
    hi = x.astype(BF16)
    r1 = x - hi.astype(F32)
    mid = r1.astype(BF16)
    lo = (r1 - mid.astype(F32)).astype(BF16)
    return hi, mid, lo


def _rwkv_local_kernel(r_ref, k_ref, v_ref, lw_ref, as_ref, bs_ref, t_out, g_out, rh_out, yh_out, *, n_chunks):
    two_c = 2 * CHUNK
    row = lax.broadcasted_iota(jnp.int32, (two_c, two_c), 0)
    col = lax.broadcasted_iota(jnp.int32, (two_c, two_c), 1)
    strict = col < row
    incl = col <= row
    eye = (col == row).astype(F32)
    tri = (lax.broadcasted_iota(jnp.int32, (CHUNK, CHUNK), 1)
           <= lax.broadcasted_iota(jnp.int32, (CHUNK, CHUNK), 0)).astype(BF16)

    for c0 in range(0, n_chunks, LOCAL_WAVE):
        units = [(c, p) for c in range(c0, min(c0 + LOCAL_WAVE, n_chunks)) for p in range(N_PAIRS)]
        at = lambda ref: [ref[0, c * CHUNK:(c + 1) * CHUNK, p * PAIR:(p + 1) * PAIR] for c, p in units]
        each = lambda f, *ls: [f(*xs) for xs in zip(*ls)]
        lw = at(lw_ref)
        lg = each(lambda x: sum(_dot(tri, part) for part in _split3(x)), lw)
        lg_end = each(lambda x: x[CHUNK - 1:CHUNK, :], lg)
        e_neg = each(lambda x: jnp.exp(-x), lg)
        e_end = each(lambda x, xe: jnp.exp(xe - x), lg, lg_end)
        a_s, b_s, kk = at(as_ref), at(bs_ref), at(k_ref)
        a_t = each(lambda x, g, w: _stack_heads(x * jnp.exp(g - w)), a_s, lg, lw)
        r_t = each(lambda x, g: _stack_heads(x * jnp.exp(g)), at(r_ref), lg)
        b_t = each(lambda x, e: _stack_heads(x * e), b_s, e_neg)
        k_t = each(lambda x, e: _stack_heads(x * e), kk, e_neg)
        b_p = each(lambda x, e: _stack_heads(x * e), b_s, e_end)
        k_p = each(lambda x, e: _stack_heads(x * e), kk, e_end)
        v_s = each(_stack_heads, at(v_ref))

        l_ab = each(lambda x, y: jnp.where(strict, _mm(x, y, _NT), 0.0), a_t, b_t)
        l_ak = each(lambda x, y: jnp.where(strict, _mm(x, y, _NT), 0.0), a_t, k_t)
        l_rb = each(lambda x, y: jnp.where(incl, _mm(x, y, _NT), 0.0), r_t, b_t)
        l_rk = each(lambda x, y: jnp.where(incl, _mm(x, y, _NT), 0.0), r_t, k_t)

        inv = each(lambda x: eye + x, l_ab)
        pw = l_ab
        for _ in range(int(math.log2(CHUNK)) - 1):
            pw = each(lambda x: _mm(x, x), pw)
            inv = each(lambda x, y: x + _mm(x, y), inv, pw)

        a_h = each(_mm, inv, a_t)
        lv = each(_mm, l_ak, v_s)
        w_h = each(_mm, inv, lv)
        r_h = each(lambda x, l, y: x + _mm(l, y), r_t, l_rb, a_h)
        y_h = each(lambda l, w, l2, v: _mm(l, w) + _mm(l2, v), l_rb, w_h, l_rk, v_s)
        t_m = each(lambda ge, x, y: eye * jnp.exp(ge) + _mm(x, y, _TN), lg_end, a_h, b_p)
        g_m = each(lambda w, bp, v, kp: _mm(w, bp, _TN) + _mm(v, kp, _TN), w_h, b_p, v_s, k_p)
        for i, (c, p) in enumerate(units):
            rows = slice(c * CHUNK, (c + 1) * CHUNK)
            lanes = slice(p * PAIR, (p + 1) * PAIR)
            t_out[0, c, p] = t_m[i]
            g_out[0, c, p] = g_m[i]
            rh_out[0, rows, lanes] = _unstack_heads(r_h[i])
            yh_out[0, rows, lanes] = _unstack_heads(y_h[i])


def _rwkv_local(r, k, v, lw, a_s, b_s, ts):
    b, s, _ = r.shape
    n_chunks = ts // CHUNK
    cur = lambda bi, i: (bi, i, 0)
    mat = lambda bi, i: (bi, i, 0, 0, 0)
    seq_spec = pl.BlockSpec((1, ts, WIDTH_A), cur)
    mat_spec = pl.BlockSpec((1, n_chunks, N_PAIRS, PAIR, PAIR), mat)
    seq_shape = jax.ShapeDtypeStruct((b, s, WIDTH_A), F32)
    mat_shape = jax.ShapeDtypeStruct((b, s // CHUNK, N_PAIRS, PAIR, PAIR), F32)
    return pl.pallas_call(
        functools.partial(_rwkv_local_kernel, n_chunks=n_chunks),
        grid=(b, s // ts),
        in_specs=[seq_spec] * 6,
        out_specs=[mat_spec, mat_spec, seq_spec, seq_spec],
        out_shape=[mat_shape, mat_shape, seq_shape, seq_shape],
        compiler_params=_cparams(("parallel", "parallel")),
        name="rwkv_local",
    )(r, k, v, lw, a_s, b_s)


def _rwkv_state_kernel(t_ref, gm_ref, rh_ref, yh_ref, g_ref, bonus_ref, lng_ref, lnb_ref, ones_ref,
                       y_out, s_ref, *, n_chunks):
    @pl.when(pl.program_id(1) == 0)
    def _():
        s_ref[...] = jnp.zeros_like(s_ref)

    ones = ones_ref[...]
    state = [s_ref[p] for p in range(N_PAIRS)]
    for c in range(n_chunks):
        rows = slice(c * CHUNK, (c + 1) * CHUNK)
        ys = []
        for p in range(N_PAIRS):
            lanes = slice(p * PAIR, (p + 1) * PAIR)
            ys.append(_mm(rh_ref[0, rows, lanes], state[p], _NT) + yh_ref[0, rows, lanes])
            state[p] = _mm(state[p], t_ref[0, c, p]) + gm_ref[0, c, p]
        y = jnp.concatenate(ys, axis=1)
        mean = _dot(y.astype(BF16), ones) * (1.0 / HEAD)
        yc = y - mean
        var = _dot((yc * yc).astype(BF16), ones) * (1.0 / HEAD)
        yn = yc * lax.rsqrt(var + GN_EPS) * lng_ref[...] + lnb_ref[...]
        y_out[0, rows, :] = (yn + bonus_ref[0, rows, :]) * g_ref[0, rows, :]
    for p in range(N_PAIRS):
        s_ref[p] = state[p]


def _rwkv_state(t, gm, rh, yh, g, bonus, p, ts):
    b, s, _ = rh.shape
    n_chunks = ts // CHUNK
    cur = lambda bi, i: (bi, i, 0)
    mat = lambda bi, i: (bi, i, 0, 0, 0)
    const = lambda bi, i: (0, 0)
    seq_spec = pl.BlockSpec((1, ts, WIDTH_A), cur)
    mat_spec = pl.BlockSpec((1, n_chunks, N_PAIRS, PAIR, PAIR), mat)
    vec = pl.BlockSpec((1, WIDTH_A), const)
    return pl.pallas_call(
        functools.partial(_rwkv_state_kernel, n_chunks=n_chunks),
        grid=(b, s // ts),
        in_specs=[mat_spec, mat_spec, seq_spec, seq_spec, seq_spec, seq_spec, vec, vec,
                  pl.BlockSpec((WIDTH_A, WIDTH_A), const)],
        out_specs=seq_spec,
        out_shape=jax.ShapeDtypeStruct((b, s, WIDTH_A), F32),
        scratch_shapes=[pltpu.VMEM((N_PAIRS, PAIR, PAIR), F32)],
        compiler_params=_cparams(("parallel", "arbitrary")),
        name="rwkv_state",
    )(t, gm, rh, yh, g, bonus, p["ln_g"], p["ln_b"], p["ones"])


def _rwkv_mixer(zr, zl, p):
    r, k, v, lw, a_s, b_s, g, bonus = _rwkv_prep(zr, zl, p, ts=256)
    t, gm, rh, yh = _rwkv_local(r, k, v, lw, a_s, b_s, ts=256)
    return _rwkv_state(t, gm, rh, yh, g, bonus, p, ts=256)


def _lru_kernel(z_ref, zp_ref, cw_ref, cb_ref, wa_ref, ba_ref, wx_ref, bx_ref, lam_ref,
                o_ref, a_scr, b_scr, carry_ref, *, ts):
    i = pl.program_id(1)

    @pl.when(i == 0)
    def _():
        carry_ref[...] = jnp.zeros_like(carry_ref)

    x = z_ref[0, :, 0:WIDTH_B]
    yb = z_ref[0, :, WIDTH_B:2 * WIDTH_B]
    prev8 = jnp.where(i == 0, 0.0, zp_ref[0])
    ext = jnp.concatenate([prev8, x], axis=0)
    xc = x * cw_ref[CONV_TAPS - 1:CONV_TAPS, :] + cb_ref[...]
    for back in range(1, CONV_TAPS):
        tap = CONV_TAPS - 1 - back
        xc = xc + pltpu.roll(ext, back, axis=0)[8:, :] * cw_ref[tap:tap + 1, :]

    half = WIDTH_B // 2
    xcb = xc.astype(BF16)
    ga = jnp.concatenate([_dot(xcb[:, j * half:(j + 1) * half], wa_ref[j]) for j in range(2)], axis=1)
    gx = jnp.concatenate([_dot(xcb[:, j * half:(j + 1) * half], wx_ref[j]) for j in range(2)], axis=1)
    gate_a = _sigmoid(ga + ba_ref[...])
    gate_x = _sigmoid(gx + bx_ref[...])
    log_a = -LRU_C * gate_a * _softplus(-lam_ref[...])
    a = jnp.exp(log_a)
    mult = jnp.sqrt(jnp.maximum(-jnp.tanh(log_a) * (1.0 + a * a), 0.0))
    t_glob = i * ts + lax.broadcasted_iota(jnp.int32, (ts, WIDTH_B), 0)
    mult = jnp.where(t_glob == 0, 1.0, mult)
    a_scr[...] = a
    b_scr[...] = xc * gate_x * mult

    row8 = lax.broadcasted_iota(jnp.int32, (8, WIDTH_B), 0)

    def group_body(gi, h_prev):
        rows = pl.ds(pl.multiple_of(gi * 8, 8), 8)
        a8 = a_scr[rows, :]
        b8 = b_scr[rows, :]
        for sh in (1, 2, 4):
            ar = pltpu.roll(a8, sh, axis=0)
            br = pltpu.roll(b8, sh, axis=0)
            m = row8 >= sh
            b8 = jnp.where(m, a8 * br + b8, b8)
            a8 = jnp.where(m, a8 * ar, a8)
        h8 = a8 * h_prev + b8
        b_scr[rows, :] = h8
        return h8[7:8, :]

    carry_ref[0:1, :] = lax.fori_loop(0, ts // 8, group_body, carry_ref[0:1, :])
    h = b_scr[...]
    gelu = 0.5 * yb * (1.0 + jnp.tanh(math.sqrt(2.0 / math.pi) * (yb + 0.044715 * (yb * yb * yb))))
    o_ref[0] = h * gelu


def _lru_mixer(zb, p, ts):
    b, s, _ = zb.shape
    cur = lambda bi, i: (bi, i, 0)
    prev = lambda bi, i: (bi, jnp.maximum(i * (ts // 8) - 1, 0), 0)
    const2 = lambda bi, i: (0, 0)
    const3 = lambda bi, i: (0, 0, 0)
    vec = pl.BlockSpec((1, WIDTH_B), const2)
    half = WIDTH_B // 2
    return pl.pallas_call(
        functools.partial(_lru_kernel, ts=ts),
        grid=(b, s // ts),
        in_specs=[pl.BlockSpec((1, ts, 2 * WIDTH_B), cur), pl.BlockSpec((1, 8, WIDTH_B), prev),
                  pl.BlockSpec((CONV_TAPS, WIDTH_B), const2), vec,
                  pl.BlockSpec((2, half, half), const3), vec,
                  pl.BlockSpec((2, half, half), const3), vec, vec],
        out_specs=pl.BlockSpec((1, ts, WIDTH_B), cur),
        out_shape=jax.ShapeDtypeStruct((b, s, WIDTH_B), F32),
        scratch_shapes=[pltpu.VMEM((ts, WIDTH_B), F32), pltpu.VMEM((ts, WIDTH_B), F32),
                        pltpu.VMEM((8, WIDTH_B), F32)],
        compiler_params=_cparams(("parallel", "arbitrary")),
        name="lru_mixer",
    )(zb, zb, p["conv_w"], p["conv_b"], p["wa"], p["ba"], p["wx"], p["bx"], p["lam"])


def _attn_kernel(q_ref, kc_ref, kp_ref, vc_ref, vp_ref, bias_ref, qg_ref, kg_ref, ones_ref, o_ref, lse_ref):
    n = pl.program_id(2)
    ones = ones_ref[...]

    def head_norm(x, gain):
        ms = _dot((x * x).astype(BF16), ones) * (1.0 / HEAD)
        return x * lax.rsqrt(ms + RMS_EPS) * gain

    q = head_norm(q_ref[0], qg_ref[...]) * (HEAD ** -0.5)
    kk = head_norm(jnp.concatenate([kp_ref[0], kc_ref[0]], axis=0), kg_ref[...]).astype(BF16)
    vv = jnp.concatenate([vp_ref[0], vc_ref[0]], axis=0).astype(BF16)
    lane = lax.broadcasted_iota(jnp.int32, (QBLK, GROUP_W), 1)
    key_ok = (lax.broadcasted_iota(jnp.int32, (QBLK, 2 * QBLK), 1) >= QBLK) | (n > 0)
    out = jnp.zeros((QBLK, GROUP_W), F32)
    lse = jnp.zeros((QBLK, GROUP_W), F32)
    for h in range(HEADS_PER_GROUP):
        in_head = (lane >= h * HEAD) & (lane < (h + 1) * HEAD)
        qh = jnp.where(in_head, q, 0.0).astype(BF16)
        logits = jnp.where(key_ok, _dot_nt(qh, kk) + bias_ref[h], NEG_INF)
        m = jnp.max(logits, axis=-1, keepdims=True)
        pr = jnp.exp(logits - m)
        l = jnp.sum(pr, axis=-1, keepdims=True)
        oh = _dot(pr.astype(BF16), vv) / l
        out = jnp.where(in_head, oh, out)
        lse = jnp.where(in_head, m + jnp.log(l), lse)
    o_ref[0] = out
    lse_ref[0] = lse


def _attn_group(zqkv, bias, qg, kg, ones, gi, dil):
    b, s, w = zqkv.shape
    sd = s // dil
    nb = sd // QBLK
    zv = zqkv.reshape(b, sd, dil * w)
    per_row = w // GROUP_W
    col = lambda part: (lambda bi, r, n: (bi, n, r * per_row + part * len(GROUPS) + gi))
    colp = lambda part: (lambda bi, r, n: (bi, jnp.maximum(n - 1, 0), r * per_row + part * len(GROUPS) + gi))
    blk = lambda f: pl.BlockSpec((1, QBLK, GROUP_W), f)
    const2 = lambda bi, r, n: (0, 0)
    out_map = lambda bi, r, n: (bi, n, r)
    o, lse = pl.pallas_call(
        _attn_kernel,
        grid=(b, dil, nb),
        in_specs=[blk(col(0)), blk(col(1)), blk(colp(1)), blk(col(2)), blk(colp(2)),
                  pl.BlockSpec((HEADS_PER_GROUP, QBLK, 2 * QBLK), lambda bi, r, n: (0, 0, 0)),
                  pl.BlockSpec((1, GROUP_W), const2), pl.BlockSpec((1, GROUP_W), const2),
                  pl.BlockSpec((GROUP_W, GROUP_W), const2)],
        out_specs=[blk(out_map), blk(out_map)],
        out_shape=[jax.ShapeDtypeStruct((b, sd, dil * GROUP_W), F32)] * 2,
        compiler_params=_cparams(("parallel", "parallel", "arbitrary")),
        name=f"dilated_attn_g{gi}",
    )(zv, zv, zv, zv, zv, bias, qg, kg, ones)
    return o.reshape(b, s, GROUP_W), lse.reshape(b, s, GROUP_W)


def _t5_bucket(dist):
    max_exact = N_BUCKETS // 2
    d = jnp.maximum(dist, 0)
    large = max_exact + (jnp.log(jnp.maximum(d, 1).astype(F32) / max_exact)
                         / math.log(MAX_DISTANCE / max_exact) * (N_BUCKETS - max_exact)).astype(jnp.int32)
    large = jnp.minimum(large, N_BUCKETS - 1)
    return jnp.where(d < max_exact, d, large)


def _attn_bias_tiles(rel_bias):
    tiles = []
    kj = jnp.arange(2 * QBLK)[None, :]
    rel = (jnp.arange(QBLK)[:, None] + QBLK) - kj
    for gi, (window, dil) in enumerate(GROUPS):
        band = (rel >= 0) & (rel <= window // dil)
        tab = rel_bias.astype(F32)[:, gi * HEADS_PER_GROUP:(gi + 1) * HEADS_PER_GROUP]
        onehot = (_t5_bucket(rel * dil)[..., None] == jnp.arange(N_BUCKETS)).astype(F32)
        bias = jnp.einsum("qkn,nh->hqk", onehot, tab, precision=lax.Precision.HIGHEST)
        tiles.append(jnp.where(band[None], bias, NEG_INF))
    return tiles


def _merge_kernel(x_ref, ya_ref, yb_ref, o0_ref, o1_ref, o2_ref, l0_ref, l1_ref, l2_ref, zg_ref,
                  pa_ref, pb_ref, pc_ref, wo_ref, out_ref):
    d = x_ref.shape[-1]
    l0, l1, l2 = l0_ref[...], l1_ref[...], l2_ref[...]
    m = jnp.maximum(jnp.maximum(l0, l1), l2)
    e0, e1, e2 = jnp.exp(l0 - m), jnp.exp(l1 - m), jnp.exp(l2 - m)
    yc = (o0_ref[...] * e0 + o1_ref[...] * e1 + o2_ref[...] * e2) / (e0 + e1 + e2)
    merged = (_sigmoid(zg_ref[:, 0:d]) * _dot(ya_ref[...].astype(BF16), pa_ref[...])
              + _sigmoid(zg_ref[:, d:2 * d]) * _dot(yb_ref[...].astype(BF16), pb_ref[...])
              + _sigmoid(zg_ref[:, 2 * d:3 * d]) * _dot(yc.astype(BF16), pc_ref[...]))
    out_ref[...] = x_ref[...] + _dot(merged.astype(BF16), wo_ref[...])


def _merge(x2d, ya, yb, os_, ls_, zg, p, tm):
    m, d = x2d.shape
    row = lambda w: pl.BlockSpec((tm, w), lambda i: (i, 0))
    full = lambda a: pl.BlockSpec(a.shape, lambda i: (0, 0))
    return pl.pallas_call(
        _merge_kernel,
        grid=(m // tm,),
        in_specs=[row(d), row(WIDTH_A), row(WIDTH_B)] + [row(GROUP_W)] * 6 + [row(3 * d),
                  full(p["proj_a"]), full(p["proj_b"]), full(p["proj_c"]), full(p["w_out"])],
        out_specs=row(d),
        out_shape=jax.ShapeDtypeStruct((m, d), F32),
        compiler_params=_cparams(("parallel",)),
        name="merge",
    )(x2d, ya, yb, *os_, *ls_, zg, p["proj_a"], p["proj_b"], p["proj_c"], p["w_out"])


def _mlp_kernel(x_ref, g_ref, wu_ref, wd_ref, o_ref, h_ref, acc_ref):
    j = pl.program_id(1)

    @pl.when(j == 0)
    def _():
        x = x_ref[...]
        ms = jnp.mean(x * x, axis=-1, keepdims=True)
        h_ref[...] = (x * lax.rsqrt(ms + RMS_EPS) * g_ref[...]).astype(BF16)
        acc_ref[...] = jnp.zeros_like(acc_ref)

    u = jnp.maximum(_dot(h_ref[...], wu_ref[...]), 0.0)
    acc_ref[...] += _dot((u * u).astype(BF16), wd_ref[...])

    @pl.when(j == pl.num_programs(1) - 1)
    def _():
        o_ref[...] = x_ref[...] + acc_ref[...]


def _mlp(x2d, g, wu, wd, tm, tf):
    m, d = x2d.shape
    f = wu.shape[1]
    return pl.pallas_call(
        _mlp_kernel,
        grid=(m // tm, f // tf),
        in_specs=[pl.BlockSpec((tm, d), lambda i, j: (i, 0)),
                  pl.BlockSpec((1, d), lambda i, j: (0, 0)),
                  pl.BlockSpec((d, tf), lambda i, j: (0, j)),
                  pl.BlockSpec((tf, d), lambda i, j: (j, 0))],
        out_specs=pl.BlockSpec((tm, d), lambda i, j: (i, 0)),
        out_shape=jax.ShapeDtypeStruct((m, d), F32),
        scratch_shapes=[pltpu.VMEM((tm, d), BF16), pltpu.VMEM((tm, d), F32)],
        compiler_params=_cparams(("parallel", "arbitrary")),
        name="mlp",
    )(x2d, g, wu, wd)


def _pad_rows(w, lo, total):
    return jnp.pad(w, ((lo, total - lo - w.shape[0]), (0, 0)))


def _block_diag_halves(w):
    n, bd, _ = w.shape
    per = n // 2
    out = jnp.zeros((2, per * bd, per * bd), w.dtype)
    for i in range(n):
        j, q = divmod(i, per)
        out = out.at[j, q * bd:(q + 1) * bd, q * bd:(q + 1) * bd].set(w[i])
    return out.astype(BF16)


def _layer(x, l, bias_tiles, prm):
    (norm_mix_g, w_in, rwkv_mu, rwkv_w0, rwkv_w_up, rwkv_a0, rwkv_a_up, rwkv_g_up, rwkv_k_k, rwkv_k_a,
     rwkv_r_k, rwkv_ln_g, rwkv_ln_b, proj_a, conv_w, conv_b, lru_wa, lru_ba, lru_wx, lru_bx, lru_lambda,
     proj_b, q_norm_g, k_norm_g, proj_c, w_out, norm_mlp_g, mlp_up, mlp_down) = [t[l] for t in prm]
    b, s, d = x.shape
    x2d = x.reshape(b * s, d)
    row = lambda t: t.reshape(1, -1).astype(F32)

    c_rkv, c_lora = 3 * WIDTH_A, LORA_W + LORA_A + LORA_G
    o_l = c_rkv
    o_b = o_l + c_lora
    o_c = o_b + 2 * WIDTH_B
    o_g = o_c + 3 * WIDTH_C
    wb = w_in.astype(BF16)
    w_l = jnp.pad(wb[:, o_l:o_b], ((0, 0), (0, LORA_PAD - c_lora)))
    g_mix = row(norm_mix_g)
    zr = _norm_matmul(x2d, g_mix, wb[:, 0:o_l], 1024, 768).reshape(b, s, c_rkv)
    zl = _norm_matmul(x2d, g_mix, w_l, 1024, LORA_PAD).reshape(b, s, LORA_PAD)
    zb = _norm_matmul(x2d, g_mix, wb[:, o_b:o_c], 1024, 512).reshape(b, s, 2 * WIDTH_B)
    zc = _norm_matmul(x2d, g_mix, wb[:, o_c:o_g], 1024, 768).reshape(b, s, 3 * WIDTH_C)
    zg = _norm_matmul(x2d, g_mix, wb[:, o_g:], 1024, 768)

    ones_a = _head_ones(WIDTH_A)
    pa = dict(
        mu_r=row(rwkv_mu[0:c_rkv]),
        mu_l=jnp.pad(row(rwkv_mu[c_rkv:]), ((0, 0), (0, LORA_PAD - c_lora))),
        w0=row(rwkv_w0), a0=row(rwkv_a0),
        w_up=_pad_rows(rwkv_w_up, 0, LORA_W + LORA_A).astype(BF16),
        a_up=_pad_rows(rwkv_a_up, LORA_W, LORA_W + LORA_A).astype(BF16),
        g_up=_pad_rows(rwkv_g_up, 0, LORA_PAD - LORA_W - LORA_A).astype(BF16),
        k_k=row(rwkv_k_k), k_a=row(rwkv_k_a), r_k=row(rwkv_r_k),
        ln_g=row(rwkv_ln_g), ln_b=row(rwkv_ln_b), ones=ones_a)
    ya = _rwkv_mixer(zr, zl, pa)

    pb = dict(conv_w=conv_w.astype(F32), conv_b=row(conv_b), wa=_block_diag_halves(lru_wa), ba=row(lru_ba),
              wx=_block_diag_halves(lru_wx), bx=row(lru_bx), lam=row(lru_lambda))
    yb = _lru_mixer(zb, pb, ts=512)

    qg = jnp.tile(row(q_norm_g), (1, HEADS_PER_GROUP))
    kg = jnp.tile(row(k_norm_g), (1, HEADS_PER_GROUP))
    ones_c = _head_ones(GROUP_W)
    os_, ls_ = [], []
    for gi, (_, dil) in enumerate(GROUPS):
        o, lse = _attn_group(zc, bias_tiles[gi], qg, kg, ones_c, gi, dil)
        os_.append(o.reshape(b * s, GROUP_W))
        ls_.append(lse.reshape(b * s, GROUP_W))

    pm = dict(proj_a=proj_a.astype(BF16), proj_b=proj_b.astype(BF16), proj_c=proj_c.astype(BF16),
              w_out=w_out.astype(BF16))
    x1 = _merge(x2d, ya.reshape(b * s, WIDTH_A), yb.reshape(b * s, WIDTH_B), os_, ls_, zg, pm, 512)
    x2 = _mlp(x1, row(norm_mlp_g), mlp_up.astype(BF16), mlp_down.astype(BF16), 1024, 512)
    return x2.reshape(b, s, d)


def kernel(x, rel_bias, norm_mix_g, w_in, rwkv_mu, rwkv_w0, rwkv_w_up, rwkv_a0, rwkv_a_up, rwkv_g_up, rwkv_k_k, rwkv_k_a, rwkv_r_k, rwkv_ln_g, rwkv_ln_b, proj_a, conv_w, conv_b, lru_wa, lru_ba, lru_wx, lru_bx, lru_lambda, proj_b, q_norm_g, k_norm_g, proj_c, w_out, norm_mlp_g, mlp_up, mlp_down):
    prm = (norm_mix_g, w_in, rwkv_mu, rwkv_w0, rwkv_w_up, rwkv_a0, rwkv_a_up, rwkv_g_up, rwkv_k_k, rwkv_k_a,
           rwkv_r_k, rwkv_ln_g, rwkv_ln_b, proj_a, conv_w, conv_b, lru_wa, lru_ba, lru_wx, lru_bx, lru_lambda,
           proj_b, q_norm_g, k_norm_g, proj_c, w_out, norm_mlp_g, mlp_up, mlp_down)
    bias_tiles = _attn_bias_tiles(rel_bias)
    x = x.astype(F32)
    for l in range(norm_mix_g.shape[0]):
        x = _layer(x, l, bias_tiles, prm)
    return x
```

```python
import functools
import math

import jax
import jax.numpy as jnp
from jax import lax
from jax.experimental import pallas as pl
from jax.experimental.pallas import tpu as pltpu

F32 = jnp.float32
BF16 = jnp.bfloat16

N_HEADS_A = 8
HEAD = 64
PAIR = 2 * HEAD
WIDTH_A = N_HEADS_A * HEAD
N_PAIRS = WIDTH_A // PAIR
CHUNK = 64
LOCAL_WAVE = 2
LORA_W, LORA_A, LORA_G = 64, 64, 160
LORA_PAD = 384
GN_EPS = 64e-5
WIDTH_B = 512
LRU_BLOCK = 64
CONV_TAPS = 4
LRU_C = 8.0
GROUPS = ((128, 1), (512, 4), (2048, 16))
HEADS_PER_GROUP = 4
GROUP_W = HEADS_PER_GROUP * HEAD
WIDTH_C = len(GROUPS) * GROUP_W
QBLK = 128
ATTN_TILE = 2048
ATTN_WAVE = 4
N_BUCKETS = 32
MAX_DISTANCE = 2048
NEG_INF = -1e30
RMS_EPS = 1e-6
VMEM_LIMIT = 56 * 1024 * 1024


def _cparams(sem):
    return pltpu.CompilerParams(dimension_semantics=sem, vmem_limit_bytes=VMEM_LIMIT)


def _dot(a, b):
    return jnp.dot(a, b, preferred_element_type=F32)


def _dot_nt(a, b):
    return lax.dot_general(a, b, (((1,), (1,)), ((), ())), preferred_element_type=F32)


_NN = (((1,), (0,)), ((), ()))
_NT = (((1,), (1,)), ((), ()))
_TN = (((0,), (0,)), ((), ()))


def _mm(a, b, dims=_NN):
    return lax.dot_general(a.astype(BF16), b.astype(BF16), dims, preferred_element_type=F32)


def _sigmoid(x):
    return 1.0 / (1.0 + jnp.exp(-x))


def _softplus(x):
    return jnp.maximum(x, 0.0) + jnp.log1p(jnp.exp(-jnp.abs(x)))


def _head_ones(width):
    i = jnp.arange(width) // HEAD
    return (i[:, None] == i[None, :]).astype(BF16)


def _norm_matmul_kernel(x_ref, g_ref, w_ref, o_ref, h_ref):
    @pl.when(pl.program_id(1) == 0)
    def _():
        x = x_ref[...]
        ms = jnp.mean(x * x, axis=-1, keepdims=True)
        h_ref[...] = (x * lax.rsqrt(ms + RMS_EPS) * g_ref[...]).astype(BF16)

    o_ref[...] = _dot(h_ref[...], w_ref[...]).astype(o_ref.dtype)


def _norm_matmul(x2d, g, w, tm, tn):
    m, d = x2d.shape
    n = w.shape[1]
    return pl.pallas_call(
        _norm_matmul_kernel,
        grid=(m // tm, n // tn),
        in_specs=[pl.BlockSpec((tm, d), lambda i, j: (i, 0)),
                  pl.BlockSpec((1, d), lambda i, j: (0, 0)),
                  pl.BlockSpec((d, tn), lambda i, j: (0, j))],
        out_specs=pl.BlockSpec((tm, tn), lambda i, j: (i, j)),
        out_shape=jax.ShapeDtypeStruct((m, n), F32),
        scratch_shapes=[pltpu.VMEM((tm, d), BF16)],
        compiler_params=_cparams(("parallel", "arbitrary")),
        name="norm_matmul",
    )(x2d, g, w)


def _shift_rows(cur, prev8, first):
    prev_row = jnp.where(first, 0.0, prev8[7:8, :])
    rolled = pltpu.roll(cur, 1, axis=0)
    row = lax.broadcasted_iota(jnp.int32, cur.shape, 0)
    return jnp.where(row == 0, prev_row, rolled)


def _rwkv_prep_kernel(zr_ref, zrp_ref, zl_ref, zlp_ref, mur_ref, mul_ref, w0_ref, wup_ref, a0_ref,
                      aup_ref, gup_ref, kk_ref, ka_ref, rk_ref, ones_ref,
                      r_out, k_out, v_out, lw_out, as_out, bs_out, g_out, bonus_out):
    first = pl.program_id(1) == 0
    zr = zr_ref[0]
    zl = zl_ref[0]
    fr = zr + (_shift_rows(zr, zrp_ref[0], first) - zr) * mur_ref[...]
    fl = zl + (_shift_rows(zl, zlp_ref[0], first) - zl) * mul_ref[...]
    r = fr[:, 0:WIDTH_A]
    k = fr[:, WIDTH_A:2 * WIDTH_A]
    v = fr[:, 2 * WIDTH_A:3 * WIDTH_A]
    x_wa = fl[:, 0:LORA_W + LORA_A]
    x_g = fl[:, LORA_W + LORA_A:LORA_PAD]
    ones = ones_ref[...]

    w = -_softplus(-(w0_ref[...] + _dot(jnp.tanh(x_wa).astype(BF16), wup_ref[...]))) - 0.5
    a = _sigmoid(a0_ref[...] + _dot(x_wa.astype(BF16), aup_ref[...]))
    g = _dot(_sigmoid(x_g).astype(BF16), gup_ref[...])
    kk = k * kk_ref[...]
    kk = kk / jnp.maximum(jnp.sqrt(_dot((kk * kk).astype(BF16), ones)), 1e-12)
    k2 = k * (1.0 + (a - 1.0) * ka_ref[...])
    bonus = _dot((r * k2 * rk_ref[...]).astype(BF16), ones) * v

    r_out[0] = r
    k_out[0] = k2
    v_out[0] = v
    lw_out[0] = -jnp.exp(w)
    as_out[0] = -kk
    bs_out[0] = kk * a
    g_out[0] = g
    bonus_out[0] = bonus


def _rwkv_prep(z3, p, ts):
    b, s, _ = z3.shape
    wr, wl = 3 * WIDTH_A, LORA_PAD
    lora_block = wr // wl
    prev = lambda c: (lambda bi, i: (bi, jnp.maximum(i * (ts // 8) - 1, 0), c))
    cur = lambda bi, i: (bi, i, 0)
    const = lambda bi, i: (0, 0)
    vec = pl.BlockSpec((1, WIDTH_A), const)
    out = jax.ShapeDtypeStruct((b, s, WIDTH_A), F32)
    return pl.pallas_call(
        _rwkv_prep_kernel,
        grid=(b, s // ts),
        in_specs=[pl.BlockSpec((1, ts, wr), cur), pl.BlockSpec((1, 8, wr), prev(0)),
                  pl.BlockSpec((1, ts, wl), lambda bi, i: (bi, i, lora_block)),
                  pl.BlockSpec((1, 8, wl), prev(lora_block)),
                  pl.BlockSpec((1, wr), const), pl.BlockSpec((1, wl), const),
                  vec, pl.BlockSpec((LORA_W + LORA_A, WIDTH_A), const),
                  vec, pl.BlockSpec((LORA_W + LORA_A, WIDTH_A), const),
                  pl.BlockSpec((LORA_PAD - LORA_W - LORA_A, WIDTH_A), const),
                  vec, vec, vec, pl.BlockSpec((WIDTH_A, WIDTH_A), const)],
        out_specs=[pl.BlockSpec((1, ts, WIDTH_A), cur)] * 8,
        out_shape=[out] * 8,
        compiler_params=_cparams(("parallel", "arbitrary")),
        name="rwkv_prep",
    )(z3, z3, z3, z3, p["mu_r"], p["mu_l"], p["w0"], p["w_up"], p["a0"], p["a_up"], p["g_up"],
      p["k_k"], p["k_a"], p["r_k"], p["ones"])


def _stack_heads(x):
    lo = lax.broadcasted_iota(jnp.int32, x.shape, 1) < HEAD
    return jnp.concatenate([jnp.where(lo, x, 0.0), jnp.where(lo, 0.0, x)], axis=0)


def _unstack_heads(x):
    return x[0:CHUNK, :] + x[CHUNK:2 * CHUNK, :]


def _split3(x):
    hi = x.astype(BF16)
    r1 = x - hi.astype(F32)
    mid = r1.astype(BF16)
    lo = (r1 - mid.astype(F32)).astype(BF16)
    return hi, mid, lo


def _rwkv_local_kernel(r_ref, k_ref, v_ref, lw_ref, as_ref, bs_ref, t_out, g_out, rh_out, yh_out, *, n_chunks):
    two_c = 2 * CHUNK
    row = lax.broadcasted_iota(jnp.int32, (two_c, two_c), 0)
    col = lax.broadcasted_iota(jnp.int32, (two_c, two_c), 1)
    strict = col < row
    incl = col <= row
    eye = (col == row).astype(F32)
    tri = (lax.broadcasted_iota(jnp.int32, (CHUNK, CHUNK), 1)
           <= lax.broadcasted_iota(jnp.int32, (CHUNK, CHUNK), 0)).astype(BF16)

    for c0 in range(0, n_chunks, LOCAL_WAVE):
        units = [(c, p) for c in range(c0, min(c0 + LOCAL_WAVE, n_chunks)) for p in range(N_PAIRS)]
        at = lambda ref: [ref[0, c * CHUNK:(c + 1) * CHUNK, p * PAIR:(p + 1) * PAIR] for c, p in units]
        each = lambda f, *ls: [f(*xs) for xs in zip(*ls)]
        lw = at(lw_ref)
        lg = each(lambda x: sum(_dot(tri, part) for part in _split3(x)), lw)
        lg_end = each(lambda x: x[CHUNK - 1:CHUNK, :], lg)
        e_neg = each(lambda x: jnp.exp(-x), lg)
        e_end = each(lambda x, xe: jnp.exp(xe - x), lg, lg_end)
        a_s, b_s, kk = at(as_ref), at(bs_ref), at(k_ref)
        a_t = each(lambda x, g, w: _stack_heads(x * jnp.exp(g - w)), a_s, lg, lw)
        r_t = each(lambda x, g: _stack_heads(x * jnp.exp(g)), at(r_ref), lg)
        b_t = each(lambda x, e: _stack_heads(x * e), b_s, e_neg)
        k_t = each(lambda x, e: _stack_heads(x * e), kk, e_neg)
        b_p = each(lambda x, e: _stack_heads(x * e), b_s, e_end)
        k_p = each(lambda x, e: _stack_heads(x * e), kk, e_end)
        v_s = each(_stack_heads, at(v_ref))

        vcat = lambda x, y: jnp.concatenate([x, y], axis=0)
        hcat = lambda x, y: jnp.concatenate([x, y], axis=1)
        top = lambda x: x[0:two_c]
        bot = lambda x: x[two_c:2 * two_c]
        left = lambda x: x[:, 0:two_c]
        right = lambda x: x[:, two_c:2 * two_c]

        prod = each(lambda a, r, b, k: _mm(vcat(a, r), vcat(b, k), _NT), a_t, r_t, b_t, k_t)
        l_ab = each(lambda x: jnp.where(strict, left(top(x)), 0.0), prod)
        l_ak = each(lambda x: jnp.where(strict, right(top(x)), 0.0), prod)
        l_rb = each(lambda x: jnp.where(incl, left(bot(x)), 0.0), prod)
        l_rk = each(lambda x: jnp.where(incl, right(bot(x)), 0.0), prod)

        inv = each(lambda x: eye + x, l_ab)
        pw = each(lambda x: _mm(x, x), l_ab)
        for _ in range(int(math.log2(CHUNK)) - 2):
            both = each(lambda p, x: _mm(vcat(p, x), p), pw, inv)
            pw = each(top, both)
            inv = each(lambda x, y: x + bot(y), inv, both)
        inv = each(lambda x, p: x + _mm(x, p), inv, pw)

        lv_rkv = each(lambda l1, l2, v: _mm(vcat(l1, l2), v), l_ak, l_rk, v_s)
        aw_h = each(lambda m, a, x: _mm(m, hcat(a, top(x))), inv, a_t, lv_rkv)
        l_aw = each(_mm, l_rb, aw_h)
        r_h = each(lambda x, y: x + left(y), r_t, l_aw)
        y_h = each(lambda y, x: right(y) + bot(x), l_aw, lv_rkv)
        tg = each(lambda x, bp: _mm(x, bp, _TN), aw_h, b_p)
        t_m = each(lambda ge, x: eye * jnp.exp(ge) + top(x), lg_end, tg)
        g_m = each(lambda x, v, kp: bot(x) + _mm(v, kp, _TN), tg, v_s, k_p)
        for i, (c, p) in enumerate(units):
            rows = slice(c * CHUNK, (c + 1) * CHUNK)
            lanes = slice(p * PAIR, (p + 1) * PAIR)
            t_out[0, c, p] = t_m[i]
            g_out[0, c, p] = g_m[i]
            rh_out[0, rows, lanes] = _unstack_heads(r_h[i])
            yh_out[0, rows, lanes] = _unstack_heads(y_h[i])


def _rwkv_local(r, k, v, lw, a_s, b_s, ts):
    b, s, _ = r.shape
    n_chunks = ts // CHUNK
    cur = lambda bi, i: (bi, i, 0)
    mat = lambda bi, i: (bi, i, 0, 0, 0)
    seq_spec = pl.BlockSpec((1, ts, WIDTH_A), cur)
    mat_spec = pl.BlockSpec((1, n_chunks, N_PAIRS, PAIR, PAIR), mat)
    seq_shape = jax.ShapeDtypeStruct((b, s, WIDTH_A), F32)
    mat_shape = jax.ShapeDtypeStruct((b, s // CHUNK, N_PAIRS, PAIR, PAIR), F32)
    return pl.pallas_call(
        functools.partial(_rwkv_local_kernel, n_chunks=n_chunks),
        grid=(b, s // ts),
        in_specs=[seq_spec] * 6,
        out_specs=[mat_spec, mat_spec, seq_spec, seq_spec],
        out_shape=[mat_shape, mat_shape, seq_shape, seq_shape],
        compiler_params=_cparams(("parallel", "parallel")),
        name="rwkv_local",
    )(r, k, v, lw, a_s, b_s)


def _rwkv_state_kernel(t_ref, gm_ref, rh_ref, yh_ref, g_ref, bonus_ref, lng_ref, lnb_ref, ones_ref,
                       y_out, s_ref, *, n_chunks):
    @pl.when(pl.program_id(1) == 0)
    def _():
        s_ref[...] = jnp.zeros_like(s_ref)

    ones = ones_ref[...]
    state = [s_ref[p] for p in range(N_PAIRS)]
    for c in range(n_chunks):
        rows = slice(c * CHUNK, (c + 1) * CHUNK)
        ys = []
        for p in range(N_PAIRS):
            lanes = slice(p * PAIR, (p + 1) * PAIR)
            ys.append(_mm(rh_ref[0, rows, lanes], state[p], _NT) + yh_ref[0, rows, lanes])
            state[p] = _mm(state[p], t_ref[0, c, p]) + gm_ref[0, c, p]
        y = jnp.concatenate(ys, axis=1)
        mean = _dot(y.astype(BF16), ones) * (1.0 / HEAD)
        yc = y - mean
        var = _dot((yc * yc).astype(BF16), ones) * (1.0 / HEAD)
        yn = yc * lax.rsqrt(var + GN_EPS) * lng_ref[...] + lnb_ref[...]
        y_out[0, rows, :] = (yn + bonus_ref[0, rows, :]) * g_ref[0, rows, :]
    for p in range(N_PAIRS):
        s_ref[p] = state[p]


def _rwkv_state(t, gm, rh, yh, g, bonus, p, ts):
    b, s, _ = rh.shape
    n_chunks = ts // CHUNK
    cur = lambda bi, i: (bi, i, 0)
    mat = lambda bi, i: (bi, i, 0, 0, 0)
    const = lambda bi, i: (0, 0)
    seq_spec = pl.BlockSpec((1, ts, WIDTH_A), cur)
    mat_spec = pl.BlockSpec((1, n_chunks, N_PAIRS, PAIR, PAIR), mat)
    vec = pl.BlockSpec((1, WIDTH_A), const)
    return pl.pallas_call(
        functools.partial(_rwkv_state_kernel, n_chunks=n_chunks),
        grid=(b, s // ts),
        in_specs=[mat_spec, mat_spec, seq_spec, seq_spec, seq_spec, seq_spec, vec, vec,
                  pl.BlockSpec((WIDTH_A, WIDTH_A), const)],
        out_specs=seq_spec,
        out_shape=jax.ShapeDtypeStruct((b, s, WIDTH_A), F32),
        scratch_shapes=[pltpu.VMEM((N_PAIRS, PAIR, PAIR), F32)],
        compiler_params=_cparams(("parallel", "arbitrary")),
        name="rwkv_state",
    )(t, gm, rh, yh, g, bonus, p["ln_g"], p["ln_b"], p["ones"])


def _rwkv_mixer(z3, p):
    r, k, v, lw, a_s, b_s, g, bonus = _rwkv_prep(z3, p, ts=256)
    t, gm, rh, yh = _rwkv_local(r, k, v, lw, a_s, b_s, ts=256)
    return _rwkv_state(t, gm, rh, yh, g, bonus, p, ts=256)


def _lru_kernel(x_ref, zp_ref, y_ref, cw_ref, cb_ref, wa_ref, ba_ref, wx_ref, bx_ref, lam_ref,
                o_ref, a_scr, b_scr, carry_ref, *, ts):
    i = pl.program_id(1)

    @pl.when(i == 0)
    def _():
        carry_ref[...] = jnp.zeros_like(carry_ref)

    x = x_ref[0]
    yb = y_ref[0]
    prev8 = jnp.where(i == 0, 0.0, zp_ref[0])
    ext = jnp.concatenate([prev8, x], axis=0)
    xc = x * cw_ref[CONV_TAPS - 1:CONV_TAPS, :] + cb_ref[...]
    for back in range(1, CONV_TAPS):
        tap = CONV_TAPS - 1 - back
        xc = xc + pltpu.roll(ext, back, axis=0)[8:, :] * cw_ref[tap:tap + 1, :]

    half = WIDTH_B // 2
    xcb = xc.astype(BF16)
    ga = jnp.concatenate([_dot(xcb[:, j * half:(j + 1) * half], wa_ref[j]) for j in range(2)], axis=1)
    gx = jnp.concatenate([_dot(xcb[:, j * half:(j + 1) * half], wx_ref[j]) for j in range(2)], axis=1)
    gate_a = _sigmoid(ga + ba_ref[...])
    gate_x = _sigmoid(gx + bx_ref[...])
    log_a = -LRU_C * gate_a * _softplus(-lam_ref[...])
    a = jnp.exp(log_a)
    mult = jnp.sqrt(jnp.maximum(-jnp.tanh(log_a) * (1.0 + a * a), 0.0))
    t_glob = i * ts + lax.broadcasted_iota(jnp.int32, (ts, WIDTH_B), 0)
    mult = jnp.where(t_glob == 0, 1.0, mult)
    a_scr[...] = a
    b_scr[...] = xc * gate_x * mult

    row8 = lax.broadcasted_iota(jnp.int32, (8, WIDTH_B), 0)

    def group_body(gi, h_prev):
        rows = pl.ds(pl.multiple_of(gi * 8, 8), 8)
        a8 = a_scr[rows, :]
        b8 = b_scr[rows, :]
        for sh in (1, 2, 4):
            ar = pltpu.roll(a8, sh, axis=0)
            br = pltpu.roll(b8, sh, axis=0)
            m = row8 >= sh
            b8 = jnp.where(m, a8 * br + b8, b8)
            a8 = jnp.where(m, a8 * ar, a8)
        h8 = a8 * h_prev + b8
        b_scr[rows, :] = h8
        return h8[7:8, :]

    carry_ref[0:1, :] = lax.fori_loop(0, ts // 8, group_body, carry_ref[0:1, :])
    h = b_scr[...]
    gelu = 0.5 * yb * (1.0 + jnp.tanh(math.sqrt(2.0 / math.pi) * (yb + 0.044715 * (yb * yb * yb))))
    o_ref[0] = h * gelu


def _lru_mixer(z3, x_block, p, ts):
    b, s, _ = z3.shape
    cur = lambda bi, i: (bi, i, 0)
    const2 = lambda bi, i: (0, 0)
    const3 = lambda bi, i: (0, 0, 0)
    vec = pl.BlockSpec((1, WIDTH_B), const2)
    half = WIDTH_B // 2
    return pl.pallas_call(
        functools.partial(_lru_kernel, ts=ts),
        grid=(b, s // ts),
        in_specs=[pl.BlockSpec((1, ts, WIDTH_B), lambda bi, i: (bi, i, x_block)),
                  pl.BlockSpec((1, 8, WIDTH_B), lambda bi, i: (bi, jnp.maximum(i * (ts // 8) - 1, 0), x_block)),
                  pl.BlockSpec((1, ts, WIDTH_B), lambda bi, i: (bi, i, x_block + 1)),
                  pl.BlockSpec((CONV_TAPS, WIDTH_B), const2), vec,
                  pl.BlockSpec((2, half, half), const3), vec,
                  pl.BlockSpec((2, half, half), const3), vec, vec],
        out_specs=pl.BlockSpec((1, ts, WIDTH_B), cur),
        out_shape=jax.ShapeDtypeStruct((b, s, WIDTH_B), F32),
        scratch_shapes=[pltpu.VMEM((ts, WIDTH_B), F32), pltpu.VMEM((ts, WIDTH_B), F32),
                        pltpu.VMEM((8, WIDTH_B), F32)],
        compiler_params=_cparams(("parallel", "arbitrary")),
        name="lru_mixer",
    )(z3, z3, z3, p["conv_w"], p["conv_b"], p["wa"], p["ba"], p["wx"], p["bx"], p["lam"])


def _attn_kernel(q0, q1, k0, k1, kp0, kp1, v0, v1, vp0, vp1, bias_ref, qg_ref, kg_ref, ones_ref,
                 o0, o1, l0, l1, *, dil, n_sub):
    j = pl.program_id(1)
    ones = ones_ref[...]
    lane = lax.broadcasted_iota(jnp.int32, (QBLK, GROUP_W), 1)
    in_head = [(lane >= h * HEAD) & (lane < (h + 1) * HEAD) for h in range(HEADS_PER_GROUP)]
    prev_valid = (lax.broadcasted_iota(jnp.int32, (HEADS_PER_GROUP * QBLK, 2 * QBLK), 1) >= QBLK) | (j > 0)

    def rows(start):
        return pl.ds(start, QBLK, stride=dil) if dil > 1 else pl.ds(start, QBLK)

    def take(lo, hi, start):
        return jnp.concatenate([lo[0, rows(start), :], hi[0, rows(start), :]], axis=1)

    def head_sumsq(x):
        return _dot((x * x).astype(BF16), ones) * (1.0 / HEAD)

    def select_heads(x):
        out = jnp.zeros((QBLK, GROUP_W), F32)
        for h, m in enumerate(in_head):
            out = jnp.where(m, x[h * QBLK:(h + 1) * QBLK, :], out)
        return out

    def wave(units):
        each = lambda f, *ls: [f(*xs) for xs in zip(*ls)]
        span = dil * QBLK
        starts = [s for s, _ in units]
        before = [s - span if dil > 1 or isinstance(s, int) else pl.multiple_of(s - span, QBLK) for s in starts]
        q_raw = [take(q0, q1, s) for s in starts]
        k_raw = [jnp.concatenate([take(kp0, kp1, s) if far else take(k0, k1, p), take(k0, k1, s)], axis=0)
                 for (s, far), p in zip(units, before)]
        vv = [jnp.concatenate([take(vp0, vp1, s) if far else take(v0, v1, p), take(v0, v1, s)],
                              axis=0).astype(BF16) for (s, far), p in zip(units, before)]
        q_ms = each(head_sumsq, q_raw)
        k_ms = each(head_sumsq, k_raw)
        q = each(lambda x, ms: x * lax.rsqrt(ms + RMS_EPS) * qg_ref[...] * (HEAD ** -0.5), q_raw, q_ms)
        kk = each(lambda x, ms: (x * lax.rsqrt(ms + RMS_EPS) * kg_ref[...]).astype(BF16), k_raw, k_ms)
        qs = each(lambda x: jnp.concatenate([jnp.where(m, x, 0.0) for m in in_head], axis=0).astype(BF16), q)
        logits = each(lambda a, b: _dot_nt(a, b) + bias_ref[...], qs, kk)
        logits = [jnp.where(prev_valid, lg, NEG_INF) if far else lg for lg, (_, far) in zip(logits, units)]
        mx = each(lambda lg: jnp.max(lg, axis=-1, keepdims=True), logits)
        pr = each(lambda lg, m: jnp.exp(lg - m), logits, mx)
        den = each(lambda p: jnp.sum(p, axis=-1, keepdims=True), pr)
        pv = each(lambda p, v, dn: _dot(p.astype(BF16), v) / dn, pr, vv, den)
        out = each(select_heads, pv)
        lse = each(lambda m, dn: select_heads(jnp.broadcast_to(m + jnp.log(dn), (HEADS_PER_GROUP * QBLK, GROUP_W))),
                   mx, den)
        for s, o, l in zip(starts, out, lse):
            o0[0, rows(s), :] = o[:, 0:PAIR]
            o1[0, rows(s), :] = o[:, PAIR:GROUP_W]
            l0[0, rows(s), :] = l[:, 0:PAIR]
            l1[0, rows(s), :] = l[:, PAIR:GROUP_W]

    def loop(lo, hi, body):
        def step(i, carry):
            body(i)
            return carry
        lax.fori_loop(lo, hi, step, 0)

    span = dil * QBLK
    if n_sub == 1:
        loop(0, dil // ATTN_WAVE, lambda i: wave([(i * ATTN_WAVE + u, True) for u in range(ATTN_WAVE)]))
    elif dil > 1:
        wave([(r, True) for r in range(dil)])
        loop(1, n_sub, lambda n: wave([(r + n * span, False) for r in range(dil)]))
    else:
        wave([(u * QBLK, u == 0) for u in range(ATTN_WAVE)])
        loop(1, n_sub // ATTN_WAVE,
             lambda i: wave([(pl.multiple_of((i * ATTN_WAVE + u) * QBLK, QBLK), False) for u in range(ATTN_WAVE)]))


def _attn_group(z3, col0, bias, qg, kg, ones, gi, dil):
    b, s, _ = z3.shape
    span = dil * QBLK
    n_sub = ATTN_TILE // span
    halves = GROUP_W // PAIR
    per_part = len(GROUPS) * halves

    def cur(part, half):
        c = col0 + part * per_part + gi * halves + half
        return pl.BlockSpec((1, ATTN_TILE, PAIR), lambda bi, j: (bi, j, c))

    def prev(part, half):
        c = col0 + part * per_part + gi * halves + half
        return pl.BlockSpec((1, span, PAIR), lambda bi, j: (bi, jnp.maximum(j * n_sub - 1, 0), c))

    const2 = lambda bi, j: (0, 0)
    out_spec = pl.BlockSpec((1, ATTN_TILE, PAIR), lambda bi, j: (bi, j, 0))
    return pl.pallas_call(
        functools.partial(_attn_kernel, dil=dil, n_sub=n_sub),
        grid=(b, s // ATTN_TILE),
        in_specs=[cur(0, 0), cur(0, 1), cur(1, 0), cur(1, 1), prev(1, 0), prev(1, 1),
                  cur(2, 0), cur(2, 1), prev(2, 0), prev(2, 1),
                  pl.BlockSpec((HEADS_PER_GROUP * QBLK, 2 * QBLK), const2),
                  pl.BlockSpec((1, GROUP_W), const2), pl.BlockSpec((1, GROUP_W), const2),
                  pl.BlockSpec((GROUP_W, GROUP_W), const2)],
        out_specs=[out_spec] * 4,
        out_shape=[jax.ShapeDtypeStruct((b, s, PAIR), F32)] * 4,
        compiler_params=_cparams(("parallel", "arbitrary")),
        name=f"dilated_attn_g{gi}",
    )(*([z3] * 10), bias, qg, kg, ones)


def _t5_bucket(dist):
    max_exact = N_BUCKETS // 2
    d = jnp.maximum(dist, 0)
    large = max_exact + (jnp.log(jnp.maximum(d, 1).astype(F32) / max_exact)
                         / math.log(MAX_DISTANCE / max_exact) * (N_BUCKETS - max_exact)).astype(jnp.int32)
    large = jnp.minimum(large, N_BUCKETS - 1)
    return jnp.where(d < max_exact, d, large)


def _attn_bias_tiles(rel_bias):
    tiles = []
    kj = jnp.arange(2 * QBLK)[None, :]
    rel = (jnp.arange(QBLK)[:, None] + QBLK) - kj
    for gi, (window, dil) in enumerate(GROUPS):
        band = (rel >= 0) & (rel <= window // dil)
        tab = rel_bias.astype(F32)[:, gi * HEADS_PER_GROUP:(gi + 1) * HEADS_PER_GROUP]
        onehot = (_t5_bucket(rel * dil)[..., None] == jnp.arange(N_BUCKETS)).astype(F32)
        bias = jnp.einsum("qkn,nh->hqk", onehot, tab, precision=lax.Precision.HIGHEST)
        tiles.append(jnp.where(band[None], bias, NEG_INF).reshape(HEADS_PER_GROUP * QBLK, 2 * QBLK))
    return tiles


def _merge_kernel(x_ref, ya_ref, yb_ref, *rest):
    n_g = len(GROUPS)
    attn = rest[:4 * n_g]
    zg_ref, pa_ref, pb_ref, pc_ref, wo_ref, out_ref = rest[4 * n_g:]
    d = x_ref.shape[-1]
    outs = [jnp.concatenate([attn[4 * g][...], attn[4 * g + 1][...]], axis=1) for g in range(n_g)]
    lses = [jnp.concatenate([attn[4 * g + 2][...], attn[4 * g + 3][...]], axis=1) for g in range(n_g)]
    m = functools.reduce(jnp.maximum, lses)
    es = [jnp.exp(l - m) for l in lses]
    yc = sum(o * e for o, e in zip(outs, es)) / sum(es)
    merged = (_sigmoid(zg_ref[:, 0:d]) * _dot(ya_ref[...].astype(BF16), pa_ref[...])
              + _sigmoid(zg_ref[:, d:2 * d]) * _dot(yb_ref[...].astype(BF16), pb_ref[...])
              + _sigmoid(zg_ref[:, 2 * d:3 * d]) * _dot(yc.astype(BF16), pc_ref[...]))
    out_ref[...] = x_ref[...] + _dot(merged.astype(BF16), wo_ref[...])


def _merge(x2d, ya, yb, attn, z2d, gate_block, p, tm):
    m, d = x2d.shape
    row = lambda w: pl.BlockSpec((tm, w), lambda i: (i, 0))
    full = lambda a: pl.BlockSpec(a.shape, lambda i: (0, 0))
    return pl.pallas_call(
        _merge_kernel,
        grid=(m // tm,),
        in_specs=[row(d), row(WIDTH_A), row(WIDTH_B)] + [row(PAIR)] * len(attn)
                 + [pl.BlockSpec((tm, 3 * d), lambda i: (i, gate_block)),
                    full(p["proj_a"]), full(p["proj_b"]), full(p["proj_c"]), full(p["w_out"])],
        out_specs=row(d),
        out_shape=jax.ShapeDtypeStruct((m, d), F32),
        compiler_params=_cparams(("parallel",)),
        name="merge",
    )(x2d, ya, yb, *attn, z2d, p["proj_a"], p["proj_b"], p["proj_c"], p["w_out"])


def _mlp_kernel(x_ref, g_ref, wu_ref, wd_ref, o_ref, h_ref, acc_ref):
    j = pl.program_id(1)

    @pl.when(j == 0)
    def _():
        x = x_ref[...]
        ms = jnp.mean(x * x, axis=-1, keepdims=True)
        h_ref[...] = (x * lax.rsqrt(ms + RMS_EPS) * g_ref[...]).astype(BF16)
        acc_ref[...] = jnp.zeros_like(acc_ref)

    u = jnp.maximum(_dot(h_ref[...], wu_ref[...]), 0.0)
    acc_ref[...] += _dot((u * u).astype(BF16), wd_ref[...])

    @pl.when(j == pl.num_programs(1) - 1)
    def _():
        o_ref[...] = x_ref[...] + acc_ref[...]


def _mlp(x2d, g, wu, wd, tm, tf):
    m, d = x2d.shape
    f = wu.shape[1]
    return pl.pallas_call(
        _mlp_kernel,
        grid=(m // tm, f // tf),
        in_specs=[pl.BlockSpec((tm, d), lambda i, j: (i, 0)),
                  pl.BlockSpec((1, d), lambda i, j: (0, 0)),
                  pl.BlockSpec((d, tf), lambda i, j: (0, j)),
                  pl.BlockSpec((tf, d), lambda i, j: (j, 0))],
        out_specs=pl.BlockSpec((tm, d), lambda i, j: (i, 0)),
        out_shape=jax.ShapeDtypeStruct((m, d), F32),
        scratch_shapes=[pltpu.VMEM((tm, d), BF16), pltpu.VMEM((tm, d), F32)],
        compiler_params=_cparams(("parallel", "arbitrary")),
        name="mlp",
    )(x2d, g, wu, wd)


def _pad_rows(w, lo, total):
    return jnp.pad(w, ((lo, total - lo - w.shape[0]), (0, 0)))


def _block_diag_halves(w):
    n, bd, _ = w.shape
    per = n // 2
    out = jnp.zeros((2, per * bd, per * bd), w.dtype)
    for i in range(n):
        j, q = divmod(i, per)
        out = out.at[j, q * bd:(q + 1) * bd, q * bd:(q + 1) * bd].set(w[i])
    return out.astype(BF16)


def _layer(x, l, bias_tiles, prm):
    (norm_mix_g, w_in, rwkv_mu, rwkv_w0, rwkv_w_up, rwkv_a0, rwkv_a_up, rwkv_g_up, rwkv_k_k, rwkv_k_a,
     rwkv_r_k, rwkv_ln_g, rwkv_ln_b, proj_a, conv_w, conv_b, lru_wa, lru_ba, lru_wx, lru_bx, lru_lambda,
     proj_b, q_norm_g, k_norm_g, proj_c, w_out, norm_mlp_g, mlp_up, mlp_down) = [t[l] for t in prm]
    b, s, d = x.shape
    x2d = x.reshape(b * s, d)
    row = lambda t: t.reshape(1, -1).astype(F32)

    c_rkv, c_lora = 3 * WIDTH_A, LORA_W + LORA_A + LORA_G
    o_b = c_rkv + c_lora
    o_c = o_b + 2 * WIDTH_B
    o_g = o_c + 3 * WIDTH_C
    z_b = -(-(c_rkv + LORA_PAD) // WIDTH_B) * WIDTH_B
    z_g = -(-(z_b + 2 * WIDTH_B) // (3 * d)) * (3 * d)
    z_c = z_g + 3 * d
    wb = w_in.astype(BF16)
    zeros = lambda n: jnp.zeros((d, n), BF16)
    wz = jnp.concatenate([wb[:, 0:o_b], zeros(z_b - o_b), wb[:, o_b:o_c], zeros(z_g - z_b - 2 * WIDTH_B),
                          wb[:, o_g:], wb[:, o_c:o_g]], axis=1)
    z2d = _norm_matmul(x2d, row(norm_mix_g), wz, 2048, 768)
    z3 = z2d.reshape(b, s, wz.shape[1])

    ones_a = _head_ones(WIDTH_A)
    pa = dict(
        mu_r=row(rwkv_mu[0:c_rkv]),
        mu_l=jnp.pad(row(rwkv_mu[c_rkv:]), ((0, 0), (0, LORA_PAD - c_lora))),
        w0=row(rwkv_w0), a0=row(rwkv_a0),
        w_up=_pad_rows(rwkv_w_up, 0, LORA_W + LORA_A).astype(BF16),
        a_up=_pad_rows(rwkv_a_up, LORA_W, LORA_W + LORA_A).astype(BF16),
        g_up=_pad_rows(rwkv_g_up, 0, LORA_PAD - LORA_W - LORA_A).astype(BF16),
        k_k=row(rwkv_k_k), k_a=row(rwkv_k_a), r_k=row(rwkv_r_k),
        ln_g=row(rwkv_ln_g), ln_b=row(rwkv_ln_b), ones=ones_a)
    ya = _rwkv_mixer(z3, pa)

    pb = dict(conv_w=conv_w.astype(F32), conv_b=row(conv_b), wa=_block_diag_halves(lru_wa), ba=row(lru_ba),
              wx=_block_diag_halves(lru_wx), bx=row(lru_bx), lam=row(lru_lambda))
    yb = _lru_mixer(z3, z_b // WIDTH_B, pb, ts=512)

    qg = jnp.tile(row(q_norm_g), (1, HEADS_PER_GROUP))
    kg = jnp.tile(row(k_norm_g), (1, HEADS_PER_GROUP))
    ones_c = _head_ones(GROUP_W)
    attn = []
    for gi, (_, dil) in enumerate(GROUPS):
        parts = _attn_group(z3, z_c // PAIR, bias_tiles[gi], qg, kg, ones_c, gi, dil)
        attn += [t.reshape(b * s, PAIR) for t in parts]

    pm = dict(proj_a=proj_a.astype(BF16), proj_b=proj_b.astype(BF16), proj_c=proj_c.astype(BF16),
              w_out=w_out.astype(BF16))
    x1 = _merge(x2d, ya.reshape(b * s, WIDTH_A), yb.reshape(b * s, WIDTH_B), attn, z2d, z_g // (3 * d), pm, 512)
    x2 = _mlp(x1, row(norm_mlp_g), mlp_up.astype(BF16), mlp_down.astype(BF16), 1024, 512)
    return x2.reshape(b, s, d)


def kernel(x, rel_bias, norm_mix_g, w_in, rwkv_mu, rwkv_w0, rwkv_w_up, rwkv_a0, rwkv_a_up, rwkv_g_up, rwkv_k_k, rwkv_k_a, rwkv_r_k, rwkv_ln_g, rwkv_ln_b, proj_a, conv_w, conv_b, lru_wa, lru_ba, lru_wx, lru_bx, lru_lambda, proj_b, q_norm_g, k_norm_g, proj_c, w_out, norm_mlp_g, mlp_up, mlp_down):
    prm = (norm_mix_g, w_in, rwkv_mu, rwkv_w0, rwkv_w_up, rwkv_a0, rwkv_a_up, rwkv_g_up, rwkv_k_k, rwkv_k_a,
           rwkv_r_k, rwkv_ln_g, rwkv_ln_b, proj_a, conv_w, conv_b, lru_wa, lru_ba, lru_wx, lru_bx, lru_lambda,
           proj_b, q_norm_g, k_norm_g, proj_c, w_out, norm_mlp_g, mlp_up, mlp_down)
    bias_tiles = _attn_bias_tiles(rel_bias)
    x = x.astype(F32)
    for l in range(norm_mix_g.shape[0]):
        x = _layer(x, l, bias_tiles, prm)
    return x
```

```python
import functools
import math

import jax
import jax.numpy as jnp
from jax import lax
from jax.experimental import pallas as pl
from jax.experimental.pallas import tpu as pltpu

F32 = jnp.float32
BF16 = jnp.bfloat16

N_HEADS_A = 8
HEAD = 64
PAIR = 2 * HEAD
WIDTH_A = N_HEADS_A * HEAD
N_PAIRS = WIDTH_A // PAIR
CHUNK = 64
LOCAL_WAVE = 2
PREV_ROWS = 16
LORA_W, LORA_A, LORA_G = 64, 64, 160
LORA_PAD = 384
GN_EPS = 64e-5
WIDTH_B = 512
LRU_BLOCK = 64
CONV_TAPS = 4
LRU_C = 8.0
GROUPS = ((128, 1), (512, 4), (2048, 16))
HEADS_PER_GROUP = 4
GROUP_W = HEADS_PER_GROUP * HEAD
WIDTH_C = len(GROUPS) * GROUP_W
QBLK = 128
ATTN_TILE = 2048
ATTN_WAVE = 4
N_BUCKETS = 32
MAX_DISTANCE = 2048
NEG_INF = -1e30
RMS_EPS = 1e-6
VMEM_LIMIT = 56 * 1024 * 1024


def _cparams(sem):
    return pltpu.CompilerParams(dimension_semantics=sem, vmem_limit_bytes=VMEM_LIMIT)


def _dot(a, b):
    return jnp.dot(a, b, preferred_element_type=F32)


def _dot_nt(a, b):
    return lax.dot_general(a, b, (((1,), (1,)), ((), ())), preferred_element_type=F32)


_NN = (((1,), (0,)), ((), ()))
_NT = (((1,), (1,)), ((), ()))
_TN = (((0,), (0,)), ((), ()))


def _mm(a, b, dims=_NN):
    return lax.dot_general(a.astype(BF16), b.astype(BF16), dims, preferred_element_type=F32)


def _sigmoid(x):
    return 1.0 / (1.0 + jnp.exp(-x))


def _softplus(x):
    return jnp.maximum(x, 0.0) + jnp.log1p(jnp.exp(-jnp.abs(x)))


def _head_ones(width):
    i = jnp.arange(width) // HEAD
    return (i[:, None] == i[None, :]).astype(BF16)


def _norm_matmul_kernel(x_ref, g_ref, w_ref, lo_ref, hi_ref, h_ref, *, n_lo):
    j = pl.program_id(1)

    @pl.when(j == 0)
    def _():
        x = x_ref[...]
        ms = jnp.mean(x * x, axis=-1, keepdims=True)
        h_ref[...] = (x * lax.rsqrt(ms + RMS_EPS) * g_ref[...]).astype(BF16)

    z = _dot(h_ref[...], w_ref[...])

    @pl.when(j < n_lo)
    def _():
        lo_ref[...] = z.astype(lo_ref.dtype)

    @pl.when(j >= n_lo)
    def _():
        hi_ref[...] = z.astype(hi_ref.dtype)


def _norm_matmul(x2d, g, w, n_lo_cols, tm, tn):
    m, d = x2d.shape
    n = w.shape[1]
    n_lo = n_lo_cols // tn
    return pl.pallas_call(
        functools.partial(_norm_matmul_kernel, n_lo=n_lo),
        grid=(m // tm, n // tn),
        in_specs=[pl.BlockSpec((tm, d), lambda i, j: (i, 0)),
                  pl.BlockSpec((1, d), lambda i, j: (0, 0)),
                  pl.BlockSpec((d, tn), lambda i, j: (0, j))],
        out_specs=[pl.BlockSpec((tm, tn), lambda i, j: (i, jnp.minimum(j, n_lo - 1))),
                   pl.BlockSpec((tm, tn), lambda i, j: (i, jnp.maximum(j - n_lo, 0)))],
        out_shape=[jax.ShapeDtypeStruct((m, n_lo_cols), BF16),
                   jax.ShapeDtypeStruct((m, n - n_lo_cols), F32)],
        scratch_shapes=[pltpu.VMEM((tm, d), BF16)],
        compiler_params=_cparams(("parallel", "arbitrary")),
        name="norm_matmul",
    )(x2d, g, w)


def _shift_rows(cur, prev, first):
    prev_row = jnp.where(first, 0.0, prev[PREV_ROWS - 1:PREV_ROWS, :])
    rolled = pltpu.roll(cur, 1, axis=0)
    row = lax.broadcasted_iota(jnp.int32, cur.shape, 0)
    return jnp.where(row == 0, prev_row, rolled)


def _rwkv_prep(zr_ref, zrp_ref, zl_ref, zlp_ref, mur_ref, mul_ref, w0_ref, wup_ref, a0_ref,
               aup_ref, gup_ref, kk_ref, ka_ref, rk_ref, ones_ref,
               r_out, k_out, v_out, lw_out, as_out, bs_out, g_out, bonus_out):
    first = pl.program_id(1) == 0
    zr = zr_ref[0].astype(F32)
    zl = zl_ref[0].astype(F32)
    fr = zr + (_shift_rows(zr, zrp_ref[0].astype(F32), first) - zr) * mur_ref[...]
    fl = zl + (_shift_rows(zl, zlp_ref[0].astype(F32), first) - zl) * mul_ref[...]
    r = fr[:, 0:WIDTH_A]
    k = fr[:, WIDTH_A:2 * WIDTH_A]
    v = fr[:, 2 * WIDTH_A:3 * WIDTH_A]
    x_wa = fl[:, 0:LORA_W + LORA_A]
    x_g = fl[:, LORA_W + LORA_A:LORA_PAD]
    ones = ones_ref[...]

    w = -_softplus(-(w0_ref[...] + _dot(jnp.tanh(x_wa).astype(BF16), wup_ref[...]))) - 0.5
    a = _sigmoid(a0_ref[...] + _dot(x_wa.astype(BF16), aup_ref[...]))
    g = _dot(_sigmoid(x_g).astype(BF16), gup_ref[...])
    kk = k * kk_ref[...]
    kk = kk / jnp.maximum(jnp.sqrt(_dot((kk * kk).astype(BF16), ones)), 1e-12)
    k2 = k * (1.0 + (a - 1.0) * ka_ref[...])
    bonus = _dot((r * k2 * rk_ref[...]).astype(BF16), ones) * v

    r_out[0] = r
    k_out[0] = k2
    v_out[0] = v
    lw_out[0] = -jnp.exp(w)
    as_out[0] = -kk
    bs_out[0] = kk * a
    g_out[0] = g
    bonus_out[0] = bonus


def _stack_heads(x):
    lo = lax.broadcasted_iota(jnp.int32, x.shape, 1) < HEAD
    return jnp.concatenate([jnp.where(lo, x, 0.0), jnp.where(lo, 0.0, x)], axis=0)


def _unstack_heads(x):
    return x[0:CHUNK, :] + x[CHUNK:2 * CHUNK, :]


def _split3(x):
    hi = x.astype(BF16)
    r1 = x - hi.astype(F32)
    mid = r1.astype(BF16)
    lo = (r1 - mid.astype(F32)).astype(BF16)
    return hi, mid, lo


def _rwkv_local_kernel(*refs, n_chunks):
    prep_in, (t_out, g_out, rh_out, yh_out, gate_out, bonus_out), scan = refs[:15], refs[15:21], refs[21:]
    r_ref, k_ref, v_ref, lw_ref, as_ref, bs_ref = scan
    _rwkv_prep(*prep_in, *scan, gate_out, bonus_out)
    two_c = 2 * CHUNK
    row = lax.broadcasted_iota(jnp.int32, (two_c, two_c), 0)
    col = lax.broadcasted_iota(jnp.int32, (two_c, two_c), 1)
    strict = col < row
    incl = col <= row
    eye = (col == row).astype(F32)
    tri = (lax.broadcasted_iota(jnp.int32, (CHUNK, CHUNK), 1)
           <= lax.broadcasted_iota(jnp.int32, (CHUNK, CHUNK), 0)).astype(BF16)

    for c0 in range(0, n_chunks, LOCAL_WAVE):
        units = [(c, p) for c in range(c0, min(c0 + LOCAL_WAVE, n_chunks)) for p in range(N_PAIRS)]
        at = lambda ref: [ref[0, c * CHUNK:(c + 1) * CHUNK, p * PAIR:(p + 1) * PAIR] for c, p in units]
        each = lambda f, *ls: [f(*xs) for xs in zip(*ls)]
        lw = at(lw_ref)
        lg = each(lambda x: sum(_dot(tri, part) for part in _split3(x)), lw)
        lg_end = each(lambda x: x[CHUNK - 1:CHUNK, :], lg)
        e_neg = each(lambda x: jnp.exp(-x), lg)
        e_end = each(lambda x, xe: jnp.exp(xe - x), lg, lg_end)
        a_s, b_s, kk = at(as_ref), at(bs_ref), at(k_ref)
        a_t = each(lambda x, g, w: _stack_heads(x * jnp.exp(g - w)), a_s, lg, lw)
        r_t = each(lambda x, g: _stack_heads(x * jnp.exp(g)), at(r_ref), lg)
        b_t = each(lambda x, e: _stack_heads(x * e), b_s, e_neg)
        k_t = each(lambda x, e: _stack_heads(x * e), kk, e_neg)
        b_p = each(lambda x, e: _stack_heads(x * e), b_s, e_end)
        k_p = each(lambda x, e: _stack_heads(x * e), kk, e_end)
        v_s = each(_stack_heads, at(v_ref))

        vcat = lambda x, y: jnp.concatenate([x, y], axis=0)
        hcat = lambda x, y: jnp.concatenate([x, y], axis=1)
        top = lambda x: x[0:two_c]
        bot = lambda x: x[two_c:2 * two_c]
        left = lambda x: x[:, 0:two_c]
        right = lambda x: x[:, two_c:2 * two_c]

        prod = each(lambda a, r, b, k: _mm(vcat(a, r), vcat(b, k), _NT), a_t, r_t, b_t, k_t)
        l_ab = each(lambda x: jnp.where(strict, left(top(x)), 0.0), prod)
        l_ak = each(lambda x: jnp.where(strict, right(top(x)), 0.0), prod)
        l_rb = each(lambda x: jnp.where(incl, left(bot(x)), 0.0), prod)
        l_rk = each(lambda x: jnp.where(incl, right(bot(x)), 0.0), prod)

        inv = each(lambda x: eye + x, l_ab)
        pw = each(lambda x: _mm(x, x), l_ab)
        for _ in range(int(math.log2(CHUNK)) - 2):
            both = each(lambda p, x: _mm(vcat(p, x), p), pw, inv)
            pw = each(top, both)
            inv = each(lambda x, y: x + bot(y), inv, both)
        inv = each(lambda x, p: x + _mm(x, p), inv, pw)

        lv_rkv = each(lambda l1, l2, v: _mm(vcat(l1, l2), v), l_ak, l_rk, v_s)
        aw_h = each(lambda m, a, x: _mm(m, hcat(a, top(x))), inv, a_t, lv_rkv)
        l_aw = each(_mm, l_rb, aw_h)
        r_h = each(lambda x, y: x + left(y), r_t, l_aw)
        y_h = each(lambda y, x: right(y) + bot(x), l_aw, lv_rkv)
        tg = each(lambda x, bp: _mm(x, bp, _TN), aw_h, b_p)
        t_m = each(lambda ge, x: eye * jnp.exp(ge) + top(x), lg_end, tg)
        g_m = each(lambda x, v, kp: bot(x) + _mm(v, kp, _TN), tg, v_s, k_p)
        for i, (c, p) in enumerate(units):
            rows = slice(c * CHUNK, (c + 1) * CHUNK)
            lanes = slice(p * PAIR, (p + 1) * PAIR)
            t_out[0, c, p] = t_m[i]
            g_out[0, c, p] = g_m[i]
            rh_out[0, rows, lanes] = _unstack_heads(r_h[i])
            yh_out[0, rows, lanes] = _unstack_heads(y_h[i])


def _rwkv_local(z3, p, ts):
    b, s, _ = z3.shape
    n_chunks = ts // CHUNK
    wr, wl = 3 * WIDTH_A, LORA_PAD
    lora_block = wr // wl
    prev = lambda c: (lambda bi, i: (bi, jnp.maximum(i * (ts // PREV_ROWS) - 1, 0), c))
    cur = lambda bi, i: (bi, i, 0)
    mat = lambda bi, i: (bi, i, 0, 0, 0)
    const = lambda bi, i: (0, 0)
    vec = pl.BlockSpec((1, WIDTH_A), const)
    seq_spec = pl.BlockSpec((1, ts, WIDTH_A), cur)
    mat_spec = pl.BlockSpec((1, n_chunks, N_PAIRS, PAIR, PAIR), mat)
    seq_shape = jax.ShapeDtypeStruct((b, s, WIDTH_A), F32)
    mat_shape = jax.ShapeDtypeStruct((b, s // CHUNK, N_PAIRS, PAIR, PAIR), F32)
    return pl.pallas_call(
        functools.partial(_rwkv_local_kernel, n_chunks=n_chunks),
        grid=(b, s // ts),
        in_specs=[pl.BlockSpec((1, ts, wr), cur), pl.BlockSpec((1, PREV_ROWS, wr), prev(0)),
                  pl.BlockSpec((1, ts, wl), lambda bi, i: (bi, i, lora_block)),
                  pl.BlockSpec((1, PREV_ROWS, wl), prev(lora_block)),
                  pl.BlockSpec((1, wr), const), pl.BlockSpec((1, wl), const),
                  vec, pl.BlockSpec((LORA_W + LORA_A, WIDTH_A), const),
                  vec, pl.BlockSpec((LORA_W + LORA_A, WIDTH_A), const),
                  pl.BlockSpec((LORA_PAD - LORA_W - LORA_A, WIDTH_A), const),
                  vec, vec, vec, pl.BlockSpec((WIDTH_A, WIDTH_A), const)],
        out_specs=[mat_spec, mat_spec, seq_spec, seq_spec, seq_spec, seq_spec],
        out_shape=[mat_shape, mat_shape, seq_shape, seq_shape, seq_shape, seq_shape],
        scratch_shapes=[pltpu.VMEM((1, ts, WIDTH_A), F32)] * 6,
        compiler_params=_cparams(("parallel", "arbitrary")),
        name="rwkv_local",
    )(z3, z3, z3, z3, p["mu_r"], p["mu_l"], p["w0"], p["w_up"], p["a0"], p["a_up"], p["g_up"],
      p["k_k"], p["k_a"], p["r_k"], p["ones"])


def _rwkv_state_kernel(t_ref, gm_ref, rh_ref, yh_ref, g_ref, bonus_ref, lng_ref, lnb_ref, ones_ref,
                       y_out, s_ref, *, n_chunks):
    @pl.when(pl.program_id(1) == 0)
    def _():
        s_ref[...] = jnp.zeros_like(s_ref)

    ones = ones_ref[...]
    state = [s_ref[p] for p in range(N_PAIRS)]
    entry = []
    for c in range(n_chunks):
        entry.append(list(state))
        state = [_mm(state[p], t_ref[0, c, p]) + gm_ref[0, c, p] for p in range(N_PAIRS)]
    for p in range(N_PAIRS):
        s_ref[p] = state[p]
    chunks = range(n_chunks)
    rows = [slice(c * CHUNK, (c + 1) * CHUNK) for c in chunks]
    y = [jnp.concatenate([_mm(rh_ref[0, rows[c], p * PAIR:(p + 1) * PAIR], entry[c][p], _NT)
                          for p in range(N_PAIRS)], axis=1) + yh_ref[0, rows[c], :] for c in chunks]
    mean = [_dot(v.astype(BF16), ones) * (1.0 / HEAD) for v in y]
    yc = [v - m for v, m in zip(y, mean)]
    var = [_dot((v * v).astype(BF16), ones) * (1.0 / HEAD) for v in yc]
    for c in chunks:
        yn = yc[c] * lax.rsqrt(var[c] + GN_EPS) * lng_ref[...] + lnb_ref[...]
        y_out[0, rows[c], :] = (yn + bonus_ref[0, rows[c], :]) * g_ref[0, rows[c], :]


def _rwkv_state(t, gm, rh, yh, g, bonus, p, ts):
    b, s, _ = rh.shape
    n_chunks = ts // CHUNK
    cur = lambda bi, i: (bi, i, 0)
    mat = lambda bi, i: (bi, i, 0, 0, 0)
    const = lambda bi, i: (0, 0)
    seq_spec = pl.BlockSpec((1, ts, WIDTH_A), cur)
    mat_spec = pl.BlockSpec((1, n_chunks, N_PAIRS, PAIR, PAIR), mat)
    vec = pl.BlockSpec((1, WIDTH_A), const)
    return pl.pallas_call(
        functools.partial(_rwkv_state_kernel, n_chunks=n_chunks),
        grid=(b, s // ts),
        in_specs=[mat_spec, mat_spec, seq_spec, seq_spec, seq_spec, seq_spec, vec, vec,
                  pl.BlockSpec((WIDTH_A, WIDTH_A), const)],
        out_specs=seq_spec,
        out_shape=jax.ShapeDtypeStruct((b, s, WIDTH_A), F32),
        scratch_shapes=[pltpu.VMEM((N_PAIRS, PAIR, PAIR), F32)],
        compiler_params=_cparams(("parallel", "arbitrary")),
        name="rwkv_state",
    )(t, gm, rh, yh, g, bonus, p["ln_g"], p["ln_b"], p["ones"])


def _rwkv_mixer(z3, p):
    t, gm, rh, yh, g, bonus = _rwkv_local(z3, p, ts=256)
    return _rwkv_state(t, gm, rh, yh, g, bonus, p, ts=256)


def _lru_kernel(x_ref, zp_ref, y_ref, cw_ref, cb_ref, wa_ref, ba_ref, wx_ref, bx_ref, lam_ref,
                o_ref, a_scr, b_scr, carry_ref, *, ts):
    i = pl.program_id(1)

    @pl.when(i == 0)
    def _():
        carry_ref[...] = jnp.zeros_like(carry_ref)

    x = x_ref[0].astype(F32)
    yb = y_ref[0].astype(F32)
    prev = jnp.where(i == 0, 0.0, zp_ref[0].astype(F32))
    ext = jnp.concatenate([prev, x], axis=0)
    xc = x * cw_ref[CONV_TAPS - 1:CONV_TAPS, :] + cb_ref[...]
    for back in range(1, CONV_TAPS):
        tap = CONV_TAPS - 1 - back
        xc = xc + pltpu.roll(ext, back, axis=0)[PREV_ROWS:, :] * cw_ref[tap:tap + 1, :]

    half = WIDTH_B // 2
    xcb = xc.astype(BF16)
    ga = jnp.concatenate([_dot(xcb[:, j * half:(j + 1) * half], wa_ref[j]) for j in range(2)], axis=1)
    gx = jnp.concatenate([_dot(xcb[:, j * half:(j + 1) * half], wx_ref[j]) for j in range(2)], axis=1)
    gate_a = _sigmoid(ga + ba_ref[...])
    gate_x = _sigmoid(gx + bx_ref[...])
    log_a = -LRU_C * gate_a * _softplus(-lam_ref[...])
    a = jnp.exp(log_a)
    mult = jnp.sqrt(jnp.maximum(-jnp.tanh(log_a) * (1.0 + a * a), 0.0))
    t_glob = i * ts + lax.broadcasted_iota(jnp.int32, (ts, WIDTH_B), 0)
    mult = jnp.where(t_glob == 0, 1.0, mult)
    a_scr[...] = a
    b_scr[...] = xc * gate_x * mult

    row8 = lax.broadcasted_iota(jnp.int32, (8, WIDTH_B), 0)

    def group_body(gi, h_prev):
        rows = pl.ds(pl.multiple_of(gi * 8, 8), 8)
        a8 = a_scr[rows, :]
        b8 = b_scr[rows, :]
        for sh in (1, 2, 4):
            ar = pltpu.roll(a8, sh, axis=0)
            br = pltpu.roll(b8, sh, axis=0)
            m = row8 >= sh
            b8 = jnp.where(m, a8 * br + b8, b8)
            a8 = jnp.where(m, a8 * ar, a8)
        h8 = a8 * h_prev + b8
        b_scr[rows, :] = h8
        return h8[7:8, :]

    carry_ref[0:1, :] = lax.fori_loop(0, ts // 8, group_body, carry_ref[0:1, :])
    h = b_scr[...]
    gelu = 0.5 * yb * (1.0 + jnp.tanh(math.sqrt(2.0 / math.pi) * (yb + 0.044715 * (yb * yb * yb))))
    o_ref[0] = h * gelu


def _lru_mixer(z3, x_block, p, ts):
    b, s, _ = z3.shape
    cur = lambda bi, i: (bi, i, 0)
    const2 = lambda bi, i: (0, 0)
    const3 = lambda bi, i: (0, 0, 0)
    vec = pl.BlockSpec((1, WIDTH_B), const2)
    half = WIDTH_B // 2
    return pl.pallas_call(
        functools.partial(_lru_kernel, ts=ts),
        grid=(b, s // ts),
        in_specs=[pl.BlockSpec((1, ts, WIDTH_B), lambda bi, i: (bi, i, x_block)),
                  pl.BlockSpec((1, PREV_ROWS, WIDTH_B),
                               lambda bi, i: (bi, jnp.maximum(i * (ts // PREV_ROWS) - 1, 0), x_block)),
                  pl.BlockSpec((1, ts, WIDTH_B), lambda bi, i: (bi, i, x_block + 1)),
                  pl.BlockSpec((CONV_TAPS, WIDTH_B), const2), vec,
                  pl.BlockSpec((2, half, half), const3), vec,
                  pl.BlockSpec((2, half, half), const3), vec, vec],
        out_specs=pl.BlockSpec((1, ts, WIDTH_B), cur),
        out_shape=jax.ShapeDtypeStruct((b, s, WIDTH_B), F32),
        scratch_shapes=[pltpu.VMEM((ts, WIDTH_B), F32), pltpu.VMEM((ts, WIDTH_B), F32),
                        pltpu.VMEM((8, WIDTH_B), F32)],
        compiler_params=_cparams(("parallel", "arbitrary")),
        name="lru_mixer",
    )(z3, z3, z3, p["conv_w"], p["conv_b"], p["wa"], p["ba"], p["wx"], p["bx"], p["lam"])


def _attn_kernel(q0, q1, k0, k1, kp0, kp1, v0, v1, vp0, vp1, bias_ref, qg_ref, kg_ref, ones_ref,
                 o0, o1, l0, l1, *, dil, n_sub):
    j = pl.program_id(1)
    ones = ones_ref[...]
    lane = lax.broadcasted_iota(jnp.int32, (QBLK, GROUP_W), 1)
    in_head = [(lane >= h * HEAD) & (lane < (h + 1) * HEAD) for h in range(HEADS_PER_GROUP)]
    prev_valid = (lax.broadcasted_iota(jnp.int32, (HEADS_PER_GROUP * QBLK, 2 * QBLK), 1) >= QBLK) | (j > 0)

    def rows(start):
        return pl.ds(start, QBLK, stride=dil) if dil > 1 else pl.ds(start, QBLK)

    def take(lo, hi, start):
        return jnp.concatenate([lo[0, rows(start), :], hi[0, rows(start), :]], axis=1)

    def head_sumsq(x):
        return _dot((x * x).astype(BF16), ones) * (1.0 / HEAD)

    def select_heads(x):
        out = jnp.zeros((QBLK, GROUP_W), F32)
        for h, m in enumerate(in_head):
            out = jnp.where(m, x[h * QBLK:(h + 1) * QBLK, :], out)
        return out

    def wave(units):
        each = lambda f, *ls: [f(*xs) for xs in zip(*ls)]
        span = dil * QBLK
        starts = [s for s, _ in units]
        before = [s - span if dil > 1 or isinstance(s, int) else pl.multiple_of(s - span, QBLK) for s in starts]
        q_raw = [take(q0, q1, s) for s in starts]
        k_raw = [jnp.concatenate([take(kp0, kp1, s) if far else take(k0, k1, p), take(k0, k1, s)], axis=0)
                 for (s, far), p in zip(units, before)]
        vv = [jnp.concatenate([take(vp0, vp1, s) if far else take(v0, v1, p), take(v0, v1, s)],
                              axis=0).astype(BF16) for (s, far), p in zip(units, before)]
        q_ms = each(head_sumsq, q_raw)
        k_ms = each(head_sumsq, k_raw)
        q = each(lambda x, ms: x * lax.rsqrt(ms + RMS_EPS) * qg_ref[...] * (HEAD ** -0.5), q_raw, q_ms)
        kk = each(lambda x, ms: (x * lax.rsqrt(ms + RMS_EPS) * kg_ref[...]).astype(BF16), k_raw, k_ms)
        qs = each(lambda x: jnp.concatenate([jnp.where(m, x, 0.0) for m in in_head], axis=0).astype(BF16), q)
        logits = each(lambda a, b: _dot_nt(a, b) + bias_ref[...], qs, kk)
        logits = [jnp.where(prev_valid, lg, NEG_INF) if far else lg for lg, (_, far) in zip(logits, units)]
        mx = each(lambda lg: jnp.max(lg, axis=-1, keepdims=True), logits)
        pr = each(lambda lg, m: jnp.exp(lg - m), logits, mx)
        den = each(lambda p: jnp.sum(p, axis=-1, keepdims=True), pr)
        pv = each(lambda p, v, dn: _dot(p.astype(BF16), v) / dn, pr, vv, den)
        out = each(select_heads, pv)
        lse = each(lambda m, dn: select_heads(jnp.broadcast_to(m + jnp.log(dn), (HEADS_PER_GROUP * QBLK, GROUP_W))),
                   mx, den)
        for s, o, l in zip(starts, out, lse):
            o0[0, rows(s), :] = o[:, 0:PAIR]
            o1[0, rows(s), :] = o[:, PAIR:GROUP_W]
            l0[0, rows(s), :] = l[:, 0:PAIR]
            l1[0, rows(s), :] = l[:, PAIR:GROUP_W]

    def loop(lo, hi, body):
        def step(i, carry):
            body(i)
            return carry
        lax.fori_loop(lo, hi, step, 0)

    span = dil * QBLK
    if n_sub == 1:
        loop(0, dil // ATTN_WAVE, lambda i: wave([(i * ATTN_WAVE + u, True) for u in range(ATTN_WAVE)]))
    elif dil > 1:
        wave([(r, True) for r in range(dil)])
        loop(1, n_sub, lambda n: wave([(r + n * span, False) for r in range(dil)]))
    else:
        wave([(u * QBLK, u == 0) for u in range(ATTN_WAVE)])
        loop(1, n_sub // ATTN_WAVE,
             lambda i: wave([(pl.multiple_of((i * ATTN_WAVE + u) * QBLK, QBLK), False) for u in range(ATTN_WAVE)]))


def _attn_group(z3, col0, bias, qg, kg, ones, gi, dil):
    b, s, _ = z3.shape
    span = dil * QBLK
    n_sub = ATTN_TILE // span
    halves = GROUP_W // PAIR
    per_part = len(GROUPS) * halves

    def cur(part, half):
        c = col0 + part * per_part + gi * halves + half
        return pl.BlockSpec((1, ATTN_TILE, PAIR), lambda bi, j: (bi, j, c))

    def prev(part, half):
        c = col0 + part * per_part + gi * halves + half
        return pl.BlockSpec((1, span, PAIR), lambda bi, j: (bi, jnp.maximum(j * n_sub - 1, 0), c))

    const2 = lambda bi, j: (0, 0)
    out_spec = pl.BlockSpec((1, ATTN_TILE, PAIR), lambda bi, j: (bi, j, 0))
    return pl.pallas_call(
        functools.partial(_attn_kernel, dil=dil, n_sub=n_sub),
        grid=(b, s // ATTN_TILE),
        in_specs=[cur(0, 0), cur(0, 1), cur(1, 0), cur(1, 1), prev(1, 0), prev(1, 1),
                  cur(2, 0), cur(2, 1), prev(2, 0), prev(2, 1),
                  pl.BlockSpec((HEADS_PER_GROUP * QBLK, 2 * QBLK), const2),
                  pl.BlockSpec((1, GROUP_W), const2), pl.BlockSpec((1, GROUP_W), const2),
                  pl.BlockSpec((GROUP_W, GROUP_W), const2)],
        out_specs=[out_spec] * 4,
        out_shape=[jax.ShapeDtypeStruct((b, s, PAIR), F32)] * 4,
        compiler_params=_cparams(("parallel", "arbitrary")),
        name=f"dilated_attn_g{gi}",
    )(*([z3] * 10), bias, qg, kg, ones)


def _t5_bucket(dist):
    max_exact = N_BUCKETS // 2
    d = jnp.maximum(dist, 0)
    large = max_exact + (jnp.log(jnp.maximum(d, 1).astype(F32) / max_exact)
                         / math.log(MAX_DISTANCE / max_exact) * (N_BUCKETS - max_exact)).astype(jnp.int32)
    large = jnp.minimum(large, N_BUCKETS - 1)
    return jnp.where(d < max_exact, d, large)


def _attn_bias_tiles(rel_bias):
    tiles = []
    kj = jnp.arange(2 * QBLK)[None, :]
    rel = (jnp.arange(QBLK)[:, None] + QBLK) - kj
    for gi, (window, dil) in enumerate(GROUPS):
        band = (rel >= 0) & (rel <= window // dil)
        tab = rel_bias.astype(F32)[:, gi * HEADS_PER_GROUP:(gi + 1) * HEADS_PER_GROUP]
        onehot = (_t5_bucket(rel * dil)[..., None] == jnp.arange(N_BUCKETS)).astype(F32)
        bias = jnp.einsum("qkn,nh->hqk", onehot, tab, precision=lax.Precision.HIGHEST)
        tiles.append(jnp.where(band[None], bias, NEG_INF).reshape(HEADS_PER_GROUP * QBLK, 2 * QBLK))
    return tiles


def _merge_kernel(x_ref, ya_ref, yb_ref, *rest):
    n_g = len(GROUPS)
    attn = rest[:4 * n_g]
    zg_ref, pa_ref, pb_ref, pc_ref, wo_ref, out_ref = rest[4 * n_g:]
    d = x_ref.shape[-1]
    outs = [jnp.concatenate([attn[4 * g][...], attn[4 * g + 1][...]], axis=1) for g in range(n_g)]
    lses = [jnp.concatenate([attn[4 * g + 2][...], attn[4 * g + 3][...]], axis=1) for g in range(n_g)]
    m = functools.reduce(jnp.maximum, lses)
    es = [jnp.exp(l - m) for l in lses]
    yc = sum(o * e for o, e in zip(outs, es)) / sum(es)
    gate = lambda n: _sigmoid(zg_ref[:, n * d:(n + 1) * d].astype(F32))
    merged = (gate(0) * _dot(ya_ref[...].astype(BF16), pa_ref[...])
              + gate(1) * _dot(yb_ref[...].astype(BF16), pb_ref[...])
              + gate(2) * _dot(yc.astype(BF16), pc_ref[...]))
    out_ref[...] = x_ref[...] + _dot(merged.astype(BF16), wo_ref[...])


def _merge(x2d, ya, yb, attn, z2d, gate_block, p, tm):
    m, d = x2d.shape
    row = lambda w: pl.BlockSpec((tm, w), lambda i: (i, 0))
    full = lambda a: pl.BlockSpec(a.shape, lambda i: (0, 0))
    return pl.pallas_call(
        _merge_kernel,
        grid=(m // tm,),
        in_specs=[row(d), row(WIDTH_A), row(WIDTH_B)] + [row(PAIR)] * len(attn)
                 + [pl.BlockSpec((tm, 3 * d), lambda i: (i, gate_block)),
                    full(p["proj_a"]), full(p["proj_b"]), full(p["proj_c"]), full(p["w_out"])],
        out_specs=row(d),
        out_shape=jax.ShapeDtypeStruct((m, d), F32),
        compiler_params=_cparams(("parallel",)),
        name="merge",
    )(x2d, ya, yb, *attn, z2d, p["proj_a"], p["proj_b"], p["proj_c"], p["w_out"])


def _mlp_kernel(x_ref, g_ref, wu_ref, wd_ref, o_ref, h_ref, acc_ref):
    j = pl.program_id(1)

    @pl.when(j == 0)
    def _():
        x = x_ref[...]
        ms = jnp.mean(x * x, axis=-1, keepdims=True)
        h_ref[...] = (x * lax.rsqrt(ms + RMS_EPS) * g_ref[...]).astype(BF16)
        acc_ref[...] = jnp.zeros_like(acc_ref)

    u = jnp.maximum(_dot(h_ref[...], wu_ref[...]), 0.0)
    acc_ref[...] += _dot((u * u).astype(BF16), wd_ref[...])

    @pl.when(j == pl.num_programs(1) - 1)
    def _():
        o_ref[...] = x_ref[...] + acc_ref[...]


def _mlp(x2d, g, wu, wd, tm, tf):
    m, d = x2d.shape
    f = wu.shape[1]
    return pl.pallas_call(
        _mlp_kernel,
        grid=(m // tm, f // tf),
        in_specs=[pl.BlockSpec((tm, d), lambda i, j: (i, 0)),
                  pl.BlockSpec((1, d), lambda i, j: (0, 0)),
                  pl.BlockSpec((d, tf), lambda i, j: (0, j)),
                  pl.BlockSpec((tf, d), lambda i, j: (j, 0))],
        out_specs=pl.BlockSpec((tm, d), lambda i, j: (i, 0)),
        out_shape=jax.ShapeDtypeStruct((m, d), F32),
        scratch_shapes=[pltpu.VMEM((tm, d), BF16), pltpu.VMEM((tm, d), F32)],
        compiler_params=_cparams(("parallel", "arbitrary")),
        name="mlp",
    )(x2d, g, wu, wd)


def _pad_rows(w, lo, total):
    return jnp.pad(w, ((lo, total - lo - w.shape[0]), (0, 0)))


def _block_diag_halves(w):
    n, bd, _ = w.shape
    per = n // 2
    out = jnp.zeros((2, per * bd, per * bd), w.dtype)
    for i in range(n):
        j, q = divmod(i, per)
        out = out.at[j, q * bd:(q + 1) * bd, q * bd:(q + 1) * bd].set(w[i])
    return out.astype(BF16)


def _layer(x, l, bias_tiles, prm):
    (norm_mix_g, w_in, rwkv_mu, rwkv_w0, rwkv_w_up, rwkv_a0, rwkv_a_up, rwkv_g_up, rwkv_k_k, rwkv_k_a,
     rwkv_r_k, rwkv_ln_g, rwkv_ln_b, proj_a, conv_w, conv_b, lru_wa, lru_ba, lru_wx, lru_bx, lru_lambda,
     proj_b, q_norm_g, k_norm_g, proj_c, w_out, norm_mlp_g, mlp_up, mlp_down) = [t[l] for t in prm]
    b, s, d = x.shape
    x2d = x.reshape(b * s, d)
    row = lambda t: t.reshape(1, -1).astype(F32)

    c_rkv, c_lora = 3 * WIDTH_A, LORA_W + LORA_A + LORA_G
    o_b = c_rkv + c_lora
    o_c = o_b + 2 * WIDTH_B
    o_g = o_c + 3 * WIDTH_C
    z_b = -(-(c_rkv + LORA_PAD) // WIDTH_B) * WIDTH_B
    z_g = -(-(z_b + 2 * WIDTH_B) // (3 * d)) * (3 * d)
    z_c = z_g + 3 * d
    wb = w_in.astype(BF16)
    zeros = lambda n: jnp.zeros((d, n), BF16)
    wz = jnp.concatenate([wb[:, 0:o_b], zeros(z_b - o_b), wb[:, o_b:o_c], zeros(z_g - z_b - 2 * WIDTH_B),
                          wb[:, o_g:], wb[:, o_c:o_g]], axis=1)
    z2d, zc2d = _norm_matmul(x2d, row(norm_mix_g), wz, z_c, 2048, 768)
    z3 = z2d.reshape(b, s, z_c)
    zc3 = zc2d.reshape(b, s, 3 * WIDTH_C)

    ones_a = _head_ones(WIDTH_A)
    pa = dict(
        mu_r=row(rwkv_mu[0:c_rkv]),
        mu_l=jnp.pad(row(rwkv_mu[c_rkv:]), ((0, 0), (0, LORA_PAD - c_lora))),
        w0=row(rwkv_w0), a0=row(rwkv_a0),
        w_up=_pad_rows(rwkv_w_up, 0, LORA_W + LORA_A).astype(BF16),
        a_up=_pad_rows(rwkv_a_up, LORA_W, LORA_W + LORA_A).astype(BF16),
        g_up=_pad_rows(rwkv_g_up, 0, LORA_PAD - LORA_W - LORA_A).astype(BF16),
        k_k=row(rwkv_k_k), k_a=row(rwkv_k_a), r_k=row(rwkv_r_k),
        ln_g=row(rwkv_ln_g), ln_b=row(rwkv_ln_b), ones=ones_a)
    ya = _rwkv_mixer(z3, pa)

    pb = dict(conv_w=conv_w.astype(F32), conv_b=row(conv_b), wa=_block_diag_halves(lru_wa), ba=row(lru_ba),
              wx=_block_diag_halves(lru_wx), bx=row(lru_bx), lam=row(lru_lambda))
    yb = _lru_mixer(z3, z_b // WIDTH_B, pb, ts=512)

    qg = jnp.tile(row(q_norm_g), (1, HEADS_PER_GROUP))
    kg = jnp.tile(row(k_norm_g), (1, HEADS_PER_GROUP))
    ones_c = _head_ones(GROUP_W)
    attn = []
    for gi, (_, dil) in enumerate(GROUPS):
        parts = _attn_group(zc3, 0, bias_tiles[gi], qg, kg, ones_c, gi, dil)
        attn += [t.reshape(b * s, PAIR) for t in parts]

    pm = dict(proj_a=proj_a.astype(BF16), proj_b=proj_b.astype(BF16), proj_c=proj_c.astype(BF16),
              w_out=w_out.astype(BF16))
    x1 = _merge(x2d, ya.reshape(b * s, WIDTH_A), yb.reshape(b * s, WIDTH_B), attn, z2d, z_g // (3 * d), pm, 512)
    x2 = _mlp(x1, row(norm_mlp_g), mlp_up.astype(BF16), mlp_down.astype(BF16), 1024, 1024)
    return x2.reshape(b, s, d)


def kernel(x, rel_bias, norm_mix_g, w_in, rwkv_mu, rwkv_w0, rwkv_w_up, rwkv_a0, rwkv_a_up, rwkv_g_up, rwkv_k_k, rwkv_k_a, rwkv_r_k, rwkv_ln_g, rwkv_ln_b, proj_a, conv_w, conv_b, lru_wa, lru_ba, lru_wx, lru_bx, lru_lambda, proj_b, q_norm_g, k_norm_g, proj_c, w_out, norm_mlp_g, mlp_up, mlp_down):
    prm = (norm_mix_g, w_in, rwkv_mu, rwkv_w0, rwkv_w_up, rwkv_a0, rwkv_a_up, rwkv_g_up, rwkv_k_k, rwkv_k_a,
           rwkv_r_k, rwkv_ln_g, rwkv_ln_b, proj_a, conv_w, conv_b, lru_wa, lru_ba, lru_wx, lru_bx, lru_lambda,
           proj_b, q_norm_g, k_norm_g, proj_c, w_out, norm_mlp_g, mlp_up, mlp_down)
    bias_tiles = _attn_bias_tiles(rel_bias)
    x = x.astype(F32)
    for l in range(norm_mix_g.shape[0]):
        x = _layer(x, l, bias_tiles, prm)
    return x
```

```python
import functools
import math

import jax
import jax.numpy as jnp
from jax import lax
from jax.experimental import pallas as pl
from jax.experimental.pallas import tpu as pltpu

F32 = jnp.float32
BF16 = jnp.bfloat16

N_HEADS_A = 8
HEAD = 64
PAIR = 2 * HEAD
WIDTH_A = N_HEADS_A * HEAD
N_PAIRS = WIDTH_A // PAIR
CHUNK = 64
LOCAL_WAVE = 2
PREV_ROWS = 16
LORA_W, LORA_A, LORA_G = 64, 64, 160
LORA_PAD = 384
GN_EPS = 64e-5
WIDTH_B = 512
LRU_BLOCK = 64
CONV_TAPS = 4
LRU_C = 8.0
GROUPS = ((128, 1), (512, 4), (2048, 16))
HEADS_PER_GROUP = 4
GROUP_W = HEADS_PER_GROUP * HEAD
WIDTH_C = len(GROUPS) * GROUP_W
QBLK = 128
ATTN_TILE = 2048
ATTN_WAVE = 4
N_BUCKETS = 32
MAX_DISTANCE = 2048
NEG_INF = -1e30
RMS_EPS = 1e-6
VMEM_LIMIT = 56 * 1024 * 1024


def _cparams(sem):
    return pltpu.CompilerParams(dimension_semantics=sem, vmem_limit_bytes=VMEM_LIMIT)


def _dot(a, b):
    return jnp.dot(a, b, preferred_element_type=F32)


def _dot_nt(a, b):
    return lax.dot_general(a, b, (((1,), (1,)), ((), ())), preferred_element_type=F32)


_NN = (((1,), (0,)), ((), ()))
_NT = (((1,), (1,)), ((), ()))
_TN = (((0,), (0,)), ((), ()))


def _mm(a, b, dims=_NN):
    return lax.dot_general(a.astype(BF16), b.astype(BF16), dims, preferred_element_type=F32)


def _sigmoid(x):
    return 1.0 / (1.0 + jnp.exp(-x))


def _softplus(x):
    return jnp.maximum(x, 0.0) + jnp.log1p(jnp.exp(-jnp.abs(x)))


def _head_ones(width):
    i = jnp.arange(width) // HEAD
    return (i[:, None] == i[None, :]).astype(BF16)


def _norm_matmul_kernel(x_ref, g_ref, w_ref, lo_ref, hi_ref, h_ref, *, n_lo):
    j = pl.program_id(1)

    @pl.when(j == 0)
    def _():
        x = x_ref[...]
        ms = jnp.mean(x * x, axis=-1, keepdims=True)
        h_ref[...] = (x * lax.rsqrt(ms + RMS_EPS) * g_ref[...]).astype(BF16)

    @pl.when(j < n_lo)
    def _():
        lo_ref[...] = _dot(h_ref[...], w_ref[...].astype(BF16)).astype(lo_ref.dtype)

    @pl.when(j >= n_lo)
    def _():
        hi_ref[...] = _dot(h_ref[...], w_ref[...].astype(BF16)).astype(hi_ref.dtype)


def _norm_matmul(x2d, g, w_all, layer, n_lo_cols, tm, tn):
    m, d = x2d.shape
    n = w_all.shape[2]
    n_lo = n_lo_cols // tn
    return pl.pallas_call(
        functools.partial(_norm_matmul_kernel, n_lo=n_lo),
        grid=(m // tm, n // tn),
        in_specs=[pl.BlockSpec((tm, d), lambda i, j: (i, 0)),
                  pl.BlockSpec((1, d), lambda i, j: (0, 0)),
                  pl.BlockSpec((None, d, tn), lambda i, j: (layer, 0, j))],
        out_specs=[pl.BlockSpec((tm, tn), lambda i, j: (i, jnp.minimum(j, n_lo - 1))),
                   pl.BlockSpec((tm, tn), lambda i, j: (i, jnp.maximum(j - n_lo, 0)))],
        out_shape=[jax.ShapeDtypeStruct((m, n_lo_cols), BF16),
                   jax.ShapeDtypeStruct((m, n - n_lo_cols), F32)],
        scratch_shapes=[pltpu.VMEM((tm, d), BF16)],
        compiler_params=_cparams(("parallel", "arbitrary")),
        name="norm_matmul",
    )(x2d, g, w_all)


def _projection_weights(w_in, d):
    c_rkv, c_lora = 3 * WIDTH_A, LORA_W + LORA_A + LORA_G
    o_b = c_rkv + c_lora
    o_c = o_b + 2 * WIDTH_B
    o_g = o_c + 3 * WIDTH_C
    z_b = -(-(c_rkv + LORA_PAD) // WIDTH_B) * WIDTH_B
    z_g = -(-(z_b + 2 * WIDTH_B) // (3 * d)) * (3 * d)
    z_c = z_g + 3 * d
    zeros = lambda n: jnp.zeros(w_in.shape[:2] + (n,), w_in.dtype)
    wz = jnp.concatenate([w_in[..., 0:o_b], zeros(z_b - o_b), w_in[..., o_b:o_c],
                          zeros(z_g - z_b - 2 * WIDTH_B), w_in[..., o_g:], w_in[..., o_c:o_g]], axis=-1)
    return wz, z_b, z_g, z_c


def _shift_rows(cur, prev, first):
    prev_row = jnp.where(first, 0.0, prev[PREV_ROWS - 1:PREV_ROWS, :])
    rolled = pltpu.roll(cur, 1, axis=0)
    row = lax.broadcasted_iota(jnp.int32, cur.shape, 0)
    return jnp.where(row == 0, prev_row, rolled)


def _interleave(*stages):
    live = list(stages)
    while live:
        for gen in list(live):
            try:
                next(gen)
            except StopIteration:
                live.remove(gen)


def _rwkv_prep(rows, zr_ref, zrp_ref, zl_ref, zlp_ref, mur_ref, mul_ref, w0_ref, wup_ref, a0_ref,
               aup_ref, gup_ref, kk_ref, ka_ref, rk_ref, ones_ref,
               r_out, k_out, v_out, lg_out, lgp_out, as_out, bs_out, g_out, bonus_out):
    lo = rows.start
    if lo == 0:
        first = pl.program_id(1) == 0
        prev_r, prev_l = zrp_ref[0].astype(F32), zlp_ref[0].astype(F32)
    else:
        first = False
        prev_r = zr_ref[0, lo - PREV_ROWS:lo, :].astype(F32)
        prev_l = zl_ref[0, lo - PREV_ROWS:lo, :].astype(F32)
    zr = zr_ref[0, rows, :].astype(F32)
    zl = zl_ref[0, rows, :].astype(F32)
    fr = zr + (_shift_rows(zr, prev_r, first) - zr) * mur_ref[...]
    fl = zl + (_shift_rows(zl, prev_l, first) - zl) * mul_ref[...]
    r = fr[:, 0:WIDTH_A]
    k = fr[:, WIDTH_A:2 * WIDTH_A]
    v = fr[:, 2 * WIDTH_A:3 * WIDTH_A]
    x_wa = fl[:, 0:LORA_W + LORA_A]
    x_g = fl[:, LORA_W + LORA_A:LORA_PAD]
    ones = ones_ref[...]
    r_out[0, rows, :] = r
    v_out[0, rows, :] = v
    yield

    w = -_softplus(-(w0_ref[...] + _dot(jnp.tanh(x_wa).astype(BF16), wup_ref[...]))) - 0.5
    lw = -jnp.exp(w)
    pos = lax.broadcasted_iota(jnp.int32, lw.shape, 0) & (CHUNK - 1)
    lg = lw
    for sh in [1 << i for i in range(int(math.log2(CHUNK)))]:
        lg = lg + jnp.where(pos >= sh, pltpu.roll(lg, sh, axis=0), 0.0)
    lg_out[0, rows, :] = lg
    lgp_out[0, rows, :] = lg - lw
    yield

    a = _sigmoid(a0_ref[...] + _dot(x_wa.astype(BF16), aup_ref[...]))
    k2 = k * (1.0 + (a - 1.0) * ka_ref[...])
    k_out[0, rows, :] = k2
    yield

    kk = k * kk_ref[...]
    kk = kk / jnp.maximum(jnp.sqrt(_dot((kk * kk).astype(BF16), ones)), 1e-12)
    as_out[0, rows, :] = -kk
    bs_out[0, rows, :] = kk * a
    yield

    g_out[0, rows, :] = _dot(_sigmoid(x_g).astype(BF16), gup_ref[...])
    yield

    bonus_out[0, rows, :] = _dot((r * k2 * rk_ref[...]).astype(BF16), ones) * v


def _stack_heads(x):
    lo = lax.broadcasted_iota(jnp.int32, x.shape, 1) < HEAD
    return jnp.concatenate([jnp.where(lo, x, 0.0), jnp.where(lo, 0.0, x)], axis=0)


def _unstack_heads(x):
    return x[0:CHUNK, :] + x[CHUNK:2 * CHUNK, :]


def _rwkv_local_kernel(*refs, n_chunks):
    prep_in, (t_out, g_out, rh_out, yh_out, gate_out, bonus_out), scan = refs[:15], refs[15:21], refs[21:]
    r_ref, k_ref, v_ref, lg_ref, lgp_ref, as_ref, bs_ref = scan
    two_c = 2 * CHUNK
    row = lax.broadcasted_iota(jnp.int32, (two_c, two_c), 0)
    col = lax.broadcasted_iota(jnp.int32, (two_c, two_c), 1)
    strict = col < row
    incl = col <= row
    eye = (col == row).astype(F32)

    def prep(w):
        rows = slice(w * LOCAL_WAVE * CHUNK, min((w + 1) * LOCAL_WAVE, n_chunks) * CHUNK)
        return _rwkv_prep(rows, *prep_in, *scan, gate_out, bonus_out)

    def wave(w):
        c0 = w * LOCAL_WAVE
        units = [(c, p) for c in range(c0, min(c0 + LOCAL_WAVE, n_chunks)) for p in range(N_PAIRS)]
        at = lambda ref: [ref[0, c * CHUNK:(c + 1) * CHUNK, p * PAIR:(p + 1) * PAIR] for c, p in units]
        each = lambda f, *ls: [f(*xs) for xs in zip(*ls)]
        lg = at(lg_ref)
        lg_end = each(lambda x: x[CHUNK - 1:CHUNK, :], lg)
        e_neg = each(lambda x: jnp.exp(-x), lg)
        e_end = each(lambda x, xe: jnp.exp(xe - x), lg, lg_end)
        a_s, b_s, kk = at(as_ref), at(bs_ref), at(k_ref)
        a_t = each(lambda x, gp: _stack_heads(x * jnp.exp(gp)), a_s, at(lgp_ref))
        r_t = each(lambda x, g: _stack_heads(x * jnp.exp(g)), at(r_ref), lg)
        b_t = each(lambda x, e: _stack_heads(x * e), b_s, e_neg)
        k_t = each(lambda x, e: _stack_heads(x * e), kk, e_neg)
        b_p = each(lambda x, e: _stack_heads(x * e), b_s, e_end)
        k_p = each(lambda x, e: _stack_heads(x * e), kk, e_end)
        v_s = each(_stack_heads, at(v_ref))
        yield

        vcat = lambda x, y: jnp.concatenate([x, y], axis=0)
        hcat = lambda x, y: jnp.concatenate([x, y], axis=1)
        top = lambda x: x[0:two_c]
        bot = lambda x: x[two_c:2 * two_c]
        left = lambda x: x[:, 0:two_c]
        right = lambda x: x[:, two_c:2 * two_c]

        prod = each(lambda a, r, b, k: _mm(vcat(a, r), vcat(b, k), _NT), a_t, r_t, b_t, k_t)
        l_ab = each(lambda x: jnp.where(strict, left(top(x)), 0.0), prod)
        l_ak = each(lambda x: jnp.where(strict, right(top(x)), 0.0), prod)
        l_rb = each(lambda x: jnp.where(incl, left(bot(x)), 0.0), prod)
        l_rk = each(lambda x: jnp.where(incl, right(bot(x)), 0.0), prod)
        yield

        inv = each(lambda x: eye + x, l_ab)
        pw = each(lambda x: _mm(x, x), l_ab)
        yield
        for _ in range(int(math.log2(CHUNK)) - 2):
            both = each(lambda p, x: _mm(vcat(p, x), p), pw, inv)
            pw = each(top, both)
            inv = each(lambda x, y: x + bot(y), inv, both)
            yield
        inv = each(lambda x, p: x + _mm(x, p), inv, pw)
        yield

        lv_rkv = each(lambda l1, l2, v: _mm(vcat(l1, l2), v), l_ak, l_rk, v_s)
        yield
        aw_h = each(lambda m, a, x: _mm(m, hcat(a, top(x))), inv, a_t, lv_rkv)
        yield
        l_aw = each(_mm, l_rb, aw_h)
        r_h = each(lambda x, y: x + left(y), r_t, l_aw)
        y_h = each(lambda y, x: right(y) + bot(x), l_aw, lv_rkv)
        yield
        tg = each(lambda x, bp: _mm(x, bp, _TN), aw_h, b_p)
        t_m = each(lambda ge, x: eye * jnp.exp(ge) + top(x), lg_end, tg)
        yield
        g_m = each(lambda x, v, kp: bot(x) + _mm(v, kp, _TN), tg, v_s, k_p)
        for i, (c, p) in enumerate(units):
            rows = slice(c * CHUNK, (c + 1) * CHUNK)
            lanes = slice(p * PAIR, (p + 1) * PAIR)
            t_out[0, c, p] = t_m[i].astype(t_out.dtype)
            g_out[0, c, p] = g_m[i]
            rh_out[0, rows, lanes] = _unstack_heads(r_h[i]).astype(rh_out.dtype)
            yh_out[0, rows, lanes] = _unstack_heads(y_h[i])

    n_waves = -(-n_chunks // LOCAL_WAVE)
    _interleave(prep(0))
    for w in range(n_waves):
        _interleave(wave(w), *([prep(w + 1)] if w + 1 < n_waves else []))


def _rwkv_local(z3, p, ts):
    b, s, _ = z3.shape
    n_chunks = ts // CHUNK
    wr, wl = 3 * WIDTH_A, LORA_PAD
    lora_block = wr // wl
    prev = lambda c: (lambda bi, i: (bi, jnp.maximum(i * (ts // PREV_ROWS) - 1, 0), c))
    cur = lambda bi, i: (bi, i, 0)
    mat = lambda bi, i: (bi, i, 0, 0, 0)
    const = lambda bi, i: (0, 0)
    vec = pl.BlockSpec((1, WIDTH_A), const)
    seq_spec = pl.BlockSpec((1, ts, WIDTH_A), cur)
    mat_spec = pl.BlockSpec((1, n_chunks, N_PAIRS, PAIR, PAIR), mat)
    seq_shape = lambda dt: jax.ShapeDtypeStruct((b, s, WIDTH_A), dt)
    mat_shape = lambda dt: jax.ShapeDtypeStruct((b, s // CHUNK, N_PAIRS, PAIR, PAIR), dt)
    return pl.pallas_call(
        functools.partial(_rwkv_local_kernel, n_chunks=n_chunks),
        grid=(b, s // ts),
        in_specs=[pl.BlockSpec((1, ts, wr), cur), pl.BlockSpec((1, PREV_ROWS, wr), prev(0)),
                  pl.BlockSpec((1, ts, wl), lambda bi, i: (bi, i, lora_block)),
                  pl.BlockSpec((1, PREV_ROWS, wl), prev(lora_block)),
                  pl.BlockSpec((1, wr), const), pl.BlockSpec((1, wl), const),
                  vec, pl.BlockSpec((LORA_W + LORA_A, WIDTH_A), const),
                  vec, pl.BlockSpec((LORA_W + LORA_A, WIDTH_A), const),
                  pl.BlockSpec((LORA_PAD - LORA_W - LORA_A, WIDTH_A), const),
                  vec, vec, vec, pl.BlockSpec((WIDTH_A, WIDTH_A), const)],
        out_specs=[mat_spec, mat_spec, seq_spec, seq_spec, seq_spec, seq_spec],
        out_shape=[mat_shape(BF16), mat_shape(F32), seq_shape(BF16), seq_shape(F32), seq_shape(F32), seq_shape(F32)],
        scratch_shapes=[pltpu.VMEM((1, ts, WIDTH_A), F32)] * 7,
        compiler_params=_cparams(("parallel", "arbitrary")),
        name="rwkv_local",
    )(z3, z3, z3, z3, p["mu_r"], p["mu_l"], p["w0"], p["w_up"], p["a0"], p["a_up"], p["g_up"],
      p["k_k"], p["k_a"], p["r_k"], p["ones"])


def _rwkv_state_kernel(t_ref, gm_ref, rh_ref, yh_ref, g_ref, bonus_ref, lng_ref, lnb_ref, ones_ref,
                       y_out, s_ref, *, n_chunks):
    @pl.when(pl.program_id(1) == 0)
    def _():
        s_ref[...] = jnp.zeros_like(s_ref)

    ones = ones_ref[...]
    state = [s_ref[p] for p in range(N_PAIRS)]
    entry = []
    for c in range(n_chunks):
        entry.append(list(state))
        state = [_mm(state[p], t_ref[0, c, p]) + gm_ref[0, c, p] for p in range(N_PAIRS)]
    for p in range(N_PAIRS):
        s_ref[p] = state[p]
    chunks = range(n_chunks)
    rows = [slice(c * CHUNK, (c + 1) * CHUNK) for c in chunks]
    y = [jnp.concatenate([_mm(rh_ref[0, rows[c], p * PAIR:(p + 1) * PAIR], entry[c][p], _NT)
                          for p in range(N_PAIRS)], axis=1) + yh_ref[0, rows[c], :] for c in chunks]
    mean = [_dot(v.astype(BF16), ones) * (1.0 / HEAD) for v in y]
    yc = [v - m for v, m in zip(y, mean)]
    var = [_dot((v * v).astype(BF16), ones) * (1.0 / HEAD) for v in yc]
    for c in chunks:
        yn = yc[c] * lax.rsqrt(var[c] + GN_EPS) * lng_ref[...] + lnb_ref[...]
        y_out[0, rows[c], :] = ((yn + bonus_ref[0, rows[c], :]) * g_ref[0, rows[c], :]).astype(y_out.dtype)


def _rwkv_state(t, gm, rh, yh, g, bonus, p, ts):
    b, s, _ = rh.shape
    n_chunks = ts // CHUNK
    cur = lambda bi, i: (bi, i, 0)
    mat = lambda bi, i: (bi, i, 0, 0, 0)
    const = lambda bi, i: (0, 0)
    seq_spec = pl.BlockSpec((1, ts, WIDTH_A), cur)
    mat_spec = pl.BlockSpec((1, n_chunks, N_PAIRS, PAIR, PAIR), mat)
    vec = pl.BlockSpec((1, WIDTH_A), const)
    return pl.pallas_call(
        functools.partial(_rwkv_state_kernel, n_chunks=n_chunks),
        grid=(b, s // ts),
        in_specs=[mat_spec, mat_spec, seq_spec, seq_spec, seq_spec, seq_spec, vec, vec,
                  pl.BlockSpec((WIDTH_A, WIDTH_A), const)],
        out_specs=seq_spec,
        out_shape=jax.ShapeDtypeStruct((b, s, WIDTH_A), BF16),
        scratch_shapes=[pltpu.VMEM((N_PAIRS, PAIR, PAIR), F32)],
        compiler_params=_cparams(("parallel", "arbitrary")),
        name="rwkv_state",
    )(t, gm, rh, yh, g, bonus, p["ln_g"], p["ln_b"], p["ones"])


def _rwkv_mixer(z3, p):
    t, gm, rh, yh, g, bonus = _rwkv_local(z3, p, ts=256)
    return _rwkv_state(t, gm, rh, yh, g, bonus, p, ts=256)


def _lru_kernel(x_ref, zp_ref, y_ref, cw_ref, cb_ref, wa_ref, ba_ref, wx_ref, bx_ref, lam_ref,
                o_ref, a_scr, b_scr, carry_ref, *, ts):
    i = pl.program_id(1)

    @pl.when(i == 0)
    def _():
        carry_ref[...] = jnp.zeros_like(carry_ref)

    x = x_ref[0].astype(F32)
    yb = y_ref[0].astype(F32)
    prev = jnp.where(i == 0, 0.0, zp_ref[0].astype(F32))
    ext = jnp.concatenate([prev, x], axis=0)
    xc = x * cw_ref[CONV_TAPS - 1:CONV_TAPS, :] + cb_ref[...]
    for back in range(1, CONV_TAPS):
        tap = CONV_TAPS - 1 - back
        xc = xc + pltpu.roll(ext, back, axis=0)[PREV_ROWS:, :] * cw_ref[tap:tap + 1, :]

    half = WIDTH_B // 2
    xcb = xc.astype(BF16)
    ga = jnp.concatenate([_dot(xcb[:, j * half:(j + 1) * half], wa_ref[j]) for j in range(2)], axis=1)
    gx = jnp.concatenate([_dot(xcb[:, j * half:(j + 1) * half], wx_ref[j]) for j in range(2)], axis=1)
    gate_a = _sigmoid(ga + ba_ref[...])
    gate_x = _sigmoid(gx + bx_ref[...])
    log_a = -LRU_C * gate_a * _softplus(-lam_ref[...])
    a = jnp.exp(log_a)
    mult = jnp.sqrt(jnp.maximum(-jnp.tanh(log_a) * (1.0 + a * a), 0.0))
    t_glob = i * ts + lax.broadcasted_iota(jnp.int32, (ts, WIDTH_B), 0)
    mult = jnp.where(t_glob == 0, 1.0, mult)
    a_scr[...] = a
    b_scr[...] = xc * gate_x * mult

    row8 = lax.broadcasted_iota(jnp.int32, (8, WIDTH_B), 0)

    def group_body(gi, h_prev):
        rows = pl.ds(pl.multiple_of(gi * 8, 8), 8)
        a8 = a_scr[rows, :]
        b8 = b_scr[rows, :]
        for sh in (1, 2, 4):
            ar = pltpu.roll(a8, sh, axis=0)
            br = pltpu.roll(b8, sh, axis=0)
            m = row8 >= sh
            b8 = jnp.where(m, a8 * br + b8, b8)
            a8 = jnp.where(m, a8 * ar, a8)
        h8 = a8 * h_prev + b8
        b_scr[rows, :] = h8
        return h8[7:8, :]

    carry_ref[0:1, :] = lax.fori_loop(0, ts // 8, group_body, carry_ref[0:1, :])
    h = b_scr[...]
    gelu = 0.5 * yb * (1.0 + jnp.tanh(math.sqrt(2.0 / math.pi) * (yb + 0.044715 * (yb * yb * yb))))
    o_ref[0] = (h * gelu).astype(o_ref.dtype)


def _lru_mixer(z3, x_block, p, ts):
    b, s, _ = z3.shape
    cur = lambda bi, i: (bi, i, 0)
    const2 = lambda bi, i: (0, 0)
    const3 = lambda bi, i: (0, 0, 0)
    vec = pl.BlockSpec((1, WIDTH_B), const2)
    half = WIDTH_B // 2
    return pl.pallas_call(
        functools.partial(_lru_kernel, ts=ts),
        grid=(b, s // ts),
        in_specs=[pl.BlockSpec((1, ts, WIDTH_B), lambda bi, i: (bi, i, x_block)),
                  pl.BlockSpec((1, PREV_ROWS, WIDTH_B),
                               lambda bi, i: (bi, jnp.maximum(i * (ts // PREV_ROWS) - 1, 0), x_block)),
                  pl.BlockSpec((1, ts, WIDTH_B), lambda bi, i: (bi, i, x_block + 1)),
                  pl.BlockSpec((CONV_TAPS, WIDTH_B), const2), vec,
                  pl.BlockSpec((2, half, half), const3), vec,
                  pl.BlockSpec((2, half, half), const3), vec, vec],
        out_specs=pl.BlockSpec((1, ts, WIDTH_B), cur),
        out_shape=jax.ShapeDtypeStruct((b, s, WIDTH_B), BF16),
        scratch_shapes=[pltpu.VMEM((ts, WIDTH_B), F32), pltpu.VMEM((ts, WIDTH_B), F32),
                        pltpu.VMEM((8, WIDTH_B), F32)],
        compiler_params=_cparams(("parallel", "arbitrary")),
        name="lru_mixer",
    )(z3, z3, z3, p["conv_w"], p["conv_b"], p["wa"], p["ba"], p["wx"], p["bx"], p["lam"])


def _attn_kernel(q0, q1, k0, k1, kp0, kp1, v0, v1, vp0, vp1, bias_ref, qg_ref, kg_ref, ones_ref,
                 o0, o1, l0, l1, *, dil, n_sub):
    j = pl.program_id(1)
    ones = ones_ref[...]
    lane = lax.broadcasted_iota(jnp.int32, (QBLK, GROUP_W), 1)
    in_head = [(lane >= h * HEAD) & (lane < (h + 1) * HEAD) for h in range(HEADS_PER_GROUP)]
    prev_valid = (lax.broadcasted_iota(jnp.int32, (HEADS_PER_GROUP * QBLK, 2 * QBLK), 1) >= QBLK) | (j > 0)

    def rows(start):
        return pl.ds(start, QBLK, stride=dil) if dil > 1 else pl.ds(start, QBLK)

    def take(lo, hi, start):
        return jnp.concatenate([lo[0, rows(start), :], hi[0, rows(start), :]], axis=1)

    def head_sumsq(x):
        return _dot((x * x).astype(BF16), ones) * (1.0 / HEAD)

    def select_heads(x):
        out = jnp.zeros((QBLK, GROUP_W), F32)
        for h, m in enumerate(in_head):
            out = jnp.where(m, x[h * QBLK:(h + 1) * QBLK, :], out)
        return out

    def wave(units):
        each = lambda f, *ls: [f(*xs) for xs in zip(*ls)]
        span = dil * QBLK
        starts = [s for s, _ in units]
        before = [s - span if dil > 1 or isinstance(s, int) else pl.multiple_of(s - span, QBLK) for s in starts]
        q_raw = [take(q0, q1, s) for s in starts]
        k_raw = [jnp.concatenate([take(kp0, kp1, s) if far else take(k0, k1, p), take(k0, k1, s)], axis=0)
                 for (s, far), p in zip(units, before)]
        vv = [jnp.concatenate([take(vp0, vp1, s) if far else take(v0, v1, p), take(v0, v1, s)],
                              axis=0).astype(BF16) for (s, far), p in zip(units, before)]
        q_ms = each(head_sumsq, q_raw)
        k_ms = each(head_sumsq, k_raw)
        q = each(lambda x, ms: x * lax.rsqrt(ms + RMS_EPS) * qg_ref[...] * (HEAD ** -0.5), q_raw, q_ms)
        kk = each(lambda x, ms: (x * lax.rsqrt(ms + RMS_EPS) * kg_ref[...]).astype(BF16), k_raw, k_ms)
        qs = each(lambda x: jnp.concatenate([jnp.where(m, x, 0.0) for m in in_head], axis=0).astype(BF16), q)
        logits = each(lambda a, b: _dot_nt(a, b) + bias_ref[...], qs, kk)
        logits = [jnp.where(prev_valid, lg, NEG_INF) if far else lg for lg, (_, far) in zip(logits, units)]
        mx = each(lambda lg: jnp.max(lg, axis=-1, keepdims=True), logits)
        pr = each(lambda lg, m: jnp.exp(lg - m), logits, mx)
        den = each(lambda p: jnp.sum(p, axis=-1, keepdims=True), pr)
        pv = each(lambda p, v, dn: _dot(p.astype(BF16), v) / dn, pr, vv, den)
        out = each(select_heads, pv)
        lse = each(lambda m, dn: select_heads(jnp.broadcast_to(m + jnp.log(dn), (HEADS_PER_GROUP * QBLK, GROUP_W))),
                   mx, den)
        for s, o, l in zip(starts, out, lse):
            o0[0, rows(s), :] = o[:, 0:PAIR]
            o1[0, rows(s), :] = o[:, PAIR:GROUP_W]
            l0[0, rows(s), :] = l[:, 0:PAIR]
            l1[0, rows(s), :] = l[:, PAIR:GROUP_W]

    def loop(lo, hi, body):
        def step(i, carry):
            body(i)
            return carry
        lax.fori_loop(lo, hi, step, 0)

    span = dil * QBLK
    if n_sub == 1:
        loop(0, dil // ATTN_WAVE, lambda i: wave([(i * ATTN_WAVE + u, True) for u in range(ATTN_WAVE)]))
    elif dil > 1:
        wave([(r, True) for r in range(dil)])
        loop(1, n_sub, lambda n: wave([(r + n * span, False) for r in range(dil)]))
    else:
        wave([(u * QBLK, u == 0) for u in range(ATTN_WAVE)])
        loop(1, n_sub // ATTN_WAVE,
             lambda i: wave([(pl.multiple_of((i * ATTN_WAVE + u) * QBLK, QBLK), False) for u in range(ATTN_WAVE)]))


def _attn_group(z3, col0, bias, qg, kg, ones, gi, dil):
    b, s, _ = z3.shape
    span = dil * QBLK
    n_sub = ATTN_TILE // span
    halves = GROUP_W // PAIR
    per_part = len(GROUPS) * halves

    def cur(part, half):
        c = col0 + part * per_part + gi * halves + half
        return pl.BlockSpec((1, ATTN_TILE, PAIR), lambda bi, j: (bi, j, c))

    def prev(part, half):
        c = col0 + part * per_part + gi * halves + half
        return pl.BlockSpec((1, span, PAIR), lambda bi, j: (bi, jnp.maximum(j * n_sub - 1, 0), c))

    const2 = lambda bi, j: (0, 0)
    out_spec = pl.BlockSpec((1, ATTN_TILE, PAIR), lambda bi, j: (bi, j, 0))
    return pl.pallas_call(
        functools.partial(_attn_kernel, dil=dil, n_sub=n_sub),
        grid=(b, s // ATTN_TILE),
        in_specs=[cur(0, 0), cur(0, 1), cur(1, 0), cur(1, 1), prev(1, 0), prev(1, 1),
                  cur(2, 0), cur(2, 1), prev(2, 0), prev(2, 1),
                  pl.BlockSpec((HEADS_PER_GROUP * QBLK, 2 * QBLK), const2),
                  pl.BlockSpec((1, GROUP_W), const2), pl.BlockSpec((1, GROUP_W), const2),
                  pl.BlockSpec((GROUP_W, GROUP_W), const2)],
        out_specs=[out_spec] * 4,
        out_shape=[jax.ShapeDtypeStruct((b, s, PAIR), F32)] * 4,
        compiler_params=_cparams(("parallel", "arbitrary")),
        name=f"dilated_attn_g{gi}",
    )(*([z3] * 10), bias, qg, kg, ones)


def _t5_bucket(dist):
    max_exact = N_BUCKETS // 2
    d = jnp.maximum(dist, 0)
    large = max_exact + (jnp.log(jnp.maximum(d, 1).astype(F32) / max_exact)
                         / math.log(MAX_DISTANCE / max_exact) * (N_BUCKETS - max_exact)).astype(jnp.int32)
    large = jnp.minimum(large, N_BUCKETS - 1)
    return jnp.where(d < max_exact, d, large)


def _attn_bias_tiles(rel_bias):
    tiles = []
    kj = jnp.arange(2 * QBLK)[None, :]
    rel = (jnp.arange(QBLK)[:, None] + QBLK) - kj
    for gi, (window, dil) in enumerate(GROUPS):
        band = (rel >= 0) & (rel <= window // dil)
        tab = rel_bias.astype(F32)[:, gi * HEADS_PER_GROUP:(gi + 1) * HEADS_PER_GROUP]
        onehot = (_t5_bucket(rel * dil)[..., None] == jnp.arange(N_BUCKETS)).astype(F32)
        bias = jnp.einsum("qkn,nh->hqk", onehot, tab, precision=lax.Precision.HIGHEST)
        tiles.append(jnp.where(band[None], bias, NEG_INF).reshape(HEADS_PER_GROUP * QBLK, 2 * QBLK))
    return tiles


def _merge_kernel(x_ref, ya_ref, yb_ref, *rest):
    n_g = len(GROUPS)
    attn = rest[:4 * n_g]
    zg_ref, pa_ref, pb_ref, pc_ref, wo_ref, out_ref = rest[4 * n_g:]
    d = x_ref.shape[-1]
    outs = [jnp.concatenate([attn[4 * g][...], attn[4 * g + 1][...]], axis=1) for g in range(n_g)]
    lses = [jnp.concatenate([attn[4 * g + 2][...], attn[4 * g + 3][...]], axis=1) for g in range(n_g)]
    m = functools.reduce(jnp.maximum, lses)
    es = [jnp.exp(l - m) for l in lses]
    yc = sum(o * e for o, e in zip(outs, es)) / sum(es)
    gate = lambda n: _sigmoid(zg_ref[:, n * d:(n + 1) * d].astype(F32))
    merged = (gate(0) * _dot(ya_ref[...].astype(BF16), pa_ref[...])
              + gate(1) * _dot(yb_ref[...].astype(BF16), pb_ref[...])
              + gate(2) * _dot(yc.astype(BF16), pc_ref[...]))
    out_ref[...] = x_ref[...] + _dot(merged.astype(BF16), wo_ref[...])


def _merge(x2d, ya, yb, attn, z2d, gate_block, p, tm):
    m, d = x2d.shape
    row = lambda w: pl.BlockSpec((tm, w), lambda i: (i, 0))
    full = lambda a: pl.BlockSpec(a.shape, lambda i: (0, 0))
    return pl.pallas_call(
        _merge_kernel,
        grid=(m // tm,),
        in_specs=[row(d), row(WIDTH_A), row(WIDTH_B)] + [row(PAIR)] * len(attn)
                 + [pl.BlockSpec((tm, 3 * d), lambda i: (i, gate_block)),
                    full(p["proj_a"]), full(p["proj_b"]), full(p["proj_c"]), full(p["w_out"])],
        out_specs=row(d),
        out_shape=jax.ShapeDtypeStruct((m, d), F32),
        compiler_params=_cparams(("parallel",)),
        name="merge",
    )(x2d, ya, yb, *attn, z2d, p["proj_a"], p["proj_b"], p["proj_c"], p["w_out"])


def _mlp_kernel(x_ref, g_ref, wu_ref, wd_ref, o_ref, h_ref, acc_ref):
    j = pl.program_id(1)

    @pl.when(j == 0)
    def _():
        x = x_ref[...]
        ms = jnp.mean(x * x, axis=-1, keepdims=True)
        h_ref[...] = (x * lax.rsqrt(ms + RMS_EPS) * g_ref[...]).astype(BF16)
        acc_ref[...] = jnp.zeros_like(acc_ref)

    u = jnp.maximum(_dot(h_ref[...], wu_ref[...]), 0.0)
    acc_ref[...] += _dot((u * u).astype(BF16), wd_ref[...])

    @pl.when(j == pl.num_programs(1) - 1)
    def _():
        o_ref[...] = x_ref[...] + acc_ref[...]


def _mlp(x2d, g, wu, wd, tm, tf):
    m, d = x2d.shape
    f = wu.shape[1]
    return pl.pallas_call(
        _mlp_kernel,
        grid=(m // tm, f // tf),
        in_specs=[pl.BlockSpec((tm, d), lambda i, j: (i, 0)),
                  pl.BlockSpec((1, d), lambda i, j: (0, 0)),
                  pl.BlockSpec((d, tf), lambda i, j: (0, j)),
                  pl.BlockSpec((tf, d), lambda i, j: (j, 0))],
        out_specs=pl.BlockSpec((tm, d), lambda i, j: (i, 0)),
        out_shape=jax.ShapeDtypeStruct((m, d), F32),
        scratch_shapes=[pltpu.VMEM((tm, d), BF16), pltpu.VMEM((tm, d), F32)],
        compiler_params=_cparams(("parallel", "arbitrary")),
        name="mlp",
    )(x2d, g, wu, wd)


def _pad_rows(w, lo, total):
    return jnp.pad(w, ((lo, total - lo - w.shape[0]), (0, 0)))


def _block_diag_halves(w):
    n, bd, _ = w.shape
    per = n // 2
    out = jnp.zeros((2, per * bd, per * bd), w.dtype)
    for i in range(n):
        j, q = divmod(i, per)
        out = out.at[j, q * bd:(q + 1) * bd, q * bd:(q + 1) * bd].set(w[i])
    return out.astype(BF16)


def _layer(x, l, bias_tiles, proj, prm):
    (norm_mix_g, _, rwkv_mu, rwkv_w0, rwkv_w_up, rwkv_a0, rwkv_a_up, rwkv_g_up, rwkv_k_k, rwkv_k_a,
     rwkv_r_k, rwkv_ln_g, rwkv_ln_b, proj_a, conv_w, conv_b, lru_wa, lru_ba, lru_wx, lru_bx, lru_lambda,
     proj_b, q_norm_g, k_norm_g, proj_c, w_out, norm_mlp_g, mlp_up, mlp_down) = [t[l] for t in prm]
    b, s, d = x.shape
    x2d = x.reshape(b * s, d)
    row = lambda t: t.reshape(1, -1).astype(F32)

    wz_all, z_b, z_g, z_c = proj
    c_rkv, c_lora = 3 * WIDTH_A, LORA_W + LORA_A + LORA_G
    z2d, zc2d = _norm_matmul(x2d, row(norm_mix_g), wz_all, l, z_c, 2048, 768)
    z3 = z2d.reshape(b, s, z_c)
    zc3 = zc2d.reshape(b, s, 3 * WIDTH_C)

    ones_a = _head_ones(WIDTH_A)
    pa = dict(
        mu_r=row(rwkv_mu[0:c_rkv]),
        mu_l=jnp.pad(row(rwkv_mu[c_rkv:]), ((0, 0), (0, LORA_PAD - c_lora))),
        w0=row(rwkv_w0), a0=row(rwkv_a0),
        w_up=_pad_rows(rwkv_w_up, 0, LORA_W + LORA_A).astype(BF16),
        a_up=_pad_rows(rwkv_a_up, LORA_W, LORA_W + LORA_A).astype(BF16),
        g_up=_pad_rows(rwkv_g_up, 0, LORA_PAD - LORA_W - LORA_A).astype(BF16),
        k_k=row(rwkv_k_k), k_a=row(rwkv_k_a), r_k=row(rwkv_r_k),
        ln_g=row(rwkv_ln_g), ln_b=row(rwkv_ln_b), ones=ones_a)
    ya = _rwkv_mixer(z3, pa)

    pb = dict(conv_w=conv_w.astype(F32), conv_b=row(conv_b), wa=_block_diag_halves(lru_wa), ba=row(lru_ba),
              wx=_block_diag_halves(lru_wx), bx=row(lru_bx), lam=row(lru_lambda))
    yb = _lru_mixer(z3, z_b // WIDTH_B, pb, ts=512)

    qg = jnp.tile(row(q_norm_g), (1, HEADS_PER_GROUP))
    kg = jnp.tile(row(k_norm_g), (1, HEADS_PER_GROUP))
    ones_c = _head_ones(GROUP_W)
    attn = []
    for gi, (_, dil) in enumerate(GROUPS):
        parts = _attn_group(zc3, 0, bias_tiles[gi], qg, kg, ones_c, gi, dil)
        attn += [t.reshape(b * s, PAIR) for t in parts]

    pm = dict(proj_a=proj_a.astype(BF16), proj_b=proj_b.astype(BF16), proj_c=proj_c.astype(BF16),
              w_out=w_out.astype(BF16))
    x1 = _merge(x2d, ya.reshape(b * s, WIDTH_A), yb.reshape(b * s, WIDTH_B), attn, z2d, z_g // (3 * d), pm, 512)
    x2 = _mlp(x1, row(norm_mlp_g), mlp_up.astype(BF16), mlp_down.astype(BF16), 1024, 1024)
    return x2.reshape(b, s, d)


def kernel(x, rel_bias, norm_mix_g, w_in, rwkv_mu, rwkv_w0, rwkv_w_up, rwkv_a0, rwkv_a_up, rwkv_g_up, rwkv_k_k, rwkv_k_a, rwkv_r_k, rwkv_ln_g, rwkv_ln_b, proj_a, conv_w, conv_b, lru_wa, lru_ba, lru_wx, lru_bx, lru_lambda, proj_b, q_norm_g, k_norm_g, proj_c, w_out, norm_mlp_g, mlp_up, mlp_down):
    prm = (norm_mix_g, w_in, rwkv_mu, rwkv_w0, rwkv_w_up, rwkv_a0, rwkv_a_up, rwkv_g_up, rwkv_k_k, rwkv_k_a,
           rwkv_r_k, rwkv_ln_g, rwkv_ln_b, proj_a, conv_w, conv_b, lru_wa, lru_ba, lru_wx, lru_bx, lru_lambda,
           proj_b, q_norm_g, k_norm_g, proj_c, w_out, norm_mlp_g, mlp_up, mlp_down)
    bias_tiles = _attn_bias_tiles(rel_bias)
    proj = _projection_weights(w_in.astype(F32), x.shape[-1])
    x = x.astype(F32)
    for l in range(norm_mix_g.shape[0]):
        x = _layer(x, l, bias_tiles, proj, prm)
    return x
```

```python
import functools
import math

import jax
import jax.numpy as jnp
from jax import lax
from jax.experimental import pallas as pl
from jax.experimental.pallas import tpu as pltpu

F32 = jnp.float32
BF16 = jnp.bfloat16

N_HEADS_A = 8
HEAD = 64
PAIR = 2 * HEAD
WIDTH_A = N_HEADS_A * HEAD
N_PAIRS = WIDTH_A // PAIR
CHUNK = 64
LOCAL_WAVE = 4
PREV_ROWS = 16
LORA_W, LORA_A, LORA_G = 64, 64, 160
LORA_PAD = 384
GN_EPS = 64e-5
WIDTH_B = 512
LRU_BLOCK = 64
CONV_TAPS = 4
LRU_C = 8.0
GROUPS = ((128, 1), (512, 4), (2048, 16))
HEADS_PER_GROUP = 4
GROUP_W = HEADS_PER_GROUP * HEAD
WIDTH_C = len(GROUPS) * GROUP_W
QBLK = 128
ATTN_TILE = 2048
ATTN_WAVE = 4
N_BUCKETS = 32
MAX_DISTANCE = 2048
NEG_INF = -1e30
RMS_EPS = 1e-6
VMEM_LIMIT = 56 * 1024 * 1024


def _cparams(sem):
    return pltpu.CompilerParams(dimension_semantics=sem, vmem_limit_bytes=VMEM_LIMIT)


def _dot(a, b):
    return jnp.dot(a, b, preferred_element_type=F32)


def _dot_nt(a, b):
    return lax.dot_general(a, b, (((1,), (1,)), ((), ())), preferred_element_type=F32)


_NN = (((1,), (0,)), ((), ()))
_NT = (((1,), (1,)), ((), ()))
_TN = (((0,), (0,)), ((), ()))


def _mm(a, b, dims=_NN):
    return lax.dot_general(a.astype(BF16), b.astype(BF16), dims, preferred_element_type=F32)


def _sigmoid(x):
    return 1.0 / (1.0 + jnp.exp(-x))


def _softplus(x):
    return jnp.maximum(x, 0.0) + jnp.log1p(jnp.exp(-jnp.abs(x)))


def _head_ones(width):
    i = jnp.arange(width) // HEAD
    return (i[:, None] == i[None, :]).astype(BF16)


def _norm_matmul_kernel(x_ref, g_ref, w_ref, lo_ref, hi_ref, h_ref, *, n_lo):
    j = pl.program_id(1)

    @pl.when(j == 0)
    def _():
        x = x_ref[...]
        ms = jnp.mean(x * x, axis=-1, keepdims=True)
        h_ref[...] = (x * lax.rsqrt(ms + RMS_EPS) * g_ref[...]).astype(BF16)

    @pl.when(j < n_lo)
    def _():
        lo_ref[...] = _dot(h_ref[...], w_ref[...].astype(BF16)).astype(lo_ref.dtype)

    @pl.when(j >= n_lo)
    def _():
        hi_ref[...] = _dot(h_ref[...], w_ref[...].astype(BF16)).astype(hi_ref.dtype)


def _norm_matmul(x2d, g, w_all, layer, n_lo_cols, tm, tn):
    m, d = x2d.shape
    n = w_all.shape[2]
    n_lo = n_lo_cols // tn
    return pl.pallas_call(
        functools.partial(_norm_matmul_kernel, n_lo=n_lo),
        grid=(m // tm, n // tn),
        in_specs=[pl.BlockSpec((tm, d), lambda i, j: (i, 0)),
                  pl.BlockSpec((1, d), lambda i, j: (0, 0)),
                  pl.BlockSpec((None, d, tn), lambda i, j: (layer, 0, j))],
        out_specs=[pl.BlockSpec((tm, tn), lambda i, j: (i, jnp.minimum(j, n_lo - 1))),
                   pl.BlockSpec((tm, tn), lambda i, j: (i, jnp.maximum(j - n_lo, 0)))],
        out_shape=[jax.ShapeDtypeStruct((m, n_lo_cols), BF16),
                   jax.ShapeDtypeStruct((m, n - n_lo_cols), F32)],
        scratch_shapes=[pltpu.VMEM((tm, d), BF16)],
        compiler_params=_cparams(("parallel", "arbitrary")),
        name="norm_matmul",
    )(x2d, g, w_all)


def _relayout_kernel(w_ref, o_ref, *, moves, width):
    rows = w_ref.shape[1]
    end = 0
    for src, dst, n in moves:
        if dst > end:
            o_ref[0, :, end:dst] = jnp.zeros((rows, dst - end), o_ref.dtype)
        o_ref[0, :, dst:dst + n] = w_ref[0, :, src:src + n].astype(o_ref.dtype)
        end = dst + n
    if end < width:
        o_ref[0, :, end:width] = jnp.zeros((rows, width - end), o_ref.dtype)


def _projection_weights(w_in, d, tr=256):
    n_layers, rows, n_in = w_in.shape
    c_rkv, c_lora = 3 * WIDTH_A, LORA_W + LORA_A + LORA_G
    o_b = c_rkv + c_lora
    o_c = o_b + 2 * WIDTH_B
    o_g = o_c + 3 * WIDTH_C
    z_b = -(-(c_rkv + LORA_PAD) // WIDTH_B) * WIDTH_B
    z_g = -(-(z_b + 2 * WIDTH_B) // (3 * d)) * (3 * d)
    z_c = z_g + 3 * d
    width = z_c + 3 * WIDTH_C
    moves = ((0, 0, o_b), (o_b, z_b, o_c - o_b), (o_g, z_g, n_in - o_g), (o_c, z_c, o_g - o_c))
    wz = pl.pallas_call(
        functools.partial(_relayout_kernel, moves=moves, width=width),
        grid=(n_layers, rows // tr),
        in_specs=[pl.BlockSpec((1, tr, n_in), lambda l, i: (l, i, 0))],
        out_specs=pl.BlockSpec((1, tr, width), lambda l, i: (l, i, 0)),
        out_shape=jax.ShapeDtypeStruct((n_layers, rows, width), BF16),
        compiler_params=_cparams(("parallel", "parallel")),
        name="projection_weights",
    )(w_in)
    return wz, z_b, z_g, z_c


def _shift_rows(cur, prev):
    prev_row = prev[PREV_ROWS - 1:PREV_ROWS, :]
    rolled = pltpu.roll(cur, 1, axis=0)
    row = lax.broadcasted_iota(jnp.int32, cur.shape, 0)
    return jnp.where(row == 0, prev_row, rolled)


def _interleave(*stages):
    live = list(stages)
    while live:
        for gen in list(live):
            try:
                next(gen)
            except StopIteration:
                live.remove(gen)


def _rwkv_prep(zr, zl, prev_r, prev_l, rows, params, scan, put_gate, put_bonus):
    mur_ref, mul_ref, w0_ref, wup_ref, a0_ref, aup_ref, gup_ref, kk_ref, ka_ref, rk_ref, ones_ref = params
    r_out, k_out, v_out, lg_out, lgp_out, as_out, bs_out = scan
    fr = zr + (_shift_rows(zr, prev_r) - zr) * mur_ref[...]
    fl = zl + (_shift_rows(zl, prev_l) - zl) * mul_ref[...]
    r = fr[:, 0:WIDTH_A]
    k = fr[:, WIDTH_A:2 * WIDTH_A]
    v = fr[:, 2 * WIDTH_A:3 * WIDTH_A]
    x_wa = fl[:, 0:LORA_W + LORA_A]
    x_g = fl[:, LORA_W + LORA_A:LORA_PAD]
    ones = ones_ref[...]
    r_out[0, rows, :] = r
    v_out[0, rows, :] = v
    yield

    w = -_softplus(-(w0_ref[...] + _dot(jnp.tanh(x_wa).astype(BF16), wup_ref[...]))) - 0.5
    lw = -jnp.exp(w)
    pos = lax.broadcasted_iota(jnp.int32, lw.shape, 0) & (CHUNK - 1)
    lg = lw
    for sh in [1 << i for i in range(int(math.log2(CHUNK)))]:
        lg = lg + jnp.where(pos >= sh, pltpu.roll(lg, sh, axis=0), 0.0)
    lg_out[0, rows, :] = lg
    lgp_out[0, rows, :] = lg - lw
    yield

    a = _sigmoid(a0_ref[...] + _dot(x_wa.astype(BF16), aup_ref[...]))
    k2 = k * (1.0 + (a - 1.0) * ka_ref[...])
    k_out[0, rows, :] = k2
    yield

    kk = k * kk_ref[...]
    kk = kk / jnp.maximum(jnp.sqrt(_dot((kk * kk).astype(BF16), ones)), 1e-12)
    as_out[0, rows, :] = -kk
    bs_out[0, rows, :] = kk * a
    yield

    put_gate(_dot(_sigmoid(x_g).astype(BF16), gup_ref[...]))
    yield

    put_bonus(_dot((r * k2 * rk_ref[...]).astype(BF16), ones) * v)


def _stack_heads(x):
    lo = lax.broadcasted_iota(jnp.int32, x.shape, 1) < HEAD
    return jnp.concatenate([jnp.where(lo, x, 0.0), jnp.where(lo, 0.0, x)], axis=0)


def _unstack_heads(x):
    return x[0:CHUNK, :] + x[CHUNK:2 * CHUNK, :]


def _rwkv_local_kernel(*refs, n_chunks):
    (zr_ref, zl_ref, zr_next, zl_next), params = refs[:4], refs[4:15]
    t_out, g_out, rh_out, yh_out, gate_out, bonus_out = refs[15:21]
    scan, (gate_carry, bonus_carry) = refs[21:28], refs[28:]
    r_ref, k_ref, v_ref, lg_ref, lgp_ref, as_ref, bs_ref = scan
    two_c = 2 * CHUNK
    row = lax.broadcasted_iota(jnp.int32, (two_c, two_c), 0)
    col = lax.broadcasted_iota(jnp.int32, (two_c, two_c), 1)
    strict = col < row
    incl = col <= row
    eye = (col == row).astype(F32)
    wave_rows = LOCAL_WAVE * CHUNK
    n_waves = n_chunks // LOCAL_WAVE
    first_rows = slice(0, wave_rows)
    f32 = lambda ref, rows: ref[0, rows, :].astype(F32)

    def put(ref, rows):
        def store(val):
            ref[rows] = val
        return store

    def prep(w):
        rows = slice(w * wave_rows, (w + 1) * wave_rows)
        before = slice(w * wave_rows - PREV_ROWS, w * wave_rows)
        return _rwkv_prep(f32(zr_ref, rows), f32(zl_ref, rows), f32(zr_ref, before), f32(zl_ref, before), rows,
                          params, scan, put(gate_out, (0, rows)), put(bonus_out, (0, rows)))

    def prep_first(zr_src, zl_src, prev_r, prev_l):
        return _rwkv_prep(f32(zr_src, first_rows), f32(zl_src, first_rows), prev_r, prev_l, first_rows,
                          params, scan, put(gate_carry, slice(None)), put(bonus_carry, slice(None)))

    @pl.when(pl.program_id(1) == 0)
    def _():
        zero = lambda ref: jnp.zeros((PREV_ROWS, ref.shape[-1]), F32)
        _interleave(prep_first(zr_ref, zl_ref, zero(zr_ref), zero(zl_ref)))

    gate_out[0, first_rows, :] = gate_carry[...]
    bonus_out[0, first_rows, :] = bonus_carry[...]

    def wave(w):
        c0 = w * LOCAL_WAVE
        units = [(c, p) for c in range(c0, min(c0 + LOCAL_WAVE, n_chunks)) for p in range(N_PAIRS)]
        at = lambda ref: [ref[0, c * CHUNK:(c + 1) * CHUNK, p * PAIR:(p + 1) * PAIR] for c, p in units]
        each = lambda f, *ls: [f(*xs) for xs in zip(*ls)]
        lg = at(lg_ref)
        lg_end = each(lambda x: x[CHUNK - 1:CHUNK, :], lg)
        e_neg = each(lambda x: jnp.exp(-x), lg)
        e_end = each(lambda x, xe: jnp.exp(xe - x), lg, lg_end)
        a_s, b_s, kk = at(as_ref), at(bs_ref), at(k_ref)
        a_t = each(lambda x, gp: _stack_heads(x * jnp.exp(gp)), a_s, at(lgp_ref))
        r_t = each(lambda x, g: _stack_heads(x * jnp.exp(g)), at(r_ref), lg)
        b_t = each(lambda x, e: _stack_heads(x * e), b_s, e_neg)
        k_t = each(lambda x, e: _stack_heads(x * e), kk, e_neg)
        b_p = each(lambda x, e: _stack_heads(x * e), b_s, e_end)
        k_p = each(lambda x, e: _stack_heads(x * e), kk, e_end)
        v_s = each(_stack_heads, at(v_ref))
        yield

        vcat = lambda x, y: jnp.concatenate([x, y], axis=0)
        hcat = lambda x, y: jnp.concatenate([x, y], axis=1)
        top = lambda x: x[0:two_c]
        bot = lambda x: x[two_c:2 * two_c]
        left = lambda x: x[:, 0:two_c]
        right = lambda x: x[:, two_c:2 * two_c]

        prod = each(lambda a, r, b, k: _mm(vcat(a, r), vcat(b, k), _NT), a_t, r_t, b_t, k_t)
        l_ab = each(lambda x: jnp.where(strict, left(top(x)), 0.0), prod)
        l_ak = each(lambda x: jnp.where(strict, right(top(x)), 0.0), prod)
        l_rb = each(lambda x: jnp.where(incl, left(bot(x)), 0.0), prod)
        l_rk = each(lambda x: jnp.where(incl, right(bot(x)), 0.0), prod)
        yield

        inv = each(lambda x: eye + x, l_ab)
        pw = each(lambda x: _mm(x, x), l_ab)
        yield
        for _ in range(int(math.log2(CHUNK)) - 2):
            both = each(lambda p, x: _mm(vcat(p, x), p), pw, inv)
            pw = each(top, both)
            inv = each(lambda x, y: x + bot(y), inv, both)
            yield
        inv = each(lambda x, p: x + _mm(x, p), inv, pw)
        yield

        lv_rkv = each(lambda l1, l2, v: _mm(vcat(l1, l2), v), l_ak, l_rk, v_s)
        yield
        aw_h = each(lambda m, a, x: _mm(m, hcat(a, top(x))), inv, a_t, lv_rkv)
        yield
        l_aw = each(_mm, l_rb, aw_h)
        r_h = each(lambda x, y: x + left(y), r_t, l_aw)
        y_h = each(lambda y, x: right(y) + bot(x), l_aw, lv_rkv)
        yield
        tg = each(lambda x, bp: _mm(x, bp, _TN), aw_h, b_p)
        t_m = each(lambda ge, x: eye * jnp.exp(ge) + top(x), lg_end, tg)
        yield
        g_m = each(lambda x, v, kp: bot(x) + _mm(v, kp, _TN), tg, v_s, k_p)
        for i, (c, p) in enumerate(units):
            rows = slice(c * CHUNK, (c + 1) * CHUNK)
            lanes = slice(p * PAIR, (p + 1) * PAIR)
            t_out[0, c, p] = t_m[i].astype(t_out.dtype)
            g_out[0, c, p] = g_m[i]
            rh_out[0, rows, lanes] = _unstack_heads(r_h[i]).astype(rh_out.dtype)
            yh_out[0, rows, lanes] = _unstack_heads(y_h[i])

    last = slice(n_waves * wave_rows - PREV_ROWS, n_waves * wave_rows)
    for w in range(n_waves):
        nxt = prep(w + 1) if w + 1 < n_waves else prep_first(zr_next, zl_next, f32(zr_ref, last), f32(zl_ref, last))
        _interleave(wave(w), nxt)


def _rwkv_local(z3, p, ts):
    b, s, _ = z3.shape
    n_chunks = ts // CHUNK
    wr, wl = 3 * WIDTH_A, LORA_PAD
    lora_block = wr // wl
    wave_rows = LOCAL_WAVE * CHUNK
    n_waves = ts // wave_rows
    assert n_waves >= 2 and ts % wave_rows == 0
    nxt = lambda c: (lambda bi, i: (bi, jnp.minimum((i + 1) * n_waves, s // wave_rows - 1), c))
    cur = lambda bi, i: (bi, i, 0)
    mat = lambda bi, i: (bi, i, 0, 0, 0)
    const = lambda bi, i: (0, 0)
    vec = pl.BlockSpec((1, WIDTH_A), const)
    seq_spec = pl.BlockSpec((1, ts, WIDTH_A), cur)
    mat_spec = pl.BlockSpec((1, n_chunks, N_PAIRS, PAIR, PAIR), mat)
    seq_shape = lambda dt: jax.ShapeDtypeStruct((b, s, WIDTH_A), dt)
    mat_shape = lambda dt: jax.ShapeDtypeStruct((b, s // CHUNK, N_PAIRS, PAIR, PAIR), dt)
    return pl.pallas_call(
        functools.partial(_rwkv_local_kernel, n_chunks=n_chunks),
        grid=(b, s // ts),
        in_specs=[pl.BlockSpec((1, ts, wr), cur),
                  pl.BlockSpec((1, ts, wl), lambda bi, i: (bi, i, lora_block)),
                  pl.BlockSpec((1, wave_rows, wr), nxt(0)),
                  pl.BlockSpec((1, wave_rows, wl), nxt(lora_block)),
                  pl.BlockSpec((1, wr), const), pl.BlockSpec((1, wl), const),
                  vec, pl.BlockSpec((LORA_W + LORA_A, WIDTH_A), const),
                  vec, pl.BlockSpec((LORA_W + LORA_A, WIDTH_A), const),
                  pl.BlockSpec((LORA_PAD - LORA_W - LORA_A, WIDTH_A), const),
                  vec, vec, vec, pl.BlockSpec((WIDTH_A, WIDTH_A), const)],
        out_specs=[mat_spec, mat_spec, seq_spec, seq_spec, seq_spec, seq_spec],
        out_shape=[mat_shape(BF16), mat_shape(F32), seq_shape(BF16), seq_shape(F32), seq_shape(F32), seq_shape(F32)],
        scratch_shapes=[pltpu.VMEM((1, ts, WIDTH_A), F32)] * 7 + [pltpu.VMEM((wave_rows, WIDTH_A), F32)] * 2,
        compiler_params=_cparams(("parallel", "arbitrary")),
        name="rwkv_local",
    )(z3, z3, z3, z3, p["mu_r"], p["mu_l"], p["w0"], p["w_up"], p["a0"], p["a_up"], p["g_up"],
      p["k_k"], p["k_a"], p["r_k"], p["ones"])


def _rwkv_state_kernel(t_ref, gm_ref, rh_ref, yh_ref, g_ref, bonus_ref, lng_ref, lnb_ref, ones_ref,
                       y_out, s_ref, *, n_chunks):
    @pl.when(pl.program_id(1) == 0)
    def _():
        s_ref[...] = jnp.zeros_like(s_ref)

    ones = ones_ref[...]
    state = [s_ref[p] for p in range(N_PAIRS)]
    entry = []
    for c in range(n_chunks):
        entry.append(list(state))
        state = [_mm(state[p], t_ref[0, c, p]) + gm_ref[0, c, p] for p in range(N_PAIRS)]
    for p in range(N_PAIRS):
        s_ref[p] = state[p]
    chunks = range(n_chunks)
    rows = [slice(c * CHUNK, (c + 1) * CHUNK) for c in chunks]
    y = [jnp.concatenate([_mm(rh_ref[0, rows[c], p * PAIR:(p + 1) * PAIR], entry[c][p], _NT)
                          for p in range(N_PAIRS)], axis=1) + yh_ref[0, rows[c], :] for c in chunks]
    mean = [_dot(v.astype(BF16), ones) * (1.0 / HEAD) for v in y]
    yc = [v - m for v, m in zip(y, mean)]
    var = [_dot((v * v).astype(BF16), ones) * (1.0 / HEAD) for v in yc]
    for c in chunks:
        yn = yc[c] * lax.rsqrt(var[c] + GN_EPS) * lng_ref[...] + lnb_ref[...]
        y_out[0, rows[c], :] = ((yn + bonus_ref[0, rows[c], :]) * g_ref[0, rows[c], :]).astype(y_out.dtype)


def _rwkv_state(t, gm, rh, yh, g, bonus, p, ts):
    b, s, _ = rh.shape
    n_chunks = ts // CHUNK
    cur = lambda bi, i: (bi, i, 0)
    mat = lambda bi, i: (bi, i, 0, 0, 0)
    const = lambda bi, i: (0, 0)
    seq_spec = pl.BlockSpec((1, ts, WIDTH_A), cur)
    mat_spec = pl.BlockSpec((1, n_chunks, N_PAIRS, PAIR, PAIR), mat)
    vec = pl.BlockSpec((1, WIDTH_A), const)
    return pl.pallas_call(
        functools.partial(_rwkv_state_kernel, n_chunks=n_chunks),
        grid=(b, s // ts),
        in_specs=[mat_spec, mat_spec, seq_spec, seq_spec, seq_spec, seq_spec, vec, vec,
                  pl.BlockSpec((WIDTH_A, WIDTH_A), const)],
        out_specs=seq_spec,
        out_shape=jax.ShapeDtypeStruct((b, s, WIDTH_A), BF16),
        scratch_shapes=[pltpu.VMEM((N_PAIRS, PAIR, PAIR), F32)],
        compiler_params=_cparams(("parallel", "arbitrary")),
        name="rwkv_state",
    )(t, gm, rh, yh, g, bonus, p["ln_g"], p["ln_b"], p["ones"])


def _rwkv_mixer(z3, p):
    t, gm, rh, yh, g, bonus = _rwkv_local(z3, p, ts=512)
    return _rwkv_state(t, gm, rh, yh, g, bonus, p, ts=512)


def _lru_kernel(x_ref, zp_ref, y_ref, cw_ref, cb_ref, wa_ref, ba_ref, wx_ref, bx_ref, lam_ref,
                o_ref, a_scr, b_scr, carry_ref, *, ts):
    i = pl.program_id(1)

    @pl.when(i == 0)
    def _():
        carry_ref[...] = jnp.zeros_like(carry_ref)

    x = x_ref[0].astype(F32)
    yb = y_ref[0].astype(F32)
    prev = jnp.where(i == 0, 0.0, zp_ref[0].astype(F32))
    ext = jnp.concatenate([prev, x], axis=0)
    xc = x * cw_ref[CONV_TAPS - 1:CONV_TAPS, :] + cb_ref[...]
    for back in range(1, CONV_TAPS):
        tap = CONV_TAPS - 1 - back
        xc = xc + pltpu.roll(ext, back, axis=0)[PREV_ROWS:, :] * cw_ref[tap:tap + 1, :]

    half = WIDTH_B // 2
    xcb = xc.astype(BF16)
    ga = jnp.concatenate([_dot(xcb[:, j * half:(j + 1) * half], wa_ref[j]) for j in range(2)], axis=1)
    gx = jnp.concatenate([_dot(xcb[:, j * half:(j + 1) * half], wx_ref[j]) for j in range(2)], axis=1)
    gate_a = _sigmoid(ga + ba_ref[...])
    gate_x = _sigmoid(gx + bx_ref[...])
    log_a = -LRU_C * gate_a * _softplus(-lam_ref[...])
    a = jnp.exp(log_a)
    mult = jnp.sqrt(jnp.maximum(-jnp.tanh(log_a) * (1.0 + a * a), 0.0))
    t_glob = i * ts + lax.broadcasted_iota(jnp.int32, (ts, WIDTH_B), 0)
    mult = jnp.where(t_glob == 0, 1.0, mult)
    a_scr[...] = a
    b_scr[...] = xc * gate_x * mult

    row8 = lax.broadcasted_iota(jnp.int32, (8, WIDTH_B), 0)

    def group_body(gi, h_prev):
        rows = pl.ds(pl.multiple_of(gi * 8, 8), 8)
        a8 = a_scr[rows, :]
        b8 = b_scr[rows, :]
        for sh in (1, 2, 4):
            ar = pltpu.roll(a8, sh, axis=0)
            br = pltpu.roll(b8, sh, axis=0)
            m = row8 >= sh
            b8 = jnp.where(m, a8 * br + b8, b8)
            a8 = jnp.where(m, a8 * ar, a8)
        h8 = a8 * h_prev + b8
        b_scr[rows, :] = h8
        return h8[7:8, :]

    carry_ref[0:1, :] = lax.fori_loop(0, ts // 8, group_body, carry_ref[0:1, :])
    h = b_scr[...]
    gelu = 0.5 * yb * (1.0 + jnp.tanh(math.sqrt(2.0 / math.pi) * (yb + 0.044715 * (yb * yb * yb))))
    o_ref[0] = (h * gelu).astype(o_ref.dtype)


def _lru_mixer(z3, x_block, p, ts):
    b, s, _ = z3.shape
    cur = lambda bi, i: (bi, i, 0)
    const2 = lambda bi, i: (0, 0)
    const3 = lambda bi, i: (0, 0, 0)
    vec = pl.BlockSpec((1, WIDTH_B), const2)
    half = WIDTH_B // 2
    return pl.pallas_call(
        functools.partial(_lru_kernel, ts=ts),
        grid=(b, s // ts),
        in_specs=[pl.BlockSpec((1, ts, WIDTH_B), lambda bi, i: (bi, i, x_block)),
                  pl.BlockSpec((1, PREV_ROWS, WIDTH_B),
                               lambda bi, i: (bi, jnp.maximum(i * (ts // PREV_ROWS) - 1, 0), x_block)),
                  pl.BlockSpec((1, ts, WIDTH_B), lambda bi, i: (bi, i, x_block + 1)),
                  pl.BlockSpec((CONV_TAPS, WIDTH_B), const2), vec,
                  pl.BlockSpec((2, half, half), const3), vec,
                  pl.BlockSpec((2, half, half), const3), vec, vec],
        out_specs=pl.BlockSpec((1, ts, WIDTH_B), cur),
        out_shape=jax.ShapeDtypeStruct((b, s, WIDTH_B), BF16),
        scratch_shapes=[pltpu.VMEM((ts, WIDTH_B), F32), pltpu.VMEM((ts, WIDTH_B), F32),
                        pltpu.VMEM((8, WIDTH_B), F32)],
        compiler_params=_cparams(("parallel", "arbitrary")),
        name="lru_mixer",
    )(z3, z3, z3, p["conv_w"], p["conv_b"], p["wa"], p["ba"], p["wx"], p["bx"], p["lam"])


def _attn_kernel(q0, q1, k0, k1, kp0, kp1, v0, v1, vp0, vp1, bias_ref, qg_ref, kg_ref, ones_ref,
                 o0, o1, l0, l1, *, dil, n_sub):
    j = pl.program_id(1)
    ones = ones_ref[...]
    lane = lax.broadcasted_iota(jnp.int32, (QBLK, GROUP_W), 1)
    in_head = [(lane >= h * HEAD) & (lane < (h + 1) * HEAD) for h in range(HEADS_PER_GROUP)]
    prev_valid = (lax.broadcasted_iota(jnp.int32, (HEADS_PER_GROUP * QBLK, 2 * QBLK), 1) >= QBLK) | (j > 0)

    def rows(start):
        return pl.ds(start, QBLK, stride=dil) if dil > 1 else pl.ds(start, QBLK)

    def take(lo, hi, start):
        return jnp.concatenate([lo[0, rows(start), :], hi[0, rows(start), :]], axis=1)

    def head_sumsq(x):
        return _dot((x * x).astype(BF16), ones) * (1.0 / HEAD)

    def select_heads(x):
        out = jnp.zeros((QBLK, GROUP_W), F32)
        for h, m in enumerate(in_head):
            out = jnp.where(m, x[h * QBLK:(h + 1) * QBLK, :], out)
        return out

    def wave(units):
        each = lambda f, *ls: [f(*xs) for xs in zip(*ls)]
        span = dil * QBLK
        starts = [s for s, _ in units]
        before = [s - span if dil > 1 or isinstance(s, int) else pl.multiple_of(s - span, QBLK) for s in starts]
        q_raw = [take(q0, q1, s) for s in starts]
        k_raw = [jnp.concatenate([take(kp0, kp1, s) if far else take(k0, k1, p), take(k0, k1, s)], axis=0)
                 for (s, far), p in zip(units, before)]
        vv = [jnp.concatenate([take(vp0, vp1, s) if far else take(v0, v1, p), take(v0, v1, s)],
                              axis=0).astype(BF16) for (s, far), p in zip(units, before)]
        q_ms = each(head_sumsq, q_raw)
        k_ms = each(head_sumsq, k_raw)
        q = each(lambda x, ms: x * lax.rsqrt(ms + RMS_EPS) * qg_ref[...] * (HEAD ** -0.5), q_raw, q_ms)
        kk = each(lambda x, ms: (x * lax.rsqrt(ms + RMS_EPS) * kg_ref[...]).astype(BF16), k_raw, k_ms)
        qs = each(lambda x: jnp.concatenate([jnp.where(m, x, 0.0) for m in in_head], axis=0).astype(BF16), q)
        logits = each(lambda a, b: _dot_nt(a, b) + bias_ref[...], qs, kk)
        logits = [jnp.where(prev_valid, lg, NEG_INF) if far else lg for lg, (_, far) in zip(logits, units)]
        mx = each(lambda lg: jnp.max(lg, axis=-1, keepdims=True), logits)
        pr = each(lambda lg, m: jnp.exp(lg - m), logits, mx)
        den = each(lambda p: jnp.sum(p, axis=-1, keepdims=True), pr)
        pv = each(lambda p, v, dn: _dot(p.astype(BF16), v) / dn, pr, vv, den)
        out = each(select_heads, pv)
        lse = each(lambda m, dn: select_heads(jnp.broadcast_to(m + jnp.log(dn), (HEADS_PER_GROUP * QBLK, GROUP_W))),
                   mx, den)
        for s, o, l in zip(starts, out, lse):
            o0[0, rows(s), :] = o[:, 0:PAIR]
            o1[0, rows(s), :] = o[:, PAIR:GROUP_W]
            l0[0, rows(s), :] = l[:, 0:PAIR]
            l1[0, rows(s), :] = l[:, PAIR:GROUP_W]

    def loop(lo, hi, body):
        def step(i, carry):
            body(i)
            return carry
        lax.fori_loop(lo, hi, step, 0)

    span = dil * QBLK
    if n_sub == 1:
        loop(0, dil // ATTN_WAVE, lambda i: wave([(i * ATTN_WAVE + u, True) for u in range(ATTN_WAVE)]))
    elif dil > 1:
        wave([(r, True) for r in range(dil)])
        loop(1, n_sub, lambda n: wave([(r + n * span, False) for r in range(dil)]))
    else:
        wave([(u * QBLK, u == 0) for u in range(ATTN_WAVE)])
        loop(1, n_sub // ATTN_WAVE,
             lambda i: wave([(pl.multiple_of((i * ATTN_WAVE + u) * QBLK, QBLK), False) for u in range(ATTN_WAVE)]))


def _attn_group(z3, col0, bias, qg, kg, ones, gi, dil):
    b, s, _ = z3.shape
    span = dil * QBLK
    n_sub = ATTN_TILE // span
    halves = GROUP_W // PAIR
    per_part = len(GROUPS) * halves

    def cur(part, half):
        c = col0 + part * per_part + gi * halves + half
        return pl.BlockSpec((1, ATTN_TILE, PAIR), lambda bi, j: (bi, j, c))

    def prev(part, half):
        c = col0 + part * per_part + gi * halves + half
        return pl.BlockSpec((1, span, PAIR), lambda bi, j: (bi, jnp.maximum(j * n_sub - 1, 0), c))

    const2 = lambda bi, j: (0, 0)
    out_spec = pl.BlockSpec((1, ATTN_TILE, PAIR), lambda bi, j: (bi, j, 0))
    return pl.pallas_call(
        functools.partial(_attn_kernel, dil=dil, n_sub=n_sub),
        grid=(b, s // ATTN_TILE),
        in_specs=[cur(0, 0), cur(0, 1), cur(1, 0), cur(1, 1), prev(1, 0), prev(1, 1),
                  cur(2, 0), cur(2, 1), prev(2, 0), prev(2, 1),
                  pl.BlockSpec((HEADS_PER_GROUP * QBLK, 2 * QBLK), const2),
                  pl.BlockSpec((1, GROUP_W), const2), pl.BlockSpec((1, GROUP_W), const2),
                  pl.BlockSpec((GROUP_W, GROUP_W), const2)],
        out_specs=[out_spec] * 4,
        out_shape=[jax.ShapeDtypeStruct((b, s, PAIR), F32)] * 4,
        compiler_params=_cparams(("parallel", "arbitrary")),
        name=f"dilated_attn_g{gi}",
    )(*([z3] * 10), bias, qg, kg, ones)


def _t5_bucket(dist):
    max_exact = N_BUCKETS // 2
    d = jnp.maximum(dist, 0)
    large = max_exact + (jnp.log(jnp.maximum(d, 1).astype(F32) / max_exact)
                         / math.log(MAX_DISTANCE / max_exact) * (N_BUCKETS - max_exact)).astype(jnp.int32)
    large = jnp.minimum(large, N_BUCKETS - 1)
    return jnp.where(d < max_exact, d, large)


def _attn_bias_tiles(rel_bias):
    tiles = []
    kj = jnp.arange(2 * QBLK)[None, :]
    rel = (jnp.arange(QBLK)[:, None] + QBLK) - kj
    for gi, (window, dil) in enumerate(GROUPS):
        band = (rel >= 0) & (rel <= window // dil)
        tab = rel_bias.astype(F32)[:, gi * HEADS_PER_GROUP:(gi + 1) * HEADS_PER_GROUP]
        onehot = (_t5_bucket(rel * dil)[..., None] == jnp.arange(N_BUCKETS)).astype(F32)
        bias = jnp.einsum("qkn,nh->hqk", onehot, tab, precision=lax.Precision.HIGHEST)
        tiles.append(jnp.where(band[None], bias, NEG_INF).reshape(HEADS_PER_GROUP * QBLK, 2 * QBLK))
    return tiles


def _merge_kernel(x_ref, ya_ref, yb_ref, *rest):
    n_g = len(GROUPS)
    attn = rest[:4 * n_g]
    zg_ref, pa_ref, pb_ref, pc_ref, wo_ref, out_ref = rest[4 * n_g:]
    d = x_ref.shape[-1]
    outs = [jnp.concatenate([attn[4 * g][...], attn[4 * g + 1][...]], axis=1) for g in range(n_g)]
    lses = [jnp.concatenate([attn[4 * g + 2][...], attn[4 * g + 3][...]], axis=1) for g in range(n_g)]
    m = functools.reduce(jnp.maximum, lses)
    es = [jnp.exp(l - m) for l in lses]
    yc = sum(o * e for o, e in zip(outs, es)) / sum(es)
    gate = lambda n: _sigmoid(zg_ref[:, n * d:(n + 1) * d].astype(F32))
    merged = (gate(0) * _dot(ya_ref[...].astype(BF16), pa_ref[...])
              + gate(1) * _dot(yb_ref[...].astype(BF16), pb_ref[...])
              + gate(2) * _dot(yc.astype(BF16), pc_ref[...]))
    out_ref[...] = x_ref[...] + _dot(merged.astype(BF16), wo_ref[...])


def _merge(x2d, ya, yb, attn, z2d, gate_block, p, tm):
    m, d = x2d.shape
    row = lambda w: pl.BlockSpec((tm, w), lambda i: (i, 0))
    full = lambda a: pl.BlockSpec(a.shape, lambda i: (0, 0))
    return pl.pallas_call(
        _merge_kernel,
        grid=(m // tm,),
        in_specs=[row(d), row(WIDTH_A), row(WIDTH_B)] + [row(PAIR)] * len(attn)
                 + [pl.BlockSpec((tm, 3 * d), lambda i: (i, gate_block)),
                    full(p["proj_a"]), full(p["proj_b"]), full(p["proj_c"]), full(p["w_out"])],
        out_specs=row(d),
        out_shape=jax.ShapeDtypeStruct((m, d), F32),
        compiler_params=_cparams(("parallel",)),
        name="merge",
    )(x2d, ya, yb, *attn, z2d, p["proj_a"], p["proj_b"], p["proj_c"], p["w_out"])


def _mlp_kernel(x_ref, g_ref, wu_ref, wd_ref, o_ref, h_ref, acc_ref):
    j = pl.program_id(1)

    @pl.when(j == 0)
    def _():
        x = x_ref[...]
        ms = jnp.mean(x * x, axis=-1, keepdims=True)
        h_ref[...] = (x * lax.rsqrt(ms + RMS_EPS) * g_ref[...]).astype(BF16)
        acc_ref[...] = jnp.zeros_like(acc_ref)

    u = jnp.maximum(_dot(h_ref[...], wu_ref[...]), 0.0)
    acc_ref[...] += _dot((u * u).astype(BF16), wd_ref[...])

    @pl.when(j == pl.num_programs(1) - 1)
    def _():
        o_ref[...] = x_ref[...] + acc_ref[...]


def _mlp(x2d, g, wu, wd, tm, tf):
    m, d = x2d.shape
    f = wu.shape[1]
    return pl.pallas_call(
        _mlp_kernel,
        grid=(m // tm, f // tf),
        in_specs=[pl.BlockSpec((tm, d), lambda i, j: (i, 0)),
                  pl.BlockSpec((1, d), lambda i, j: (0, 0)),
                  pl.BlockSpec((d, tf), lambda i, j: (0, j)),
                  pl.BlockSpec((tf, d), lambda i, j: (j, 0))],
        out_specs=pl.BlockSpec((tm, d), lambda i, j: (i, 0)),
        out_shape=jax.ShapeDtypeStruct((m, d), F32),
        scratch_shapes=[pltpu.VMEM((tm, d), BF16), pltpu.VMEM((tm, d), F32)],
        compiler_params=_cparams(("parallel", "arbitrary")),
        name="mlp",
    )(x2d, g, wu, wd)


def _pad_rows(w, lo, total):
    return jnp.pad(w, ((lo, total - lo - w.shape[0]), (0, 0)))


def _block_diag_halves(w):
    n, bd, _ = w.shape
    per = n // 2
    out = jnp.zeros((2, per * bd, per * bd), w.dtype)
    for i in range(n):
        j, q = divmod(i, per)
        out = out.at[j, q * bd:(q + 1) * bd, q * bd:(q + 1) * bd].set(w[i])
    return out.astype(BF16)


def _layer(x, l, bias_tiles, proj, prm):
    (norm_mix_g, _, rwkv_mu, rwkv_w0, rwkv_w_up, rwkv_a0, rwkv_a_up, rwkv_g_up, rwkv_k_k, rwkv_k_a,
     rwkv_r_k, rwkv_ln_g, rwkv_ln_b, proj_a, conv_w, conv_b, lru_wa, lru_ba, lru_wx, lru_bx, lru_lambda,
     proj_b, q_norm_g, k_norm_g, proj_c, w_out, norm_mlp_g, mlp_up, mlp_down) = [t[l] for t in prm]
    b, s, d = x.shape
    x2d = x.reshape(b * s, d)
    row = lambda t: t.reshape(1, -1).astype(F32)

    wz_all, z_b, z_g, z_c = proj
    c_rkv, c_lora = 3 * WIDTH_A, LORA_W + LORA_A + LORA_G
    z2d, zc2d = _norm_matmul(x2d, row(norm_mix_g), wz_all, l, z_c, 2048, 768)
    z3 = z2d.reshape(b, s, z_c)
    zc3 = zc2d.reshape(b, s, 3 * WIDTH_C)

    ones_a = _head_ones(WIDTH_A)
    pa = dict(
        mu_r=row(rwkv_mu[0:c_rkv]),
        mu_l=jnp.pad(row(rwkv_mu[c_rkv:]), ((0, 0), (0, LORA_PAD - c_lora))),
        w0=row(rwkv_w0), a0=row(rwkv_a0),
        w_up=_pad_rows(rwkv_w_up, 0, LORA_W + LORA_A).astype(BF16),
        a_up=_pad_rows(rwkv_a_up, LORA_W, LORA_W + LORA_A).astype(BF16),
        g_up=_pad_rows(rwkv_g_up, 0, LORA_PAD - LORA_W - LORA_A).astype(BF16),
        k_k=row(rwkv_k_k), k_a=row(rwkv_k_a), r_k=row(rwkv_r_k),
        ln_g=row(rwkv_ln_g), ln_b=row(rwkv_ln_b), ones=ones_a)
    ya = _rwkv_mixer(z3, pa)

    pb = dict(conv_w=conv_w.astype(F32), conv_b=row(conv_b), wa=_block_diag_halves(lru_wa), ba=row(lru_ba),
              wx=_block_diag_halves(lru_wx), bx=row(lru_bx), lam=row(lru_lambda))
    yb = _lru_mixer(z3, z_b // WIDTH_B, pb, ts=512)

    qg = jnp.tile(row(q_norm_g), (1, HEADS_PER_GROUP))
    kg = jnp.tile(row(k_norm_g), (1, HEADS_PER_GROUP))
    ones_c = _head_ones(GROUP_W)
    attn = []
    for gi, (_, dil) in enumerate(GROUPS):
        parts = _attn_group(zc3, 0, bias_tiles[gi], qg, kg, ones_c, gi, dil)
        attn += [t.reshape(b * s, PAIR) for t in parts]

    pm = dict(proj_a=proj_a.astype(BF16), proj_b=proj_b.astype(BF16), proj_c=proj_c.astype(BF16),
              w_out=w_out.astype(BF16))
    x1 = _merge(x2d, ya.reshape(b * s, WIDTH_A), yb.reshape(b * s, WIDTH_B), attn, z2d, z_g // (3 * d), pm, 512)
    x2 = _mlp(x1, row(norm_mlp_g), mlp_up.astype(BF16), mlp_down.astype(BF16), 1024, 1024)
    return x2.reshape(b, s, d)


def kernel(x, rel_bias, norm_mix_g, w_in, rwkv_mu, rwkv_w0, rwkv_w_up, rwkv_a0, rwkv_a_up, rwkv_g_up, rwkv_k_k, rwkv_k_a, rwkv_r_k, rwkv_ln_g, rwkv_ln_b, proj_a, conv_w, conv_b, lru_wa, lru_ba, lru_wx, lru_bx, lru_lambda, proj_b, q_norm_g, k_norm_g, proj_c, w_out, norm_mlp_g, mlp_up, mlp_down):
    prm = (norm_mix_g, w_in, rwkv_mu, rwkv_w0, rwkv_w_up, rwkv_a0, rwkv_a_up, rwkv_g_up, rwkv_k_k, rwkv_k_a,
           rwkv_r_k, rwkv_ln_g, rwkv_ln_b, proj_a, conv_w, conv_b, lru_wa, lru_ba, lru_wx, lru_bx, lru_lambda,
           proj_b, q_norm_g, k_norm_g, proj_c, w_out, norm_mlp_g, mlp_up, mlp_down)
    bias_tiles = _attn_bias_tiles(rel_bias)
    proj = _projection_weights(w_in.astype(F32), x.shape[-1])
    x = x.astype(F32)
    for l in range(norm_mix_g.shape[0]):
        x = _layer(x, l, bias_tiles, proj, prm)
    return x
```

```python
import functools
import math

import jax
import jax.numpy as jnp
from jax import lax
from jax.experimental import pallas as pl
from jax.experimental.pallas import tpu as pltpu

F32 = jnp.float32
BF16 = jnp.bfloat16

N_HEADS_A = 8
HEAD = 64
PAIR = 2 * HEAD
WIDTH_A = N_HEADS_A * HEAD
N_PAIRS = WIDTH_A // PAIR
CHUNK = 64
LOCAL_WAVE = 4
PREV_ROWS = 16
LRU_SLAB = 256
MXU_N = 256
LORA_W, LORA_A, LORA_G = 64, 64, 160
LORA_PAD = 384
GN_EPS = 64e-5
WIDTH_B = 512
LRU_BLOCK = 64
CONV_TAPS = 4
LRU_C = 8.0
GROUPS = ((128, 1), (512, 4), (2048, 16))
HEADS_PER_GROUP = 4
GROUP_W = HEADS_PER_GROUP * HEAD
WIDTH_C = len(GROUPS) * GROUP_W
QBLK = 128
ATTN_TILE = 2048
ATTN_WAVE = 4
N_BUCKETS = 32
MAX_DISTANCE = 2048
NEG_INF = -1e30
RMS_EPS = 1e-6
VMEM_LIMIT = 56 * 1024 * 1024


def _cparams(sem):
    return pltpu.CompilerParams(dimension_semantics=sem, vmem_limit_bytes=VMEM_LIMIT)


def _dot(a, b):
    return jnp.dot(a, b, preferred_element_type=F32)


def _dot_nt(a, b):
    return lax.dot_general(a, b, (((1,), (1,)), ((), ())), preferred_element_type=F32)


_NN = (((1,), (0,)), ((), ()))
_NT = (((1,), (1,)), ((), ()))
_TN = (((0,), (0,)), ((), ()))


def _mm(a, b, dims=_NN):
    return lax.dot_general(a.astype(BF16), b.astype(BF16), dims, preferred_element_type=F32)


def _sigmoid(x):
    return 1.0 / (1.0 + jnp.exp(-x))


def _softplus(x):
    return jnp.maximum(x, 0.0) + jnp.log1p(jnp.exp(-jnp.abs(x)))


def _head_ones(width):
    i = jnp.arange(width) // HEAD
    return (i[:, None] == i[None, :]).astype(BF16)


def _projection_kernel(x_ref, g_ref, w_ref, *rest, n_lo, copies, first_slab, tiles_per_seq):
    lru, (lo_ref, hi_ref, yb_ref, h_ref, zb_ref, hist_ref, carry_ref) = rest[:7], rest[7:]
    i, j = pl.program_id(0), pl.program_id(1)

    @pl.when(j == 0)
    def _():
        x = x_ref[...]
        ms = jnp.mean(x * x, axis=-1, keepdims=True)
        h_ref[...] = (x * lax.rsqrt(ms + RMS_EPS) * g_ref[...]).astype(BF16)

    s = j - first_slab
    rows = pl.ds(pl.multiple_of(s * LRU_SLAB, LRU_SLAB), LRU_SLAB)
    seq_start = jnp.logical_and(s == 0, lax.rem(i, tiles_per_seq) == 0)

    def slab_load():
        return (zb_ref[rows, 0:WIDTH_B].astype(F32), zb_ref[rows, WIDTH_B:2 * WIDTH_B].astype(F32),
                jnp.where(seq_start, 0.0, hist_ref[...]), jnp.where(seq_start, 0.0, carry_ref[0:1, :]))

    def slab_store(x, out, h_last):
        yb_ref[rows, :] = out.astype(yb_ref.dtype)
        hist_ref[...] = x[LRU_SLAB - PREV_ROWS:, :]
        carry_ref[0:1, :] = h_last

    def matmul(out_ref, copy):
        h = h_ref[...]
        for c0 in range(0, out_ref.shape[1], MXU_N):
            z = _dot(h, w_ref[:, c0:c0 + MXU_N])
            out_ref[:, c0:c0 + MXU_N] = z.astype(out_ref.dtype)
            if copy is not None:
                src, dst, n = copy
                lo_c, hi_c = max(src, c0), min(src + n, c0 + MXU_N)
                if lo_c < hi_c:
                    zb_ref[:, dst + lo_c - src:dst + hi_c - src] = z[:, lo_c - c0:hi_c - c0].astype(zb_ref.dtype)
            yield

    def step(out_ref, copy, with_slab):
        if not with_slab:
            _interleave(matmul(out_ref, copy))
        elif copy is None:
            x, yb, prev, h_prev = slab_load()
            _interleave(matmul(out_ref, None), _lru_slab(x, yb, prev, seq_start, h_prev, lru, slab_store))
        else:
            _interleave(matmul(out_ref, copy))
            x, yb, prev, h_prev = slab_load()
            _interleave(_lru_slab(x, yb, prev, seq_start, h_prev, lru, slab_store))

    plain = j < first_slab
    for jb in copies:
        plain = jnp.logical_and(plain, j != jb)
        pl.when(j == jb)(functools.partial(step, lo_ref, copies[jb], jb == first_slab))
    pl.when(plain)(functools.partial(step, lo_ref, None, False))
    pl.when(jnp.logical_and(j > first_slab, j < n_lo))(functools.partial(step, lo_ref, None, True))
    pl.when(j >= n_lo)(functools.partial(step, hi_ref, None, True))


def _projection(x2d, g, w_all, layer, lru, z_b, n_lo_cols, seq_len, tm, tn):
    m, d = x2d.shape
    n = w_all.shape[2]
    n_lo, n_blocks = n_lo_cols // tn, n // tn
    copies = {}
    for jb in range(n_blocks):
        lo_c, hi_c = max(jb * tn, z_b), min((jb + 1) * tn, z_b + 2 * WIDTH_B)
        if lo_c < hi_c:
            copies[jb] = (lo_c - jb * tn, lo_c - z_b, hi_c - lo_c)
    first_slab = max(copies)
    assert first_slab < n_lo and n_blocks - first_slab == tm // LRU_SLAB and seq_len % tm == 0
    const2 = lambda i, j: (0, 0)
    vec = pl.BlockSpec((1, WIDTH_B), const2)
    half = WIDTH_B // 2
    mat = pl.BlockSpec((2, half, half), lambda i, j: (0, 0, 0))
    return pl.pallas_call(
        functools.partial(_projection_kernel, n_lo=n_lo, copies=copies, first_slab=first_slab,
                          tiles_per_seq=seq_len // tm),
        grid=(m // tm, n_blocks),
        in_specs=[pl.BlockSpec((tm, d), lambda i, j: (i, 0)),
                  pl.BlockSpec((1, d), const2),
                  pl.BlockSpec((None, d, tn), lambda i, j: (layer, 0, j)),
                  pl.BlockSpec((CONV_TAPS, WIDTH_B), const2), vec, mat, vec, mat, vec, vec],
        out_specs=[pl.BlockSpec((tm, tn), lambda i, j: (i, jnp.minimum(j, n_lo - 1))),
                   pl.BlockSpec((tm, tn), lambda i, j: (i, jnp.maximum(j - n_lo, 0))),
                   pl.BlockSpec((tm, WIDTH_B), lambda i, j: (i, 0))],
        out_shape=[jax.ShapeDtypeStruct((m, n_lo_cols), BF16),
                   jax.ShapeDtypeStruct((m, n - n_lo_cols), F32),
                   jax.ShapeDtypeStruct((m, WIDTH_B), BF16)],
        scratch_shapes=[pltpu.VMEM((tm, d), BF16), pltpu.VMEM((tm, 2 * WIDTH_B), BF16),
                        pltpu.VMEM((PREV_ROWS, WIDTH_B), F32), pltpu.VMEM((8, WIDTH_B), F32)],
        compiler_params=_cparams(("arbitrary", "arbitrary")),
        name="projection",
    )(x2d, g, w_all, lru["conv_w"], lru["conv_b"], lru["wa"], lru["ba"], lru["wx"], lru["bx"], lru["lam"])


def _relayout_kernel(w_ref, o_ref, *, moves, width):
    rows = w_ref.shape[1]
    end = 0
    for src, dst, n in moves:
        if dst > end:
            o_ref[0, :, end:dst] = jnp.zeros((rows, dst - end), o_ref.dtype)
        o_ref[0, :, dst:dst + n] = w_ref[0, :, src:src + n].astype(o_ref.dtype)
        end = dst + n
    if end < width:
        o_ref[0, :, end:width] = jnp.zeros((rows, width - end), o_ref.dtype)


def _projection_weights(w_in, d, tr=256):
    n_layers, rows, n_in = w_in.shape
    c_rkv, c_lora = 3 * WIDTH_A, LORA_W + LORA_A + LORA_G
    o_b = c_rkv + c_lora
    o_c = o_b + 2 * WIDTH_B
    o_g = o_c + 3 * WIDTH_C
    z_b = -(-(c_rkv + LORA_PAD) // WIDTH_B) * WIDTH_B
    z_g = -(-(z_b + 2 * WIDTH_B) // (3 * d)) * (3 * d)
    z_c = z_g + 3 * d
    width = z_c + 3 * WIDTH_C
    moves = ((0, 0, o_b), (o_b, z_b, o_c - o_b), (o_g, z_g, n_in - o_g), (o_c, z_c, o_g - o_c))
    wz = pl.pallas_call(
        functools.partial(_relayout_kernel, moves=moves, width=width),
        grid=(n_layers, rows // tr),
        in_specs=[pl.BlockSpec((1, tr, n_in), lambda l, i: (l, i, 0))],
        out_specs=pl.BlockSpec((1, tr, width), lambda l, i: (l, i, 0)),
        out_shape=jax.ShapeDtypeStruct((n_layers, rows, width), BF16),
        compiler_params=_cparams(("parallel", "parallel")),
        name="projection_weights",
    )(w_in)
    return wz, z_b, z_g, z_c


def _shift_rows(cur, prev):
    prev_row = prev[PREV_ROWS - 1:PREV_ROWS, :]
    rolled = pltpu.roll(cur, 1, axis=0)
    row = lax.broadcasted_iota(jnp.int32, cur.shape, 0)
    return jnp.where(row == 0, prev_row, rolled)


def _interleave(*stages):
    live = list(stages)
    while live:
        for gen in list(live):
            try:
                next(gen)
            except StopIteration:
                live.remove(gen)


def _rwkv_prep(zr, zl, prev_r, prev_l, rows, params, scan, put_gate, put_bonus):
    mur_ref, mul_ref, w0_ref, wup_ref, a0_ref, aup_ref, gup_ref, kk_ref, ka_ref, rk_ref, ones_ref = params
    r_out, k_out, v_out, lg_out, lgp_out, as_out, bs_out = scan
    fr = zr + (_shift_rows(zr, prev_r) - zr) * mur_ref[...]
    fl = zl + (_shift_rows(zl, prev_l) - zl) * mul_ref[...]
    r = fr[:, 0:WIDTH_A]
    k = fr[:, WIDTH_A:2 * WIDTH_A]
    v = fr[:, 2 * WIDTH_A:3 * WIDTH_A]
    x_wa = fl[:, 0:LORA_W + LORA_A]
    x_g = fl[:, LORA_W + LORA_A:LORA_PAD]
    ones = ones_ref[...]
    r_out[0, rows, :] = r
    v_out[0, rows, :] = v
    yield

    w = -_softplus(-(w0_ref[...] + _dot(jnp.tanh(x_wa).astype(BF16), wup_ref[...]))) - 0.5
    lw = -jnp.exp(w)
    pos = lax.broadcasted_iota(jnp.int32, lw.shape, 0) & (CHUNK - 1)
    lg = lw
    for sh in [1 << i for i in range(int(math.log2(CHUNK)))]:
        lg = lg + jnp.where(pos >= sh, pltpu.roll(lg, sh, axis=0), 0.0)
    lg_out[0, rows, :] = lg
    lgp_out[0, rows, :] = lg - lw
    yield

    a = _sigmoid(a0_ref[...] + _dot(x_wa.astype(BF16), aup_ref[...]))
    k2 = k * (1.0 + (a - 1.0) * ka_ref[...])
    k_out[0, rows, :] = k2
    yield

    kk = k * kk_ref[...]
    kk = kk / jnp.maximum(jnp.sqrt(_dot((kk * kk).astype(BF16), ones)), 1e-12)
    as_out[0, rows, :] = -kk
    bs_out[0, rows, :] = kk * a
    yield

    put_gate(_dot(_sigmoid(x_g).astype(BF16), gup_ref[...]))
    yield

    put_bonus(_dot((r * k2 * rk_ref[...]).astype(BF16), ones) * v)


def _stack_heads(x):
    lo = lax.broadcasted_iota(jnp.int32, x.shape, 1) < HEAD
    return jnp.concatenate([jnp.where(lo, x, 0.0), jnp.where(lo, 0.0, x)], axis=0)


def _unstack_heads(x):
    return x[0:CHUNK, :] + x[CHUNK:2 * CHUNK, :]


def _rwkv_local_kernel(*refs, n_chunks):
    (zr_ref, zl_ref, zr_next, zl_next), params = refs[:4], refs[4:15]
    t_out, g_out, rh_out, yh_out, gate_out, bonus_out = refs[15:21]
    scan, (gate_carry, bonus_carry) = refs[21:28], refs[28:]
    r_ref, k_ref, v_ref, lg_ref, lgp_ref, as_ref, bs_ref = scan
    two_c = 2 * CHUNK
    row = lax.broadcasted_iota(jnp.int32, (two_c, two_c), 0)
    col = lax.broadcasted_iota(jnp.int32, (two_c, two_c), 1)
    strict = col < row
    incl = col <= row
    eye = (col == row).astype(F32)
    wave_rows = LOCAL_WAVE * CHUNK
    n_waves = n_chunks // LOCAL_WAVE
    first_rows = slice(0, wave_rows)
    f32 = lambda ref, rows: ref[0, rows, :].astype(F32)

    def put(ref, rows):
        def store(val):
            ref[rows] = val
        return store

    def prep(w):
        rows = slice(w * wave_rows, (w + 1) * wave_rows)
        before = slice(w * wave_rows - PREV_ROWS, w * wave_rows)
        return _rwkv_prep(f32(zr_ref, rows), f32(zl_ref, rows), f32(zr_ref, before), f32(zl_ref, before), rows,
                          params, scan, put(gate_out, (0, rows)), put(bonus_out, (0, rows)))

    def prep_first(zr_src, zl_src, prev_r, prev_l):
        return _rwkv_prep(f32(zr_src, first_rows), f32(zl_src, first_rows), prev_r, prev_l, first_rows,
                          params, scan, put(gate_carry, slice(None)), put(bonus_carry, slice(None)))

    @pl.when(pl.program_id(1) == 0)
    def _():
        zero = lambda ref: jnp.zeros((PREV_ROWS, ref.shape[-1]), F32)
        _interleave(prep_first(zr_ref, zl_ref, zero(zr_ref), zero(zl_ref)))

    gate_out[0, first_rows, :] = gate_carry[...]
    bonus_out[0, first_rows, :] = bonus_carry[...]

    def wave(w):
        c0 = w * LOCAL_WAVE
        units = [(c, p) for c in range(c0, min(c0 + LOCAL_WAVE, n_chunks)) for p in range(N_PAIRS)]
        at = lambda ref: [ref[0, c * CHUNK:(c + 1) * CHUNK, p * PAIR:(p + 1) * PAIR] for c, p in units]
        each = lambda f, *ls: [f(*xs) for xs in zip(*ls)]
        lg = at(lg_ref)
        lg_end = each(lambda x: x[CHUNK - 1:CHUNK, :], lg)
        e_neg = each(lambda x: jnp.exp(-x), lg)
        e_end = each(lambda x, xe: jnp.exp(xe - x), lg, lg_end)
        a_s, b_s, kk = at(as_ref), at(bs_ref), at(k_ref)
        a_t = each(lambda x, gp: _stack_heads(x * jnp.exp(gp)), a_s, at(lgp_ref))
        r_t = each(lambda x, g: _stack_heads(x * jnp.exp(g)), at(r_ref), lg)
        b_t = each(lambda x, e: _stack_heads(x * e), b_s, e_neg)
        k_t = each(lambda x, e: _stack_heads(x * e), kk, e_neg)
        b_p = each(lambda x, e: _stack_heads(x * e), b_s, e_end)
        k_p = each(lambda x, e: _stack_heads(x * e), kk, e_end)
        v_s = each(_stack_heads, at(v_ref))
        yield

        vcat = lambda x, y: jnp.concatenate([x, y], axis=0)
        hcat = lambda x, y: jnp.concatenate([x, y], axis=1)
        top = lambda x: x[0:two_c]
        bot = lambda x: x[two_c:2 * two_c]
        left = lambda x: x[:, 0:two_c]
        right = lambda x: x[:, two_c:2 * two_c]

        prod = each(lambda a, r, b, k: _mm(vcat(a, r), vcat(b, k), _NT), a_t, r_t, b_t, k_t)
        l_ab = each(lambda x: jnp.where(strict, left(top(x)), 0.0), prod)
        l_ak = each(lambda x: jnp.where(strict, right(top(x)), 0.0), prod)
        l_rb = each(lambda x: jnp.where(incl, left(bot(x)), 0.0), prod)
        l_rk = each(lambda x: jnp.where(incl, right(bot(x)), 0.0), prod)
        yield

        inv = each(lambda x: eye + x, l_ab)
        pw = each(lambda x: _mm(x, x), l_ab)
        yield
        for _ in range(int(math.log2(CHUNK)) - 2):
            both = each(lambda p, x: _mm(vcat(p, x), p), pw, inv)
            pw = each(top, both)
            inv = each(lambda x, y: x + bot(y), inv, both)
            yield
        inv = each(lambda x, p: x + _mm(x, p), inv, pw)
        yield

        lv_rkv = each(lambda l1, l2, v: _mm(vcat(l1, l2), v), l_ak, l_rk, v_s)
        yield
        aw_h = each(lambda m, a, x: _mm(m, hcat(a, top(x))), inv, a_t, lv_rkv)
        yield
        l_aw = each(_mm, l_rb, aw_h)
        r_h = each(lambda x, y: x + left(y), r_t, l_aw)
        y_h = each(lambda y, x: right(y) + bot(x), l_aw, lv_rkv)
        yield
        tg = each(lambda x, bp: _mm(x, bp, _TN), aw_h, b_p)
        t_m = each(lambda ge, x: eye * jnp.exp(ge) + top(x), lg_end, tg)
        yield
        g_m = each(lambda x, v, kp: bot(x) + _mm(v, kp, _TN), tg, v_s, k_p)
        for i, (c, p) in enumerate(units):
            rows = slice(c * CHUNK, (c + 1) * CHUNK)
            lanes = slice(p * PAIR, (p + 1) * PAIR)
            t_out[0, c, p] = t_m[i].astype(t_out.dtype)
            g_out[0, c, p] = g_m[i]
            rh_out[0, rows, lanes] = _unstack_heads(r_h[i]).astype(rh_out.dtype)
            yh_out[0, rows, lanes] = _unstack_heads(y_h[i])

    last = slice(n_waves * wave_rows - PREV_ROWS, n_waves * wave_rows)
    for w in range(n_waves):
        nxt = prep(w + 1) if w + 1 < n_waves else prep_first(zr_next, zl_next, f32(zr_ref, last), f32(zl_ref, last))
        _interleave(wave(w), nxt)


def _rwkv_local(z3, p, ts):
    b, s, _ = z3.shape
    n_chunks = ts // CHUNK
    wr, wl = 3 * WIDTH_A, LORA_PAD
    lora_block = wr // wl
    wave_rows = LOCAL_WAVE * CHUNK
    n_waves = ts // wave_rows
    assert n_waves >= 2 and ts % wave_rows == 0
    nxt = lambda c: (lambda bi, i: (bi, jnp.minimum((i + 1) * n_waves, s // wave_rows - 1), c))
    cur = lambda bi, i: (bi, i, 0)
    mat = lambda bi, i: (bi, i, 0, 0, 0)
    const = lambda bi, i: (0, 0)
    vec = pl.BlockSpec((1, WIDTH_A), const)
    seq_spec = pl.BlockSpec((1, ts, WIDTH_A), cur)
    mat_spec = pl.BlockSpec((1, n_chunks, N_PAIRS, PAIR, PAIR), mat)
    seq_shape = lambda dt: jax.ShapeDtypeStruct((b, s, WIDTH_A), dt)
    mat_shape = lambda dt: jax.ShapeDtypeStruct((b, s // CHUNK, N_PAIRS, PAIR, PAIR), dt)
    return pl.pallas_call(
        functools.partial(_rwkv_local_kernel, n_chunks=n_chunks),
        grid=(b, s // ts),
        in_specs=[pl.BlockSpec((1, ts, wr), cur),
                  pl.BlockSpec((1, ts, wl), lambda bi, i: (bi, i, lora_block)),
                  pl.BlockSpec((1, wave_rows, wr), nxt(0)),
                  pl.BlockSpec((1, wave_rows, wl), nxt(lora_block)),
                  pl.BlockSpec((1, wr), const), pl.BlockSpec((1, wl), const),
                  vec, pl.BlockSpec((LORA_W + LORA_A, WIDTH_A), const),
                  vec, pl.BlockSpec((LORA_W + LORA_A, WIDTH_A), const),
                  pl.BlockSpec((LORA_PAD - LORA_W - LORA_A, WIDTH_A), const),
                  vec, vec, vec, pl.BlockSpec((WIDTH_A, WIDTH_A), const)],
        out_specs=[mat_spec, mat_spec, seq_spec, seq_spec, seq_spec, seq_spec],
        out_shape=[mat_shape(BF16), mat_shape(F32), seq_shape(BF16), seq_shape(F32), seq_shape(F32), seq_shape(F32)],
        scratch_shapes=[pltpu.VMEM((1, ts, WIDTH_A), F32)] * 7 + [pltpu.VMEM((wave_rows, WIDTH_A), F32)] * 2,
        compiler_params=_cparams(("parallel", "arbitrary")),
        name="rwkv_local",
    )(z3, z3, z3, z3, p["mu_r"], p["mu_l"], p["w0"], p["w_up"], p["a0"], p["a_up"], p["g_up"],
      p["k_k"], p["k_a"], p["r_k"], p["ones"])


def _rwkv_state_kernel(t_ref, gm_ref, rh_ref, yh_ref, g_ref, bonus_ref, lng_ref, lnb_ref, ones_ref,
                       y_out, s_ref, *, n_chunks):
    @pl.when(pl.program_id(1) == 0)
    def _():
        s_ref[...] = jnp.zeros_like(s_ref)

    ones = ones_ref[...]
    state = [s_ref[p] for p in range(N_PAIRS)]
    entry = []
    for c in range(n_chunks):
        entry.append(list(state))
        state = [_mm(state[p], t_ref[0, c, p]) + gm_ref[0, c, p] for p in range(N_PAIRS)]
    for p in range(N_PAIRS):
        s_ref[p] = state[p]
    chunks = range(n_chunks)
    rows = [slice(c * CHUNK, (c + 1) * CHUNK) for c in chunks]
    y = [jnp.concatenate([_mm(rh_ref[0, rows[c], p * PAIR:(p + 1) * PAIR], entry[c][p], _NT)
                          for p in range(N_PAIRS)], axis=1) + yh_ref[0, rows[c], :] for c in chunks]
    mean = [_dot(v.astype(BF16), ones) * (1.0 / HEAD) for v in y]
    yc = [v - m for v, m in zip(y, mean)]
    var = [_dot((v * v).astype(BF16), ones) * (1.0 / HEAD) for v in yc]
    for c in chunks:
        yn = yc[c] * lax.rsqrt(var[c] + GN_EPS) * lng_ref[...] + lnb_ref[...]
        y_out[0, rows[c], :] = ((yn + bonus_ref[0, rows[c], :]) * g_ref[0, rows[c], :]).astype(y_out.dtype)


def _rwkv_state(t, gm, rh, yh, g, bonus, p, ts):
    b, s, _ = rh.shape
    n_chunks = ts // CHUNK
    cur = lambda bi, i: (bi, i, 0)
    mat = lambda bi, i: (bi, i, 0, 0, 0)
    const = lambda bi, i: (0, 0)
    seq_spec = pl.BlockSpec((1, ts, WIDTH_A), cur)
    mat_spec = pl.BlockSpec((1, n_chunks, N_PAIRS, PAIR, PAIR), mat)
    vec = pl.BlockSpec((1, WIDTH_A), const)
    return pl.pallas_call(
        functools.partial(_rwkv_state_kernel, n_chunks=n_chunks),
        grid=(b, s // ts),
        in_specs=[mat_spec, mat_spec, seq_spec, seq_spec, seq_spec, seq_spec, vec, vec,
                  pl.BlockSpec((WIDTH_A, WIDTH_A), const)],
        out_specs=seq_spec,
        out_shape=jax.ShapeDtypeStruct((b, s, WIDTH_A), BF16),
        scratch_shapes=[pltpu.VMEM((N_PAIRS, PAIR, PAIR), F32)],
        compiler_params=_cparams(("parallel", "arbitrary")),
        name="rwkv_state",
    )(t, gm, rh, yh, g, bonus, p["ln_g"], p["ln_b"], p["ones"])


def _rwkv_mixer(z3, p):
    t, gm, rh, yh, g, bonus = _rwkv_local(z3, p, ts=512)
    return _rwkv_state(t, gm, rh, yh, g, bonus, p, ts=512)


def _lru_slab(x, yb, prev, seq_start, h_prev, params, emit):
    cw_ref, cb_ref, wa_ref, ba_ref, wx_ref, bx_ref, lam_ref = params
    n_rows = x.shape[0]
    ext = jnp.concatenate([prev, x], axis=0)
    xc = x * cw_ref[CONV_TAPS - 1:CONV_TAPS, :] + cb_ref[...]
    for back in range(1, CONV_TAPS):
        tap = CONV_TAPS - 1 - back
        xc = xc + pltpu.roll(ext, back, axis=0)[PREV_ROWS:, :] * cw_ref[tap:tap + 1, :]

    half = WIDTH_B // 2
    xcb = xc.astype(BF16)
    ga = jnp.concatenate([_dot(xcb[:, j * half:(j + 1) * half], wa_ref[j]) for j in range(2)], axis=1)
    gx = jnp.concatenate([_dot(xcb[:, j * half:(j + 1) * half], wx_ref[j]) for j in range(2)], axis=1)
    gate_a = _sigmoid(ga + ba_ref[...])
    gate_x = _sigmoid(gx + bx_ref[...])
    log_a = -LRU_C * gate_a * _softplus(-lam_ref[...])
    a = jnp.exp(log_a)
    mult = jnp.sqrt(jnp.maximum(-jnp.tanh(log_a) * (1.0 + a * a), 0.0))
    xg = xc * gate_x
    b = xg * mult
    yield

    row8 = lax.broadcasted_iota(jnp.int32, (8, WIDTH_B), 0)
    h, hs = h_prev, []
    for g in range(n_rows // 8):
        a8, b8 = a[8 * g:8 * g + 8, :], b[8 * g:8 * g + 8, :]
        if g == 0:
            b8 = jnp.where(jnp.logical_and(row8 == 0, seq_start), xg[0:8, :], b8)
        for sh in (1, 2, 4):
            ar = pltpu.roll(a8, sh, axis=0)
            br = pltpu.roll(b8, sh, axis=0)
            m = row8 >= sh
            b8 = jnp.where(m, a8 * br + b8, b8)
            a8 = jnp.where(m, a8 * ar, a8)
        h8 = a8 * h + b8
        hs.append(h8)
        h = h8[7:8, :]
    yield
    gelu = 0.5 * yb * (1.0 + jnp.tanh(math.sqrt(2.0 / math.pi) * (yb + 0.044715 * (yb * yb * yb))))
    emit(x, jnp.concatenate(hs, axis=0) * gelu, h)


def _attn_kernel(q0, q1, k0, k1, kp0, kp1, v0, v1, vp0, vp1, bias_ref, qg_ref, kg_ref, ones_ref,
                 o0, o1, l0, l1, *, dil, n_sub):
    j = pl.program_id(1)
    ones = ones_ref[...]
    lane = lax.broadcasted_iota(jnp.int32, (QBLK, GROUP_W), 1)
    in_head = [(lane >= h * HEAD) & (lane < (h + 1) * HEAD) for h in range(HEADS_PER_GROUP)]
    prev_valid = (lax.broadcasted_iota(jnp.int32, (HEADS_PER_GROUP * QBLK, 2 * QBLK), 1) >= QBLK) | (j > 0)

    def rows(start):
        return pl.ds(start, QBLK, stride=dil) if dil > 1 else pl.ds(start, QBLK)

    def take(lo, hi, start):
        return jnp.concatenate([lo[0, rows(start), :], hi[0, rows(start), :]], axis=1)

    def head_sumsq(x):
        return _dot((x * x).astype(BF16), ones) * (1.0 / HEAD)

    def select_heads(x):
        out = jnp.zeros((QBLK, GROUP_W), F32)
        for h, m in enumerate(in_head):
            out = jnp.where(m, x[h * QBLK:(h + 1) * QBLK, :], out)
        return out

    def wave(units):
        each = lambda f, *ls: [f(*xs) for xs in zip(*ls)]
        span = dil * QBLK
        starts = [s for s, _ in units]
        before = [s - span if dil > 1 or isinstance(s, int) else pl.multiple_of(s - span, QBLK) for s in starts]
        q_raw = [take(q0, q1, s) for s in starts]
        k_raw = [jnp.concatenate([take(kp0, kp1, s) if far else take(k0, k1, p), take(k0, k1, s)], axis=0)
                 for (s, far), p in zip(units, before)]
        vv = [jnp.concatenate([take(vp0, vp1, s) if far else take(v0, v1, p), take(v0, v1, s)],
                              axis=0).astype(BF16) for (s, far), p in zip(units, before)]
        q_ms = each(head_sumsq, q_raw)
        k_ms = each(head_sumsq, k_raw)
        q = each(lambda x, ms: x * lax.rsqrt(ms + RMS_EPS) * qg_ref[...] * (HEAD ** -0.5), q_raw, q_ms)
        kk = each(lambda x, ms: (x * lax.rsqrt(ms + RMS_EPS) * kg_ref[...]).astype(BF16), k_raw, k_ms)
        qs = each(lambda x: jnp.concatenate([jnp.where(m, x, 0.0) for m in in_head], axis=0).astype(BF16), q)
        logits = each(lambda a, b: _dot_nt(a, b) + bias_ref[...], qs, kk)
        logits = [jnp.where(prev_valid, lg, NEG_INF) if far else lg for lg, (_, far) in zip(logits, units)]
        mx = each(lambda lg: jnp.max(lg, axis=-1, keepdims=True), logits)
        pr = each(lambda lg, m: jnp.exp(lg - m), logits, mx)
        den = each(lambda p: jnp.sum(p, axis=-1, keepdims=True), pr)
        pv = each(lambda p, v, dn: _dot(p.astype(BF16), v) / dn, pr, vv, den)
        out = each(select_heads, pv)
        lse = each(lambda m, dn: select_heads(jnp.broadcast_to(m + jnp.log(dn), (HEADS_PER_GROUP * QBLK, GROUP_W))),
                   mx, den)
        for s, o, l in zip(starts, out, lse):
            o0[0, rows(s), :] = o[:, 0:PAIR]
            o1[0, rows(s), :] = o[:, PAIR:GROUP_W]
            l0[0, rows(s), :] = l[:, 0:PAIR]
            l1[0, rows(s), :] = l[:, PAIR:GROUP_W]

    def loop(lo, hi, body):
        def step(i, carry):
            body(i)
            return carry
        lax.fori_loop(lo, hi, step, 0)

    span = dil * QBLK
    if n_sub == 1:
        loop(0, dil // ATTN_WAVE, lambda i: wave([(i * ATTN_WAVE + u, True) for u in range(ATTN_WAVE)]))
    elif dil > 1:
        wave([(r, True) for r in range(dil)])
        loop(1, n_sub, lambda n: wave([(r + n * span, False) for r in range(dil)]))
    else:
        wave([(u * QBLK, u == 0) for u in range(ATTN_WAVE)])
        loop(1, n_sub // ATTN_WAVE,
             lambda i: wave([(pl.multiple_of((i * ATTN_WAVE + u) * QBLK, QBLK), False) for u in range(ATTN_WAVE)]))


def _attn_group(z3, col0, bias, qg, kg, ones, gi, dil):
    b, s, _ = z3.shape
    span = dil * QBLK
    n_sub = ATTN_TILE // span
    halves = GROUP_W // PAIR
    per_part = len(GROUPS) * halves

    def cur(part, half):
        c = col0 + part * per_part + gi * halves + half
        return pl.BlockSpec((1, ATTN_TILE, PAIR), lambda bi, j: (bi, j, c))

    def prev(part, half):
        c = col0 + part * per_part + gi * halves + half
        return pl.BlockSpec((1, span, PAIR), lambda bi, j: (bi, jnp.maximum(j * n_sub - 1, 0), c))

    const2 = lambda bi, j: (0, 0)
    out_spec = pl.BlockSpec((1, ATTN_TILE, PAIR), lambda bi, j: (bi, j, 0))
    return pl.pallas_call(
        functools.partial(_attn_kernel, dil=dil, n_sub=n_sub),
        grid=(b, s // ATTN_TILE),
        in_specs=[cur(0, 0), cur(0, 1), cur(1, 0), cur(1, 1), prev(1, 0), prev(1, 1),
                  cur(2, 0), cur(2, 1), prev(2, 0), prev(2, 1),
                  pl.BlockSpec((HEADS_PER_GROUP * QBLK, 2 * QBLK), const2),
                  pl.BlockSpec((1, GROUP_W), const2), pl.BlockSpec((1, GROUP_W), const2),
                  pl.BlockSpec((GROUP_W, GROUP_W), const2)],
        out_specs=[out_spec] * 4,
        out_shape=[jax.ShapeDtypeStruct((b, s, PAIR), F32)] * 4,
        compiler_params=_cparams(("parallel", "arbitrary")),
        name=f"dilated_attn_g{gi}",
    )(*([z3] * 10), bias, qg, kg, ones)


def _t5_bucket(dist):
    max_exact = N_BUCKETS // 2
    d = jnp.maximum(dist, 0)
    large = max_exact + (jnp.log(jnp.maximum(d, 1).astype(F32) / max_exact)
                         / math.log(MAX_DISTANCE / max_exact) * (N_BUCKETS - max_exact)).astype(jnp.int32)
    large = jnp.minimum(large, N_BUCKETS - 1)
    return jnp.where(d < max_exact, d, large)


def _attn_bias_tiles(rel_bias):
    tiles = []
    kj = jnp.arange(2 * QBLK)[None, :]
    rel = (jnp.arange(QBLK)[:, None] + QBLK) - kj
    for gi, (window, dil) in enumerate(GROUPS):
        band = (rel >= 0) & (rel <= window // dil)
        tab = rel_bias.astype(F32)[:, gi * HEADS_PER_GROUP:(gi + 1) * HEADS_PER_GROUP]
        onehot = (_t5_bucket(rel * dil)[..., None] == jnp.arange(N_BUCKETS)).astype(F32)
        bias = jnp.einsum("qkn,nh->hqk", onehot, tab, precision=lax.Precision.HIGHEST)
        tiles.append(jnp.where(band[None], bias, NEG_INF).reshape(HEADS_PER_GROUP * QBLK, 2 * QBLK))
    return tiles


def _merge_kernel(x_ref, ya_ref, yb_ref, *rest):
    n_g = len(GROUPS)
    attn = rest[:4 * n_g]
    zg_ref, pa_ref, pb_ref, pc_ref, wo_ref, out_ref = rest[4 * n_g:]
    d = x_ref.shape[-1]
    outs = [jnp.concatenate([attn[4 * g][...], attn[4 * g + 1][...]], axis=1) for g in range(n_g)]
    lses = [jnp.concatenate([attn[4 * g + 2][...], attn[4 * g + 3][...]], axis=1) for g in range(n_g)]
    m = functools.reduce(jnp.maximum, lses)
    es = [jnp.exp(l - m) for l in lses]
    yc = sum(o * e for o, e in zip(outs, es)) / sum(es)
    gate = lambda n: _sigmoid(zg_ref[:, n * d:(n + 1) * d].astype(F32))
    merged = (gate(0) * _dot(ya_ref[...].astype(BF16), pa_ref[...])
              + gate(1) * _dot(yb_ref[...].astype(BF16), pb_ref[...])
              + gate(2) * _dot(yc.astype(BF16), pc_ref[...]))
    out_ref[...] = x_ref[...] + _dot(merged.astype(BF16), wo_ref[...])


def _merge(x2d, ya, yb, attn, z2d, gate_block, p, tm):
    m, d = x2d.shape
    row = lambda w: pl.BlockSpec((tm, w), lambda i: (i, 0))
    full = lambda a: pl.BlockSpec(a.shape, lambda i: (0, 0))
    return pl.pallas_call(
        _merge_kernel,
        grid=(m // tm,),
        in_specs=[row(d), row(WIDTH_A), row(WIDTH_B)] + [row(PAIR)] * len(attn)
                 + [pl.BlockSpec((tm, 3 * d), lambda i: (i, gate_block)),
                    full(p["proj_a"]), full(p["proj_b"]), full(p["proj_c"]), full(p["w_out"])],
        out_specs=row(d),
        out_shape=jax.ShapeDtypeStruct((m, d), F32),
        compiler_params=_cparams(("parallel",)),
        name="merge",
    )(x2d, ya, yb, *attn, z2d, p["proj_a"], p["proj_b"], p["proj_c"], p["w_out"])


def _mlp_kernel(x_ref, g_ref, wu_ref, wd_ref, o_ref, h_ref, acc_ref):
    j = pl.program_id(1)

    @pl.when(j == 0)
    def _():
        x = x_ref[...]
        ms = jnp.mean(x * x, axis=-1, keepdims=True)
        h_ref[...] = (x * lax.rsqrt(ms + RMS_EPS) * g_ref[...]).astype(BF16)
        acc_ref[...] = jnp.zeros_like(acc_ref)

    u = jnp.maximum(_dot(h_ref[...], wu_ref[...]), 0.0)
    acc_ref[...] += _dot((u * u).astype(BF16), wd_ref[...])

    @pl.when(j == pl.num_programs(1) - 1)
    def _():
        o_ref[...] = x_ref[...] + acc_ref[...]


def _mlp(x2d, g, wu, wd, tm, tf):
    m, d = x2d.shape
    f = wu.shape[1]
    return pl.pallas_call(
        _mlp_kernel,
        grid=(m // tm, f // tf),
        in_specs=[pl.BlockSpec((tm, d), lambda i, j: (i, 0)),
                  pl.BlockSpec((1, d), lambda i, j: (0, 0)),
                  pl.BlockSpec((d, tf), lambda i, j: (0, j)),
                  pl.BlockSpec((tf, d), lambda i, j: (j, 0))],
        out_specs=pl.BlockSpec((tm, d), lambda i, j: (i, 0)),
        out_shape=jax.ShapeDtypeStruct((m, d), F32),
        scratch_shapes=[pltpu.VMEM((tm, d), BF16), pltpu.VMEM((tm, d), F32)],
        compiler_params=_cparams(("parallel", "arbitrary")),
        name="mlp",
    )(x2d, g, wu, wd)


def _pad_rows(w, lo, total):
    return jnp.pad(w, ((lo, total - lo - w.shape[0]), (0, 0)))


def _block_diag_halves(w):
    n, bd, _ = w.shape
    per = n // 2
    out = jnp.zeros((2, per * bd, per * bd), w.dtype)
    for i in range(n):
        j, q = divmod(i, per)
        out = out.at[j, q * bd:(q + 1) * bd, q * bd:(q + 1) * bd].set(w[i])
    return out.astype(BF16)


def _layer(x, l, bias_tiles, proj, prm):
    (norm_mix_g, _, rwkv_mu, rwkv_w0, rwkv_w_up, rwkv_a0, rwkv_a_up, rwkv_g_up, rwkv_k_k, rwkv_k_a,
     rwkv_r_k, rwkv_ln_g, rwkv_ln_b, proj_a, conv_w, conv_b, lru_wa, lru_ba, lru_wx, lru_bx, lru_lambda,
     proj_b, q_norm_g, k_norm_g, proj_c, w_out, norm_mlp_g, mlp_up, mlp_down) = [t[l] for t in prm]
    b, s, d = x.shape
    x2d = x.reshape(b * s, d)
    row = lambda t: t.reshape(1, -1).astype(F32)

    wz_all, z_b, z_g, z_c = proj
    c_rkv, c_lora = 3 * WIDTH_A, LORA_W + LORA_A + LORA_G
    pb = dict(conv_w=conv_w.astype(F32), conv_b=row(conv_b), wa=_block_diag_halves(lru_wa), ba=row(lru_ba),
              wx=_block_diag_halves(lru_wx), bx=row(lru_bx), lam=row(lru_lambda))
    z2d, zc2d, yb = _projection(x2d, row(norm_mix_g), wz_all, l, pb, z_b, z_c, s, 2048, 768)
    z3 = z2d.reshape(b, s, z_c)
    zc3 = zc2d.reshape(b, s, 3 * WIDTH_C)

    ones_a = _head_ones(WIDTH_A)
    pa = dict(
        mu_r=row(rwkv_mu[0:c_rkv]),
        mu_l=jnp.pad(row(rwkv_mu[c_rkv:]), ((0, 0), (0, LORA_PAD - c_lora))),
        w0=row(rwkv_w0), a0=row(rwkv_a0),
        w_up=_pad_rows(rwkv_w_up, 0, LORA_W + LORA_A).astype(BF16),
        a_up=_pad_rows(rwkv_a_up, LORA_W, LORA_W + LORA_A).astype(BF16),
        g_up=_pad_rows(rwkv_g_up, 0, LORA_PAD - LORA_W - LORA_A).astype(BF16),
        k_k=row(rwkv_k_k), k_a=row(rwkv_k_a), r_k=row(rwkv_r_k),
        ln_g=row(rwkv_ln_g), ln_b=row(rwkv_ln_b), ones=ones_a)
    ya = _rwkv_mixer(z3, pa)

    qg = jnp.tile(row(q_norm_g), (1, HEADS_PER_GROUP))
    kg = jnp.tile(row(k_norm_g), (1, HEADS_PER_GROUP))
    ones_c = _head_ones(GROUP_W)
    attn = []
    for gi, (_, dil) in enumerate(GROUPS):
        parts = _attn_group(zc3, 0, bias_tiles[gi], qg, kg, ones_c, gi, dil)
        attn += [t.reshape(b * s, PAIR) for t in parts]

    pm = dict(proj_a=proj_a.astype(BF16), proj_b=proj_b.astype(BF16), proj_c=proj_c.astype(BF16),
              w_out=w_out.astype(BF16))
    x1 = _merge(x2d, ya.reshape(b * s, WIDTH_A), yb, attn, z2d, z_g // (3 * d), pm, 512)
    x2 = _mlp(x1, row(norm_mlp_g), mlp_up.astype(BF16), mlp_down.astype(BF16), 1024, 1024)
    return x2.reshape(b, s, d)


def kernel(x, rel_bias, norm_mix_g, w_in, rwkv_mu, rwkv_w0, rwkv_w_up, rwkv_a0, rwkv_a_up, rwkv_g_up, rwkv_k_k, rwkv_k_a, rwkv_r_k, rwkv_ln_g, rwkv_ln_b, proj_a, conv_w, conv_b, lru_wa, lru_ba, lru_wx, lru_bx, lru_lambda, proj_b, q_norm_g, k_norm_g, proj_c, w_out, norm_mlp_g, mlp_up, mlp_down):
    prm = (norm_mix_g, w_in, rwkv_mu, rwkv_w0, rwkv_w_up, rwkv_a0, rwkv_a_up, rwkv_g_up, rwkv_k_k, rwkv_k_a,
           rwkv_r_k, rwkv_ln_g, rwkv_ln_b, proj_a, conv_w, conv_b, lru_wa, lru_ba, lru_wx, lru_bx, lru_lambda,
           proj_b, q_norm_g, k_norm_g, proj_c, w_out, norm_mlp_g, mlp_up, mlp_down)
    bias_tiles = _attn_bias_tiles(rel_bias)
    proj = _projection_weights(w_in.astype(F32), x.shape[-1])
    x = x.astype(F32)
    for l in range(norm_mix_g.shape[0]):
        x = _layer(x, l, bias_tiles, proj, prm)
    return x
```

```python
import functools
import math

import jax
import jax.numpy as jnp
from jax import lax
from jax.experimental import pallas as pl
from jax.experimental.pallas import tpu as pltpu

F32 = jnp.float32
BF16 = jnp.bfloat16

N_HEADS_A = 8
HEAD = 64
PAIR = 2 * HEAD
WIDTH_A = N_HEADS_A * HEAD
N_PAIRS = WIDTH_A // PAIR
CHUNK = 64
LOCAL_WAVE = 4
PREV_ROWS = 16
LRU_SLAB = 256
MXU_N = 256
LORA_W, LORA_A, LORA_G = 64, 64, 160
LORA_PAD = 384
GN_EPS = 64e-5
WIDTH_B = 512
LRU_BLOCK = 64
CONV_TAPS = 4
LRU_C = 8.0
GROUPS = ((128, 1), (512, 4), (2048, 16))
HEADS_PER_GROUP = 4
GROUP_W = HEADS_PER_GROUP * HEAD
WIDTH_C = len(GROUPS) * GROUP_W
QBLK = 128
ATTN_TILE = 2048
ATTN_WAVE = 4
N_BUCKETS = 32
MAX_DISTANCE = 2048
NEG_INF = -1e30
RMS_EPS = 1e-6
VMEM_LIMIT = 56 * 1024 * 1024


def _cparams(sem):
    return pltpu.CompilerParams(dimension_semantics=sem, vmem_limit_bytes=VMEM_LIMIT)


def _dot(a, b):
    return jnp.dot(a, b, preferred_element_type=F32)


def _dot_nt(a, b):
    return lax.dot_general(a, b, (((1,), (1,)), ((), ())), preferred_element_type=F32)


_NN = (((1,), (0,)), ((), ()))
_NT = (((1,), (1,)), ((), ()))
_TN = (((0,), (0,)), ((), ()))


def _mm(a, b, dims=_NN):
    return lax.dot_general(a.astype(BF16), b.astype(BF16), dims, preferred_element_type=F32)


def _sigmoid(x):
    return 1.0 / (1.0 + jnp.exp(-x))


def _softplus(x):
    return jnp.maximum(x, 0.0) + jnp.log1p(jnp.exp(-jnp.abs(x)))


def _head_ones(width):
    i = jnp.arange(width) // HEAD
    return (i[:, None] == i[None, :]).astype(BF16)


def _projection_kernel(x_ref, g_ref, w_ref, *rest, n_lo, copies, first_slab, tiles_per_seq):
    lru, (lo_ref, hi_ref, yb_ref, h_ref, zb_ref, hist_ref, carry_ref) = rest[:7], rest[7:]
    i, j = pl.program_id(0), pl.program_id(1)

    @pl.when(j == 0)
    def _():
        x = x_ref[...]
        ms = jnp.mean(x * x, axis=-1, keepdims=True)
        h_ref[...] = (x * lax.rsqrt(ms + RMS_EPS) * g_ref[...]).astype(BF16)

    s = j - first_slab
    rows = pl.ds(pl.multiple_of(s * LRU_SLAB, LRU_SLAB), LRU_SLAB)
    seq_start = jnp.logical_and(s == 0, lax.rem(i, tiles_per_seq) == 0)

    def slab_load():
        return (zb_ref[rows, 0:WIDTH_B].astype(F32), zb_ref[rows, WIDTH_B:2 * WIDTH_B].astype(F32),
                jnp.where(seq_start, 0.0, hist_ref[...]), jnp.where(seq_start, 0.0, carry_ref[0:1, :]))

    def slab_store(x, out, h_last):
        yb_ref[rows, :] = out.astype(yb_ref.dtype)
        hist_ref[...] = x[LRU_SLAB - PREV_ROWS:, :]
        carry_ref[0:1, :] = h_last

    def matmul(out_ref, copy):
        h = h_ref[...]
        for c0 in range(0, out_ref.shape[1], MXU_N):
            z = _dot_nt(h, w_ref[c0:c0 + MXU_N, :])
            out_ref[:, c0:c0 + MXU_N] = z.astype(out_ref.dtype)
            if copy is not None:
                src, dst, n = copy
                lo_c, hi_c = max(src, c0), min(src + n, c0 + MXU_N)
                if lo_c < hi_c:
                    zb_ref[:, dst + lo_c - src:dst + hi_c - src] = z[:, lo_c - c0:hi_c - c0].astype(zb_ref.dtype)
            yield

    def step(out_ref, copy, with_slab):
        if not with_slab:
            _interleave(matmul(out_ref, copy))
        elif copy is None:
            x, yb, prev, h_prev = slab_load()
            _interleave(matmul(out_ref, None), _lru_slab(x, yb, prev, seq_start, h_prev, lru, slab_store))
        else:
            _interleave(matmul(out_ref, copy))
            x, yb, prev, h_prev = slab_load()
            _interleave(_lru_slab(x, yb, prev, seq_start, h_prev, lru, slab_store))

    plain = j < first_slab
    for jb in copies:
        plain = jnp.logical_and(plain, j != jb)
        pl.when(j == jb)(functools.partial(step, lo_ref, copies[jb], jb == first_slab))
    pl.when(plain)(functools.partial(step, lo_ref, None, False))
    pl.when(jnp.logical_and(j > first_slab, j < n_lo))(functools.partial(step, lo_ref, None, True))
    pl.when(j >= n_lo)(functools.partial(step, hi_ref, None, True))


def _projection(x2d, g, w_all, layer, lru, z_b, n_lo_cols, seq_len, tm, tn):
    m, d = x2d.shape
    n = w_all.shape[1]
    n_lo, n_blocks = n_lo_cols // tn, n // tn
    copies = {}
    for jb in range(n_blocks):
        lo_c, hi_c = max(jb * tn, z_b), min((jb + 1) * tn, z_b + 2 * WIDTH_B)
        if lo_c < hi_c:
            copies[jb] = (lo_c - jb * tn, lo_c - z_b, hi_c - lo_c)
    first_slab = max(copies)
    assert first_slab < n_lo and n_blocks - first_slab == tm // LRU_SLAB and seq_len % tm == 0
    const2 = lambda i, j: (0, 0)
    vec = pl.BlockSpec((1, WIDTH_B), const2)
    half = WIDTH_B // 2
    mat = pl.BlockSpec((2, half, half), lambda i, j: (0, 0, 0))
    return pl.pallas_call(
        functools.partial(_projection_kernel, n_lo=n_lo, copies=copies, first_slab=first_slab,
                          tiles_per_seq=seq_len // tm),
        grid=(m // tm, n_blocks),
        in_specs=[pl.BlockSpec((tm, d), lambda i, j: (i, 0)),
                  pl.BlockSpec((1, d), const2),
                  pl.BlockSpec((None, tn, d), lambda i, j: (layer, j, 0)),
                  pl.BlockSpec((CONV_TAPS, WIDTH_B), const2), vec, mat, vec, mat, vec, vec],
        out_specs=[pl.BlockSpec((tm, tn), lambda i, j: (i, jnp.minimum(j, n_lo - 1))),
                   pl.BlockSpec((tm, tn), lambda i, j: (i, jnp.maximum(j - n_lo, 0))),
                   pl.BlockSpec((tm, WIDTH_B), lambda i, j: (i, 0))],
        out_shape=[jax.ShapeDtypeStruct((m, n_lo_cols), BF16),
                   jax.ShapeDtypeStruct((m, n - n_lo_cols), F32),
                   jax.ShapeDtypeStruct((m, WIDTH_B), BF16)],
        scratch_shapes=[pltpu.VMEM((tm, d), BF16), pltpu.VMEM((tm, 2 * WIDTH_B), BF16),
                        pltpu.VMEM((PREV_ROWS, WIDTH_B), F32), pltpu.VMEM((8, WIDTH_B), F32)],
        compiler_params=_cparams(("arbitrary", "arbitrary")),
        name="projection",
    )(x2d, g, w_all, lru["conv_w"], lru["conv_b"], lru["wa"], lru["ba"], lru["wx"], lru["bx"], lru["lam"])


def _relayout_kernel(w_ref, o_ref, *, moves, width):
    cols = w_ref.shape[2]
    end = 0
    for src, dst, n in moves:
        if dst > end:
            o_ref[0, end:dst, :] = jnp.zeros((dst - end, cols), o_ref.dtype)
        o_ref[0, dst:dst + n, :] = w_ref[0, src:src + n, :].astype(o_ref.dtype)
        end = dst + n
    if end < width:
        o_ref[0, end:width, :] = jnp.zeros((width - end, cols), o_ref.dtype)


def _projection_weights(w_t, tc=256):
    n_layers, n_in, d = w_t.shape
    c_rkv, c_lora = 3 * WIDTH_A, LORA_W + LORA_A + LORA_G
    o_b = c_rkv + c_lora
    o_c = o_b + 2 * WIDTH_B
    o_g = o_c + 3 * WIDTH_C
    z_b = -(-(c_rkv + LORA_PAD) // WIDTH_B) * WIDTH_B
    z_g = -(-(z_b + 2 * WIDTH_B) // (3 * d)) * (3 * d)
    z_c = z_g + 3 * d
    width = z_c + 3 * WIDTH_C
    moves = ((0, 0, o_b), (o_b, z_b, o_c - o_b), (o_g, z_g, n_in - o_g), (o_c, z_c, o_g - o_c))
    wz = pl.pallas_call(
        functools.partial(_relayout_kernel, moves=moves, width=width),
        grid=(n_layers, d // tc),
        in_specs=[pl.BlockSpec((1, n_in, tc), lambda l, i: (l, 0, i))],
        out_specs=pl.BlockSpec((1, width, tc), lambda l, i: (l, 0, i)),
        out_shape=jax.ShapeDtypeStruct((n_layers, width, d), BF16),
        compiler_params=_cparams(("parallel", "parallel")),
        name="projection_weights",
    )(w_t)
    return wz, z_b, z_g, z_c


def _shift_rows(cur, prev):
    prev_row = prev[PREV_ROWS - 1:PREV_ROWS, :]
    rolled = pltpu.roll(cur, 1, axis=0)
    row = lax.broadcasted_iota(jnp.int32, cur.shape, 0)
    return jnp.where(row == 0, prev_row, rolled)


def _interleave(*stages):
    live = list(stages)
    while live:
        for gen in list(live):
            try:
                next(gen)
            except StopIteration:
                live.remove(gen)


def _rwkv_prep(zr, zl, prev_r, prev_l, rows, params, scan, put_gate, put_bonus):
    mur_ref, mul_ref, w0_ref, wup_ref, a0_ref, aup_ref, gup_ref, kk_ref, ka_ref, rk_ref, ones_ref = params
    r_out, k_out, v_out, lg_out, lgp_out, as_out, bs_out = scan
    fr = zr + (_shift_rows(zr, prev_r) - zr) * mur_ref[...]
    fl = zl + (_shift_rows(zl, prev_l) - zl) * mul_ref[...]
    r = fr[:, 0:WIDTH_A]
    k = fr[:, WIDTH_A:2 * WIDTH_A]
    v = fr[:, 2 * WIDTH_A:3 * WIDTH_A]
    x_wa = fl[:, 0:LORA_W + LORA_A]
    x_g = fl[:, LORA_W + LORA_A:LORA_PAD]
    ones = ones_ref[...]
    r_out[0, rows, :] = r
    v_out[0, rows, :] = v
    yield

    w = -_softplus(-(w0_ref[...] + _dot(jnp.tanh(x_wa).astype(BF16), wup_ref[...]))) - 0.5
    lw = -jnp.exp(w)
    pos = lax.broadcasted_iota(jnp.int32, lw.shape, 0) & (CHUNK - 1)
    lg = lw
    for sh in [1 << i for i in range(int(math.log2(CHUNK)))]:
        lg = lg + jnp.where(pos >= sh, pltpu.roll(lg, sh, axis=0), 0.0)
    lg_out[0, rows, :] = lg
    lgp_out[0, rows, :] = lg - lw
    yield

    a = _sigmoid(a0_ref[...] + _dot(x_wa.astype(BF16), aup_ref[...]))
    k2 = k * (1.0 + (a - 1.0) * ka_ref[...])
    k_out[0, rows, :] = k2
    yield

    kk = k * kk_ref[...]
    kk = kk / jnp.maximum(jnp.sqrt(_dot((kk * kk).astype(BF16), ones)), 1e-12)
    as_out[0, rows, :] = -kk
    bs_out[0, rows, :] = kk * a
    yield

    put_gate(_dot(_sigmoid(x_g).astype(BF16), gup_ref[...]))
    yield

    put_bonus(_dot((r * k2 * rk_ref[...]).astype(BF16), ones) * v)


def _stack_heads(x):
    lo = lax.broadcasted_iota(jnp.int32, x.shape, 1) < HEAD
    return jnp.concatenate([jnp.where(lo, x, 0.0), jnp.where(lo, 0.0, x)], axis=0)


def _unstack_heads(x):
    return x[0:CHUNK, :] + x[CHUNK:2 * CHUNK, :]


def _rwkv_local_kernel(*refs, n_chunks):
    (zr_ref, zl_ref, zr_next, zl_next), params = refs[:4], refs[4:15]
    t_out, g_out, rh_out, yh_out, gate_out, bonus_out = refs[15:21]
    scan, (gate_carry, bonus_carry) = refs[21:28], refs[28:]
    r_ref, k_ref, v_ref, lg_ref, lgp_ref, as_ref, bs_ref = scan
    two_c = 2 * CHUNK
    row = lax.broadcasted_iota(jnp.int32, (two_c, two_c), 0)
    col = lax.broadcasted_iota(jnp.int32, (two_c, two_c), 1)
    strict = col < row
    incl = col <= row
    eye = (col == row).astype(F32)
    wave_rows = LOCAL_WAVE * CHUNK
    n_waves = n_chunks // LOCAL_WAVE
    first_rows = slice(0, wave_rows)
    f32 = lambda ref, rows: ref[0, rows, :].astype(F32)

    def put(ref, rows):
        def store(val):
            ref[rows] = val
        return store

    def prep(w):
        rows = slice(w * wave_rows, (w + 1) * wave_rows)
        before = slice(w * wave_rows - PREV_ROWS, w * wave_rows)
        return _rwkv_prep(f32(zr_ref, rows), f32(zl_ref, rows), f32(zr_ref, before), f32(zl_ref, before), rows,
                          params, scan, put(gate_out, (0, rows)), put(bonus_out, (0, rows)))

    def prep_first(zr_src, zl_src, prev_r, prev_l):
        return _rwkv_prep(f32(zr_src, first_rows), f32(zl_src, first_rows), prev_r, prev_l, first_rows,
                          params, scan, put(gate_carry, slice(None)), put(bonus_carry, slice(None)))

    @pl.when(pl.program_id(1) == 0)
    def _():
        zero = lambda ref: jnp.zeros((PREV_ROWS, ref.shape[-1]), F32)
        _interleave(prep_first(zr_ref, zl_ref, zero(zr_ref), zero(zl_ref)))

    gate_out[0, first_rows, :] = gate_carry[...]
    bonus_out[0, first_rows, :] = bonus_carry[...]

    def wave(w):
        c0 = w * LOCAL_WAVE
        units = [(c, p) for c in range(c0, min(c0 + LOCAL_WAVE, n_chunks)) for p in range(N_PAIRS)]
        at = lambda ref: [ref[0, c * CHUNK:(c + 1) * CHUNK, p * PAIR:(p + 1) * PAIR] for c, p in units]
        each = lambda f, *ls: [f(*xs) for xs in zip(*ls)]
        lg = at(lg_ref)
        lg_end = each(lambda x: x[CHUNK - 1:CHUNK, :], lg)
        e_neg = each(lambda x: jnp.exp(-x), lg)
        e_end = each(lambda x, xe: jnp.exp(xe - x), lg, lg_end)
        a_s, b_s, kk = at(as_ref), at(bs_ref), at(k_ref)
        a_t = each(lambda x, gp: _stack_heads(x * jnp.exp(gp)), a_s, at(lgp_ref))
        r_t = each(lambda x, g: _stack_heads(x * jnp.exp(g)), at(r_ref), lg)
        b_t = each(lambda x, e: _stack_heads(x * e), b_s, e_neg)
        k_t = each(lambda x, e: _stack_heads(x * e), kk, e_neg)
        b_p = each(lambda x, e: _stack_heads(x * e), b_s, e_end)
        k_p = each(lambda x, e: _stack_heads(x * e), kk, e_end)
        v_s = each(_stack_heads, at(v_ref))
        yield

        vcat = lambda x, y: jnp.concatenate([x, y], axis=0)
        hcat = lambda x, y: jnp.concatenate([x, y], axis=1)
        top = lambda x: x[0:two_c]
        bot = lambda x: x[two_c:2 * two_c]
        left = lambda x: x[:, 0:two_c]
        right = lambda x: x[:, two_c:2 * two_c]

        prod = each(lambda a, r, b, k: _mm(vcat(a, r), vcat(b, k), _NT), a_t, r_t, b_t, k_t)
        l_ab = each(lambda x: jnp.where(strict, left(top(x)), 0.0), prod)
        l_ak = each(lambda x: jnp.where(strict, right(top(x)), 0.0), prod)
        l_rb = each(lambda x: jnp.where(incl, left(bot(x)), 0.0), prod)
        l_rk = each(lambda x: jnp.where(incl, right(bot(x)), 0.0), prod)
        yield

        inv = each(lambda x: eye + x, l_ab)
        pw = each(lambda x: _mm(x, x), l_ab)
        yield
        for _ in range(int(math.log2(CHUNK)) - 2):
            both = each(lambda p, x: _mm(vcat(p, x), p), pw, inv)
            pw = each(top, both)
            inv = each(lambda x, y: x + bot(y), inv, both)
            yield
        inv = each(lambda x, p: x + _mm(x, p), inv, pw)
        yield

        lv_rkv = each(lambda l1, l2, v: _mm(vcat(l1, l2), v), l_ak, l_rk, v_s)
        yield
        aw_h = each(lambda m, a, x: _mm(m, hcat(a, top(x))), inv, a_t, lv_rkv)
        yield
        l_aw = each(_mm, l_rb, aw_h)
        r_h = each(lambda x, y: x + left(y), r_t, l_aw)
        y_h = each(lambda y, x: right(y) + bot(x), l_aw, lv_rkv)
        yield
        tg = each(lambda x, bp: _mm(x, bp, _TN), aw_h, b_p)
        t_m = each(lambda ge, x: eye * jnp.exp(ge) + top(x), lg_end, tg)
        yield
        g_m = each(lambda x, v, kp: bot(x) + _mm(v, kp, _TN), tg, v_s, k_p)
        for i, (c, p) in enumerate(units):
            rows = slice(c * CHUNK, (c + 1) * CHUNK)
            lanes = slice(p * PAIR, (p + 1) * PAIR)
            t_out[0, c, p] = t_m[i].astype(t_out.dtype)
            g_out[0, c, p] = g_m[i]
            rh_out[0, rows, lanes] = _unstack_heads(r_h[i]).astype(rh_out.dtype)
            yh_out[0, rows, lanes] = _unstack_heads(y_h[i])

    last = slice(n_waves * wave_rows - PREV_ROWS, n_waves * wave_rows)
    for w in range(n_waves):
        nxt = prep(w + 1) if w + 1 < n_waves else prep_first(zr_next, zl_next, f32(zr_ref, last), f32(zl_ref, last))
        _interleave(wave(w), nxt)


def _rwkv_local(z3, p, ts):
    b, s, _ = z3.shape
    n_chunks = ts // CHUNK
    wr, wl = 3 * WIDTH_A, LORA_PAD
    lora_block = wr // wl
    wave_rows = LOCAL_WAVE * CHUNK
    n_waves = ts // wave_rows
    assert n_waves >= 2 and ts % wave_rows == 0
    nxt = lambda c: (lambda bi, i: (bi, jnp.minimum((i + 1) * n_waves, s // wave_rows - 1), c))
    cur = lambda bi, i: (bi, i, 0)
    mat = lambda bi, i: (bi, i, 0, 0, 0)
    const = lambda bi, i: (0, 0)
    vec = pl.BlockSpec((1, WIDTH_A), const)
    seq_spec = pl.BlockSpec((1, ts, WIDTH_A), cur)
    mat_spec = pl.BlockSpec((1, n_chunks, N_PAIRS, PAIR, PAIR), mat)
    seq_shape = lambda dt: jax.ShapeDtypeStruct((b, s, WIDTH_A), dt)
    mat_shape = lambda dt: jax.ShapeDtypeStruct((b, s // CHUNK, N_PAIRS, PAIR, PAIR), dt)
    return pl.pallas_call(
        functools.partial(_rwkv_local_kernel, n_chunks=n_chunks),
        grid=(b, s // ts),
        in_specs=[pl.BlockSpec((1, ts, wr), cur),
                  pl.BlockSpec((1, ts, wl), lambda bi, i: (bi, i, lora_block)),
                  pl.BlockSpec((1, wave_rows, wr), nxt(0)),
                  pl.BlockSpec((1, wave_rows, wl), nxt(lora_block)),
                  pl.BlockSpec((1, wr), const), pl.BlockSpec((1, wl), const),
                  vec, pl.BlockSpec((LORA_W + LORA_A, WIDTH_A), const),
                  vec, pl.BlockSpec((LORA_W + LORA_A, WIDTH_A), const),
                  pl.BlockSpec((LORA_PAD - LORA_W - LORA_A, WIDTH_A), const),
                  vec, vec, vec, pl.BlockSpec((WIDTH_A, WIDTH_A), const)],
        out_specs=[mat_spec, mat_spec, seq_spec, seq_spec, seq_spec, seq_spec],
        out_shape=[mat_shape(BF16), mat_shape(F32), seq_shape(BF16), seq_shape(F32), seq_shape(F32), seq_shape(F32)],
        scratch_shapes=[pltpu.VMEM((1, ts, WIDTH_A), F32)] * 7 + [pltpu.VMEM((wave_rows, WIDTH_A), F32)] * 2,
        compiler_params=_cparams(("parallel", "arbitrary")),
        name="rwkv_local",
    )(z3, z3, z3, z3, p["mu_r"], p["mu_l"], p["w0"], p["w_up"], p["a0"], p["a_up"], p["g_up"],
      p["k_k"], p["k_a"], p["r_k"], p["ones"])


def _rwkv_state_kernel(t_ref, gm_ref, rh_ref, yh_ref, g_ref, bonus_ref, lng_ref, lnb_ref, ones_ref,
                       y_out, s_ref, *, n_chunks):
    @pl.when(pl.program_id(1) == 0)
    def _():
        s_ref[...] = jnp.zeros_like(s_ref)

    ones = ones_ref[...]
    state = [s_ref[p] for p in range(N_PAIRS)]
    entry = []
    for c in range(n_chunks):
        entry.append(list(state))
        state = [_mm(state[p], t_ref[0, c, p]) + gm_ref[0, c, p] for p in range(N_PAIRS)]
    for p in range(N_PAIRS):
        s_ref[p] = state[p]
    chunks = range(n_chunks)
    rows = [slice(c * CHUNK, (c + 1) * CHUNK) for c in chunks]
    y = [jnp.concatenate([_mm(rh_ref[0, rows[c], p * PAIR:(p + 1) * PAIR], entry[c][p], _NT)
                          for p in range(N_PAIRS)], axis=1) + yh_ref[0, rows[c], :] for c in chunks]
    mean = [_dot(v.astype(BF16), ones) * (1.0 / HEAD) for v in y]
    yc = [v - m for v, m in zip(y, mean)]
    var = [_dot((v * v).astype(BF16), ones) * (1.0 / HEAD) for v in yc]
    for c in chunks:
        yn = yc[c] * lax.rsqrt(var[c] + GN_EPS) * lng_ref[...] + lnb_ref[...]
        y_out[0, rows[c], :] = ((yn + bonus_ref[0, rows[c], :]) * g_ref[0, rows[c], :]).astype(y_out.dtype)


def _rwkv_state(t, gm, rh, yh, g, bonus, p, ts):
    b, s, _ = rh.shape
    n_chunks = ts // CHUNK
    cur = lambda bi, i: (bi, i, 0)
    mat = lambda bi, i: (bi, i, 0, 0, 0)
    const = lambda bi, i: (0, 0)
    seq_spec = pl.BlockSpec((1, ts, WIDTH_A), cur)
    mat_spec = pl.BlockSpec((1, n_chunks, N_PAIRS, PAIR, PAIR), mat)
    vec = pl.BlockSpec((1, WIDTH_A), const)
    return pl.pallas_call(
        functools.partial(_rwkv_state_kernel, n_chunks=n_chunks),
        grid=(b, s // ts),
        in_specs=[mat_spec, mat_spec, seq_spec, seq_spec, seq_spec, seq_spec, vec, vec,
                  pl.BlockSpec((WIDTH_A, WIDTH_A), const)],
        out_specs=seq_spec,
        out_shape=jax.ShapeDtypeStruct((b, s, WIDTH_A), BF16),
        scratch_shapes=[pltpu.VMEM((N_PAIRS, PAIR, PAIR), F32)],
        compiler_params=_cparams(("parallel", "arbitrary")),
        name="rwkv_state",
    )(t, gm, rh, yh, g, bonus, p["ln_g"], p["ln_b"], p["ones"])


def _rwkv_mixer(z3, p):
    t, gm, rh, yh, g, bonus = _rwkv_local(z3, p, ts=512)
    return _rwkv_state(t, gm, rh, yh, g, bonus, p, ts=512)


def _lru_slab(x, yb, prev, seq_start, h_prev, params, emit):
    cw_ref, cb_ref, wa_ref, ba_ref, wx_ref, bx_ref, lam_ref = params
    n_rows = x.shape[0]
    ext = jnp.concatenate([prev, x], axis=0)
    xc = x * cw_ref[CONV_TAPS - 1:CONV_TAPS, :] + cb_ref[...]
    for back in range(1, CONV_TAPS):
        tap = CONV_TAPS - 1 - back
        xc = xc + pltpu.roll(ext, back, axis=0)[PREV_ROWS:, :] * cw_ref[tap:tap + 1, :]

    half = WIDTH_B // 2
    xcb = xc.astype(BF16)
    ga = jnp.concatenate([_dot(xcb[:, j * half:(j + 1) * half], wa_ref[j]) for j in range(2)], axis=1)
    gx = jnp.concatenate([_dot(xcb[:, j * half:(j + 1) * half], wx_ref[j]) for j in range(2)], axis=1)
    gate_a = _sigmoid(ga + ba_ref[...])
    gate_x = _sigmoid(gx + bx_ref[...])
    log_a = -LRU_C * gate_a * _softplus(-lam_ref[...])
    a = jnp.exp(log_a)
    mult = jnp.sqrt(jnp.maximum(-jnp.tanh(log_a) * (1.0 + a * a), 0.0))
    xg = xc * gate_x
    b = xg * mult
    yield

    row8 = lax.broadcasted_iota(jnp.int32, (8, WIDTH_B), 0)
    h, hs = h_prev, []
    for g in range(n_rows // 8):
        a8, b8 = a[8 * g:8 * g + 8, :], b[8 * g:8 * g + 8, :]
        if g == 0:
            b8 = jnp.where(jnp.logical_and(row8 == 0, seq_start), xg[0:8, :], b8)
        for sh in (1, 2, 4):
            ar = pltpu.roll(a8, sh, axis=0)
            br = pltpu.roll(b8, sh, axis=0)
            m = row8 >= sh
            b8 = jnp.where(m, a8 * br + b8, b8)
            a8 = jnp.where(m, a8 * ar, a8)
        h8 = a8 * h + b8
        hs.append(h8)
        h = h8[7:8, :]
    yield
    gelu = 0.5 * yb * (1.0 + jnp.tanh(math.sqrt(2.0 / math.pi) * (yb + 0.044715 * (yb * yb * yb))))
    emit(x, jnp.concatenate(hs, axis=0) * gelu, h)


def _attn_kernel(q0, q1, k0, k1, kp0, kp1, v0, v1, vp0, vp1, bias_ref, qg_ref, kg_ref, ones_ref,
                 o0, o1, l0, l1, *, dil, n_sub):
    j = pl.program_id(1)
    ones = ones_ref[...]
    lane = lax.broadcasted_iota(jnp.int32, (QBLK, GROUP_W), 1)
    in_head = [(lane >= h * HEAD) & (lane < (h + 1) * HEAD) for h in range(HEADS_PER_GROUP)]
    prev_valid = (lax.broadcasted_iota(jnp.int32, (HEADS_PER_GROUP * QBLK, 2 * QBLK), 1) >= QBLK) | (j > 0)

    def rows(start):
        return pl.ds(start, QBLK, stride=dil) if dil > 1 else pl.ds(start, QBLK)

    def take(lo, hi, start):
        return jnp.concatenate([lo[0, rows(start), :], hi[0, rows(start), :]], axis=1)

    def head_sumsq(x):
        return _dot((x * x).astype(BF16), ones) * (1.0 / HEAD)

    def select_heads(x):
        out = jnp.zeros((QBLK, GROUP_W), F32)
        for h, m in enumerate(in_head):
            out = jnp.where(m, x[h * QBLK:(h + 1) * QBLK, :], out)
        return out

    def wave(units):
        each = lambda f, *ls: [f(*xs) for xs in zip(*ls)]
        span = dil * QBLK
        starts = [s for s, _ in units]
        before = [s - span if dil > 1 or isinstance(s, int) else pl.multiple_of(s - span, QBLK) for s in starts]
        q_raw = [take(q0, q1, s) for s in starts]
        k_raw = [jnp.concatenate([take(kp0, kp1, s) if far else take(k0, k1, p), take(k0, k1, s)], axis=0)
                 for (s, far), p in zip(units, before)]
        vv = [jnp.concatenate([take(vp0, vp1, s) if far else take(v0, v1, p), take(v0, v1, s)],
                              axis=0).astype(BF16) for (s, far), p in zip(units, before)]
        q_ms = each(head_sumsq, q_raw)
        k_ms = each(head_sumsq, k_raw)
        q = each(lambda x, ms: x * lax.rsqrt(ms + RMS_EPS) * qg_ref[...] * (HEAD ** -0.5), q_raw, q_ms)
        kk = each(lambda x, ms: (x * lax.rsqrt(ms + RMS_EPS) * kg_ref[...]).astype(BF16), k_raw, k_ms)
        qs = each(lambda x: jnp.concatenate([jnp.where(m, x, 0.0) for m in in_head], axis=0).astype(BF16), q)
        logits = each(lambda a, b: _dot_nt(a, b) + bias_ref[...], qs, kk)
        logits = [jnp.where(prev_valid, lg, NEG_INF) if far else lg for lg, (_, far) in zip(logits, units)]
        mx = each(lambda lg: jnp.max(lg, axis=-1, keepdims=True), logits)
        pr = each(lambda lg, m: jnp.exp(lg - m), logits, mx)
        den = each(lambda p: jnp.sum(p, axis=-1, keepdims=True), pr)
        pv = each(lambda p, v, dn: _dot(p.astype(BF16), v) / dn, pr, vv, den)
        out = each(select_heads, pv)
        lse = each(lambda m, dn: select_heads(jnp.broadcast_to(m + jnp.log(dn), (HEADS_PER_GROUP * QBLK, GROUP_W))),
                   mx, den)
        for s, o, l in zip(starts, out, lse):
            o0[0, rows(s), :] = o[:, 0:PAIR]
            o1[0, rows(s), :] = o[:, PAIR:GROUP_W]
            l0[0, rows(s), :] = l[:, 0:PAIR]
            l1[0, rows(s), :] = l[:, PAIR:GROUP_W]

    def loop(lo, hi, body):
        def step(i, carry):
            body(i)
            return carry
        lax.fori_loop(lo, hi, step, 0)

    span = dil * QBLK
    if n_sub == 1:
        loop(0, dil // ATTN_WAVE, lambda i: wave([(i * ATTN_WAVE + u, True) for u in range(ATTN_WAVE)]))
    elif dil > 1:
        wave([(r, True) for r in range(dil)])
        loop(1, n_sub, lambda n: wave([(r + n * span, False) for r in range(dil)]))
    else:
        wave([(u * QBLK, u == 0) for u in range(ATTN_WAVE)])
        loop(1, n_sub // ATTN_WAVE,
             lambda i: wave([(pl.multiple_of((i * ATTN_WAVE + u) * QBLK, QBLK), False) for u in range(ATTN_WAVE)]))


def _attn_group(z3, col0, bias, qg, kg, ones, gi, dil):
    b, s, _ = z3.shape
    span = dil * QBLK
    n_sub = ATTN_TILE // span
    halves = GROUP_W // PAIR
    per_part = len(GROUPS) * halves

    def cur(part, half):
        c = col0 + part * per_part + gi * halves + half
        return pl.BlockSpec((1, ATTN_TILE, PAIR), lambda bi, j: (bi, j, c))

    def prev(part, half):
        c = col0 + part * per_part + gi * halves + half
        return pl.BlockSpec((1, span, PAIR), lambda bi, j: (bi, jnp.maximum(j * n_sub - 1, 0), c))

    const2 = lambda bi, j: (0, 0)
    out_spec = pl.BlockSpec((1, ATTN_TILE, PAIR), lambda bi, j: (bi, j, 0))
    return pl.pallas_call(
        functools.partial(_attn_kernel, dil=dil, n_sub=n_sub),
        grid=(b, s // ATTN_TILE),
        in_specs=[cur(0, 0), cur(0, 1), cur(1, 0), cur(1, 1), prev(1, 0), prev(1, 1),
                  cur(2, 0), cur(2, 1), prev(2, 0), prev(2, 1),
                  pl.BlockSpec((HEADS_PER_GROUP * QBLK, 2 * QBLK), const2),
                  pl.BlockSpec((1, GROUP_W), const2), pl.BlockSpec((1, GROUP_W), const2),
                  pl.BlockSpec((GROUP_W, GROUP_W), const2)],
        out_specs=[out_spec] * 4,
        out_shape=[jax.ShapeDtypeStruct((b, s, PAIR), F32)] * 4,
        compiler_params=_cparams(("parallel", "arbitrary")),
        name=f"dilated_attn_g{gi}",
    )(*([z3] * 10), bias, qg, kg, ones)


def _t5_bucket(dist):
    max_exact = N_BUCKETS // 2
    d = jnp.maximum(dist, 0)
    large = max_exact + (jnp.log(jnp.maximum(d, 1).astype(F32) / max_exact)
                         / math.log(MAX_DISTANCE / max_exact) * (N_BUCKETS - max_exact)).astype(jnp.int32)
    large = jnp.minimum(large, N_BUCKETS - 1)
    return jnp.where(d < max_exact, d, large)


def _attn_bias_tiles(rel_bias):
    tiles = []
    kj = jnp.arange(2 * QBLK)[None, :]
    rel = (jnp.arange(QBLK)[:, None] + QBLK) - kj
    for gi, (window, dil) in enumerate(GROUPS):
        band = (rel >= 0) & (rel <= window // dil)
        tab = rel_bias.astype(F32)[:, gi * HEADS_PER_GROUP:(gi + 1) * HEADS_PER_GROUP]
        onehot = (_t5_bucket(rel * dil)[..., None] == jnp.arange(N_BUCKETS)).astype(F32)
        bias = jnp.einsum("qkn,nh->hqk", onehot, tab, precision=lax.Precision.HIGHEST)
        tiles.append(jnp.where(band[None], bias, NEG_INF).reshape(HEADS_PER_GROUP * QBLK, 2 * QBLK))
    return tiles


def _merge_kernel(x_ref, ya_ref, yb_ref, *rest):
    n_g = len(GROUPS)
    attn = rest[:4 * n_g]
    zg_ref, pa_ref, pb_ref, pc_ref, wo_ref, out_ref = rest[4 * n_g:]
    d = x_ref.shape[-1]
    outs = [jnp.concatenate([attn[4 * g][...], attn[4 * g + 1][...]], axis=1) for g in range(n_g)]
    lses = [jnp.concatenate([attn[4 * g + 2][...], attn[4 * g + 3][...]], axis=1) for g in range(n_g)]
    m = functools.reduce(jnp.maximum, lses)
    es = [jnp.exp(l - m) for l in lses]
    yc = sum(o * e for o, e in zip(outs, es)) / sum(es)
    gate = lambda n: _sigmoid(zg_ref[:, n * d:(n + 1) * d].astype(F32))
    merged = (gate(0) * _dot(ya_ref[...].astype(BF16), pa_ref[...])
              + gate(1) * _dot(yb_ref[...].astype(BF16), pb_ref[...])
              + gate(2) * _dot(yc.astype(BF16), pc_ref[...]))
    out_ref[...] = x_ref[...] + _dot(merged.astype(BF16), wo_ref[...])


def _merge(x2d, ya, yb, attn, z2d, gate_block, p, tm):
    m, d = x2d.shape
    row = lambda w: pl.BlockSpec((tm, w), lambda i: (i, 0))
    full = lambda a: pl.BlockSpec(a.shape, lambda i: (0, 0))
    return pl.pallas_call(
        _merge_kernel,
        grid=(m // tm,),
        in_specs=[row(d), row(WIDTH_A), row(WIDTH_B)] + [row(PAIR)] * len(attn)
                 + [pl.BlockSpec((tm, 3 * d), lambda i: (i, gate_block)),
                    full(p["proj_a"]), full(p["proj_b"]), full(p["proj_c"]), full(p["w_out"])],
        out_specs=row(d),
        out_shape=jax.ShapeDtypeStruct((m, d), F32),
        compiler_params=_cparams(("parallel",)),
        name="merge",
    )(x2d, ya, yb, *attn, z2d, p["proj_a"], p["proj_b"], p["proj_c"], p["w_out"])


def _mlp_kernel(x_ref, g_ref, wu_ref, wd_ref, o_ref, h_ref, acc_ref):
    j = pl.program_id(1)

    @pl.when(j == 0)
    def _():
        x = x_ref[...]
        ms = jnp.mean(x * x, axis=-1, keepdims=True)
        h_ref[...] = (x * lax.rsqrt(ms + RMS_EPS) * g_ref[...]).astype(BF16)
        acc_ref[...] = jnp.zeros_like(acc_ref)

    u = jnp.maximum(_dot(h_ref[...], wu_ref[...]), 0.0)
    acc_ref[...] += _dot((u * u).astype(BF16), wd_ref[...])

    @pl.when(j == pl.num_programs(1) - 1)
    def _():
        o_ref[...] = x_ref[...] + acc_ref[...]


def _mlp(x2d, g, wu, wd, tm, tf):
    m, d = x2d.shape
    f = wu.shape[1]
    return pl.pallas_call(
        _mlp_kernel,
        grid=(m // tm, f // tf),
        in_specs=[pl.BlockSpec((tm, d), lambda i, j: (i, 0)),
                  pl.BlockSpec((1, d), lambda i, j: (0, 0)),
                  pl.BlockSpec((d, tf), lambda i, j: (0, j)),
                  pl.BlockSpec((tf, d), lambda i, j: (j, 0))],
        out_specs=pl.BlockSpec((tm, d), lambda i, j: (i, 0)),
        out_shape=jax.ShapeDtypeStruct((m, d), F32),
        scratch_shapes=[pltpu.VMEM((tm, d), BF16), pltpu.VMEM((tm, d), F32)],
        compiler_params=_cparams(("parallel", "arbitrary")),
        name="mlp",
    )(x2d, g, wu, wd)


def _pad_rows(w, lo, total):
    return jnp.pad(w, ((lo, total - lo - w.shape[0]), (0, 0)))


def _block_diag_halves(w):
    n, bd, _ = w.shape
    per = n // 2
    out = jnp.zeros((2, per * bd, per * bd), w.dtype)
    for i in range(n):
        j, q = divmod(i, per)
        out = out.at[j, q * bd:(q + 1) * bd, q * bd:(q + 1) * bd].set(w[i])
    return out.astype(BF16)


def _layer(x, l, bias_tiles, proj, prm):
    (norm_mix_g, _, rwkv_mu, rwkv_w0, rwkv_w_up, rwkv_a0, rwkv_a_up, rwkv_g_up, rwkv_k_k, rwkv_k_a,
     rwkv_r_k, rwkv_ln_g, rwkv_ln_b, proj_a, conv_w, conv_b, lru_wa, lru_ba, lru_wx, lru_bx, lru_lambda,
     proj_b, q_norm_g, k_norm_g, proj_c, w_out, norm_mlp_g, mlp_up, mlp_down) = [t[l] for t in prm]
    b, s, d = x.shape
    x2d = x.reshape(b * s, d)
    row = lambda t: t.reshape(1, -1).astype(F32)

    wz_all, z_b, z_g, z_c = proj
    c_rkv, c_lora = 3 * WIDTH_A, LORA_W + LORA_A + LORA_G
    pb = dict(conv_w=conv_w.astype(F32), conv_b=row(conv_b), wa=_block_diag_halves(lru_wa), ba=row(lru_ba),
              wx=_block_diag_halves(lru_wx), bx=row(lru_bx), lam=row(lru_lambda))
    z2d, zc2d, yb = _projection(x2d, row(norm_mix_g), wz_all, l, pb, z_b, z_c, s, 2048, 768)
    z3 = z2d.reshape(b, s, z_c)
    zc3 = zc2d.reshape(b, s, 3 * WIDTH_C)

    ones_a = _head_ones(WIDTH_A)
    pa = dict(
        mu_r=row(rwkv_mu[0:c_rkv]),
        mu_l=jnp.pad(row(rwkv_mu[c_rkv:]), ((0, 0), (0, LORA_PAD - c_lora))),
        w0=row(rwkv_w0), a0=row(rwkv_a0),
        w_up=_pad_rows(rwkv_w_up, 0, LORA_W + LORA_A).astype(BF16),
        a_up=_pad_rows(rwkv_a_up, LORA_W, LORA_W + LORA_A).astype(BF16),
        g_up=_pad_rows(rwkv_g_up, 0, LORA_PAD - LORA_W - LORA_A).astype(BF16),
        k_k=row(rwkv_k_k), k_a=row(rwkv_k_a), r_k=row(rwkv_r_k),
        ln_g=row(rwkv_ln_g), ln_b=row(rwkv_ln_b), ones=ones_a)
    ya = _rwkv_mixer(z3, pa)

    qg = jnp.tile(row(q_norm_g), (1, HEADS_PER_GROUP))
    kg = jnp.tile(row(k_norm_g), (1, HEADS_PER_GROUP))
    ones_c = _head_ones(GROUP_W)
    attn = []
    for gi, (_, dil) in enumerate(GROUPS):
        parts = _attn_group(zc3, 0, bias_tiles[gi], qg, kg, ones_c, gi, dil)
        attn += [t.reshape(b * s, PAIR) for t in parts]

    pm = dict(proj_a=proj_a.astype(BF16), proj_b=proj_b.astype(BF16), proj_c=proj_c.astype(BF16),
              w_out=w_out.astype(BF16))
    x1 = _merge(x2d, ya.reshape(b * s, WIDTH_A), yb, attn, z2d, z_g // (3 * d), pm, 512)
    x2 = _mlp(x1, row(norm_mlp_g), mlp_up.astype(BF16), mlp_down.astype(BF16), 1024, 1024)
    return x2.reshape(b, s, d)


def kernel(x, rel_bias, norm_mix_g, w_in, rwkv_mu, rwkv_w0, rwkv_w_up, rwkv_a0, rwkv_a_up, rwkv_g_up, rwkv_k_k, rwkv_k_a, rwkv_r_k, rwkv_ln_g, rwkv_ln_b, proj_a, conv_w, conv_b, lru_wa, lru_ba, lru_wx, lru_bx, lru_lambda, proj_b, q_norm_g, k_norm_g, proj_c, w_out, norm_mlp_g, mlp_up, mlp_down):
    prm = (norm_mix_g, w_in, rwkv_mu, rwkv_w0, rwkv_w_up, rwkv_a0, rwkv_a_up, rwkv_g_up, rwkv_k_k, rwkv_k_a,
           rwkv_r_k, rwkv_ln_g, rwkv_ln_b, proj_a, conv_w, conv_b, lru_wa, lru_ba, lru_wx, lru_bx, lru_lambda,
           proj_b, q_norm_g, k_norm_g, proj_c, w_out, norm_mlp_g, mlp_up, mlp_down)
    bias_tiles = _attn_bias_tiles(rel_bias)
    proj = _projection_weights(jnp.swapaxes(w_in.astype(F32), 1, 2))
    x = x.astype(F32)
    for l in range(norm_mix_g.shape[0]):
        x = _layer(x, l, bias_tiles, proj, prm)
    return x
```

```python
import functools
import math

import jax
import jax.numpy as jnp
from jax import lax
from jax.experimental import pallas as pl
from jax.experimental.pallas import tpu as pltpu

F32 = jnp.float32
BF16 = jnp.bfloat16

N_HEADS_A = 8
HEAD = 64
PAIR = 2 * HEAD
WIDTH_A = N_HEADS_A * HEAD
N_PAIRS = WIDTH_A // PAIR
CHUNK = 64
LOCAL_WAVE = 4
PREV_ROWS = 16
LRU_SLAB = 256
MXU_N = 256
LORA_W, LORA_A, LORA_G = 64, 64, 160
LORA_PAD = 384
GN_EPS = 64e-5
WIDTH_B = 512
LRU_BLOCK = 64
CONV_TAPS = 4
LRU_C = 8.0
GROUPS = ((128, 1), (512, 4), (2048, 16))
HEADS_PER_GROUP = 4
GROUP_W = HEADS_PER_GROUP * HEAD
WIDTH_C = len(GROUPS) * GROUP_W
QBLK = 128
ATTN_TILE = 2048
ATTN_WAVE = 4
N_BUCKETS = 32
MAX_DISTANCE = 2048
NEG_INF = -1e30
RMS_EPS = 1e-6
VMEM_LIMIT = 56 * 1024 * 1024


def _cparams(sem):
    return pltpu.CompilerParams(dimension_semantics=sem, vmem_limit_bytes=VMEM_LIMIT)


def _dot(a, b):
    return jnp.dot(a, b, preferred_element_type=F32)


def _dot_nt(a, b):
    return lax.dot_general(a, b, (((1,), (1,)), ((), ())), preferred_element_type=F32)


_NN = (((1,), (0,)), ((), ()))
_NT = (((1,), (1,)), ((), ()))
_TN = (((0,), (0,)), ((), ()))


def _mm(a, b, dims=_NN):
    return lax.dot_general(a.astype(BF16), b.astype(BF16), dims, preferred_element_type=F32)


def _sigmoid(x):
    return 1.0 / (1.0 + jnp.exp(-x))


def _softplus(x):
    return jnp.maximum(x, 0.0) + jnp.log1p(jnp.exp(-jnp.abs(x)))


def _head_ones(width):
    i = jnp.arange(width) // HEAD
    return (i[:, None] == i[None, :]).astype(BF16)


def _projection_kernel(x_ref, g_ref, w_ref, *rest, n_lo, copies, first_slab, tiles_per_seq):
    lru, (lo_ref, hi_ref, yb_ref, h_ref, zb_ref, hist_ref, carry_ref) = rest[:7], rest[7:]
    i, j = pl.program_id(0), pl.program_id(1)

    @pl.when(j == 0)
    def _():
        x = x_ref[...]
        ms = jnp.mean(x * x, axis=-1, keepdims=True)
        h_ref[...] = (x * lax.rsqrt(ms + RMS_EPS) * g_ref[...]).astype(BF16)

    s = j - first_slab
    rows = pl.ds(pl.multiple_of(s * LRU_SLAB, LRU_SLAB), LRU_SLAB)
    seq_start = jnp.logical_and(s == 0, lax.rem(i, tiles_per_seq) == 0)

    def slab_load():
        return (zb_ref[rows, 0:WIDTH_B].astype(F32), zb_ref[rows, WIDTH_B:2 * WIDTH_B].astype(F32),
                jnp.where(seq_start, 0.0, hist_ref[...]), jnp.where(seq_start, 0.0, carry_ref[0:1, :]))

    def slab_store(x, out, h_last):
        yb_ref[rows, :] = out.astype(yb_ref.dtype)
        hist_ref[...] = x[LRU_SLAB - PREV_ROWS:, :]
        carry_ref[0:1, :] = h_last

    def matmul(out_ref, copy):
        h = h_ref[...]
        for c0 in range(0, out_ref.shape[1], MXU_N):
            z = _dot_nt(h, w_ref[c0:c0 + MXU_N, :])
            out_ref[:, c0:c0 + MXU_N] = z.astype(out_ref.dtype)
            if copy is not None:
                src, dst, n = copy
                lo_c, hi_c = max(src, c0), min(src + n, c0 + MXU_N)
                if lo_c < hi_c:
                    zb_ref[:, dst + lo_c - src:dst + hi_c - src] = z[:, lo_c - c0:hi_c - c0].astype(zb_ref.dtype)
            yield

    def step(out_ref, copy, with_slab):
        if not with_slab:
            _interleave(matmul(out_ref, copy))
        elif copy is None:
            x, yb, prev, h_prev = slab_load()
            _interleave(matmul(out_ref, None), _lru_slab(x, yb, prev, seq_start, h_prev, lru, slab_store))
        else:
            _interleave(matmul(out_ref, copy))
            x, yb, prev, h_prev = slab_load()
            _interleave(_lru_slab(x, yb, prev, seq_start, h_prev, lru, slab_store))

    plain = j < first_slab
    for jb in copies:
        plain = jnp.logical_and(plain, j != jb)
        pl.when(j == jb)(functools.partial(step, lo_ref, copies[jb], jb == first_slab))
    pl.when(plain)(functools.partial(step, lo_ref, None, False))
    pl.when(jnp.logical_and(j > first_slab, j < n_lo))(functools.partial(step, lo_ref, None, True))
    pl.when(j >= n_lo)(functools.partial(step, hi_ref, None, True))


def _projection(x2d, g, w_all, layer, lru, z_b, n_lo_cols, seq_len, tm, tn):
    m, d = x2d.shape
    n = w_all.shape[1]
    n_lo, n_blocks = n_lo_cols // tn, n // tn
    copies = {}
    for jb in range(n_blocks):
        lo_c, hi_c = max(jb * tn, z_b), min((jb + 1) * tn, z_b + 2 * WIDTH_B)
        if lo_c < hi_c:
            copies[jb] = (lo_c - jb * tn, lo_c - z_b, hi_c - lo_c)
    first_slab = max(copies)
    assert first_slab < n_lo and n_blocks - first_slab == tm // LRU_SLAB and seq_len % tm == 0
    const2 = lambda i, j: (0, 0)
    vec = pl.BlockSpec((1, WIDTH_B), const2)
    half = WIDTH_B // 2
    mat = pl.BlockSpec((2, half, half), lambda i, j: (0, 0, 0))
    return pl.pallas_call(
        functools.partial(_projection_kernel, n_lo=n_lo, copies=copies, first_slab=first_slab,
                          tiles_per_seq=seq_len // tm),
        grid=(m // tm, n_blocks),
        in_specs=[pl.BlockSpec((tm, d), lambda i, j: (i, 0)),
                  pl.BlockSpec((1, d), const2),
                  pl.BlockSpec((None, tn, d), lambda i, j: (layer, j, 0)),
                  pl.BlockSpec((CONV_TAPS, WIDTH_B), const2), vec, mat, vec, mat, vec, vec],
        out_specs=[pl.BlockSpec((tm, tn), lambda i, j: (i, jnp.minimum(j, n_lo - 1))),
                   pl.BlockSpec((tm, tn), lambda i, j: (i, jnp.maximum(j - n_lo, 0))),
                   pl.BlockSpec((tm, WIDTH_B), lambda i, j: (i, 0))],
        out_shape=[jax.ShapeDtypeStruct((m, n_lo_cols), BF16),
                   jax.ShapeDtypeStruct((m, n - n_lo_cols), F32),
                   jax.ShapeDtypeStruct((m, WIDTH_B), BF16)],
        scratch_shapes=[pltpu.VMEM((tm, d), BF16), pltpu.VMEM((tm, 2 * WIDTH_B), BF16),
                        pltpu.VMEM((PREV_ROWS, WIDTH_B), F32), pltpu.VMEM((8, WIDTH_B), F32)],
        compiler_params=_cparams(("arbitrary", "arbitrary")),
        name="projection",
    )(x2d, g, w_all, lru["conv_w"], lru["conv_b"], lru["wa"], lru["ba"], lru["wx"], lru["bx"], lru["lam"])


def _relayout_kernel(w_ref, o_ref, *, moves, width):
    cols = w_ref.shape[2]
    end = 0
    for src, dst, n in moves:
        if dst > end:
            o_ref[0, end:dst, :] = jnp.zeros((dst - end, cols), o_ref.dtype)
        o_ref[0, dst:dst + n, :] = w_ref[0, src:src + n, :].astype(o_ref.dtype)
        end = dst + n
    if end < width:
        o_ref[0, end:width, :] = jnp.zeros((width - end, cols), o_ref.dtype)


def _projection_weights(w_t, tc=256):
    n_layers, n_in, d = w_t.shape
    c_rkv, c_lora = 3 * WIDTH_A, LORA_W + LORA_A + LORA_G
    o_b = c_rkv + c_lora
    o_c = o_b + 2 * WIDTH_B
    o_g = o_c + 3 * WIDTH_C
    z_b = -(-(c_rkv + LORA_PAD) // WIDTH_B) * WIDTH_B
    z_g = -(-(z_b + 2 * WIDTH_B) // (3 * d)) * (3 * d)
    z_c = z_g + 3 * d
    width = z_c + 3 * WIDTH_C
    moves = ((0, 0, o_b), (o_b, z_b, o_c - o_b), (o_g, z_g, n_in - o_g), (o_c, z_c, o_g - o_c))
    wz = pl.pallas_call(
        functools.partial(_relayout_kernel, moves=moves, width=width),
        grid=(n_layers, d // tc),
        in_specs=[pl.BlockSpec((1, n_in, tc), lambda l, i: (l, 0, i))],
        out_specs=pl.BlockSpec((1, width, tc), lambda l, i: (l, 0, i)),
        out_shape=jax.ShapeDtypeStruct((n_layers, width, d), BF16),
        compiler_params=_cparams(("parallel", "parallel")),
        name="projection_weights",
    )(w_t)
    return wz, z_b, z_g, z_c


def _shift_rows(cur, prev):
    prev_row = prev[PREV_ROWS - 1:PREV_ROWS, :]
    rolled = pltpu.roll(cur, 1, axis=0)
    row = lax.broadcasted_iota(jnp.int32, cur.shape, 0)
    return jnp.where(row == 0, prev_row, rolled)


def _interleave(*stages):
    live = list(stages)
    while live:
        for gen in list(live):
            try:
                next(gen)
            except StopIteration:
                live.remove(gen)


def _rwkv_prep(zr, zl, prev_r, prev_l, rows, params, scan, put_gate, put_bonus):
    mur_ref, mul_ref, w0_ref, wup_ref, a0_ref, aup_ref, gup_ref, kk_ref, ka_ref, rk_ref, ones_ref = params
    r_out, k_out, v_out, lg_out, lgp_out, as_out, bs_out = scan
    fr = zr + (_shift_rows(zr, prev_r) - zr) * mur_ref[...]
    fl = zl + (_shift_rows(zl, prev_l) - zl) * mul_ref[...]
    r = fr[:, 0:WIDTH_A]
    k = fr[:, WIDTH_A:2 * WIDTH_A]
    v = fr[:, 2 * WIDTH_A:3 * WIDTH_A]
    x_wa = fl[:, 0:LORA_W + LORA_A]
    x_g = fl[:, LORA_W + LORA_A:LORA_PAD]
    ones = ones_ref[...]
    r_out[0, rows, :] = r
    v_out[0, rows, :] = v
    yield

    w = -_softplus(-(w0_ref[...] + _dot(jnp.tanh(x_wa).astype(BF16), wup_ref[...]))) - 0.5
    lw = -jnp.exp(w)
    pos = lax.broadcasted_iota(jnp.int32, lw.shape, 0) & (CHUNK - 1)
    lg = lw
    for sh in [1 << i for i in range(int(math.log2(CHUNK)))]:
        lg = lg + jnp.where(pos >= sh, pltpu.roll(lg, sh, axis=0), 0.0)
    lg_out[0, rows, :] = lg
    lgp_out[0, rows, :] = lg - lw
    yield

    a = _sigmoid(a0_ref[...] + _dot(x_wa.astype(BF16), aup_ref[...]))
    k2 = k * (1.0 + (a - 1.0) * ka_ref[...])
    k_out[0, rows, :] = k2
    yield

    kk = k * kk_ref[...]
    kk = kk / jnp.maximum(jnp.sqrt(_dot((kk * kk).astype(BF16), ones)), 1e-12)
    as_out[0, rows, :] = -kk
    bs_out[0, rows, :] = kk * a
    yield

    put_gate(_dot(_sigmoid(x_g).astype(BF16), gup_ref[...]))
    yield

    put_bonus(_dot((r * k2 * rk_ref[...]).astype(BF16), ones) * v)


def _stack_heads(x):
    lo = lax.broadcasted_iota(jnp.int32, x.shape, 1) < HEAD
    return jnp.concatenate([jnp.where(lo, x, 0.0), jnp.where(lo, 0.0, x)], axis=0)


def _unstack_heads(x):
    return x[0:CHUNK, :] + x[CHUNK:2 * CHUNK, :]


def _rwkv_local_kernel(*refs, n_chunks):
    (zr_ref, zl_ref, zr_next, zl_next), params = refs[:4], refs[4:15]
    t_out, g_out, rh_out, yh_out, gate_out, bonus_out = refs[15:21]
    scan, (gate_carry, bonus_carry) = refs[21:28], refs[28:]
    r_ref, k_ref, v_ref, lg_ref, lgp_ref, as_ref, bs_ref = scan
    two_c = 2 * CHUNK
    row = lax.broadcasted_iota(jnp.int32, (two_c, two_c), 0)
    col = lax.broadcasted_iota(jnp.int32, (two_c, two_c), 1)
    strict = col < row
    incl = col <= row
    eye = (col == row).astype(F32)
    wave_rows = LOCAL_WAVE * CHUNK
    n_waves = n_chunks // LOCAL_WAVE
    first_rows = slice(0, wave_rows)
    f32 = lambda ref, rows: ref[0, rows, :].astype(F32)

    def put(ref, rows):
        def store(val):
            ref[rows] = val
        return store

    def prep(w):
        rows = slice(w * wave_rows, (w + 1) * wave_rows)
        before = slice(w * wave_rows - PREV_ROWS, w * wave_rows)
        return _rwkv_prep(f32(zr_ref, rows), f32(zl_ref, rows), f32(zr_ref, before), f32(zl_ref, before), rows,
                          params, scan, put(gate_out, (0, rows)), put(bonus_out, (0, rows)))

    def prep_first(zr_src, zl_src, prev_r, prev_l):
        return _rwkv_prep(f32(zr_src, first_rows), f32(zl_src, first_rows), prev_r, prev_l, first_rows,
                          params, scan, put(gate_carry, slice(None)), put(bonus_carry, slice(None)))

    @pl.when(pl.program_id(1) == 0)
    def _():
        zero = lambda ref: jnp.zeros((PREV_ROWS, ref.shape[-1]), F32)
        _interleave(prep_first(zr_ref, zl_ref, zero(zr_ref), zero(zl_ref)))

    gate_out[0, first_rows, :] = gate_carry[...]
    bonus_out[0, first_rows, :] = bonus_carry[...]

    def wave(w):
        c0 = w * LOCAL_WAVE
        units = [(c, p) for c in range(c0, min(c0 + LOCAL_WAVE, n_chunks)) for p in range(N_PAIRS)]
        at = lambda ref: [ref[0, c * CHUNK:(c + 1) * CHUNK, p * PAIR:(p + 1) * PAIR] for c, p in units]
        each = lambda f, *ls: [f(*xs) for xs in zip(*ls)]
        lg = at(lg_ref)
        lg_end = each(lambda x: x[CHUNK - 1:CHUNK, :], lg)
        e_neg = each(lambda x: jnp.exp(-x), lg)
        e_end = each(lambda x, xe: jnp.exp(xe - x), lg, lg_end)
        a_s, b_s, kk = at(as_ref), at(bs_ref), at(k_ref)
        a_t = each(lambda x, gp: _stack_heads(x * jnp.exp(gp)), a_s, at(lgp_ref))
        r_t = each(lambda x, g: _stack_heads(x * jnp.exp(g)), at(r_ref), lg)
        b_t = each(lambda x, e: _stack_heads(x * e), b_s, e_neg)
        k_t = each(lambda x, e: _stack_heads(x * e), kk, e_neg)
        b_p = each(lambda x, e: _stack_heads(x * e), b_s, e_end)
        k_p = each(lambda x, e: _stack_heads(x * e), kk, e_end)
        v_s = each(_stack_heads, at(v_ref))
        yield

        vcat = lambda x, y: jnp.concatenate([x, y], axis=0)
        hcat = lambda x, y: jnp.concatenate([x, y], axis=1)
        top = lambda x: x[0:two_c]
        bot = lambda x: x[two_c:2 * two_c]
        left = lambda x: x[:, 0:two_c]
        right = lambda x: x[:, two_c:2 * two_c]

        prod = each(lambda a, r, b, k: _mm(vcat(a, r), vcat(b, k), _NT), a_t, r_t, b_t, k_t)
        l_ab = each(lambda x: jnp.where(strict, left(top(x)), 0.0), prod)
        l_ak = each(lambda x: jnp.where(strict, right(top(x)), 0.0), prod)
        l_rb = each(lambda x: jnp.where(incl, left(bot(x)), 0.0), prod)
        l_rk = each(lambda x: jnp.where(incl, right(bot(x)), 0.0), prod)
        yield

        inv = each(lambda x: eye + x, l_ab)
        pw = each(lambda x: _mm(x, x), l_ab)
        yield
        for _ in range(int(math.log2(CHUNK)) - 2):
            both = each(lambda p, x: _mm(vcat(p, x), p), pw, inv)
            pw = each(top, both)
            inv = each(lambda x, y: x + bot(y), inv, both)
            yield
        inv = each(lambda x, p: x + _mm(x, p), inv, pw)
        yield

        lv_rkv = each(lambda l1, l2, v: _mm(vcat(l1, l2), v), l_ak, l_rk, v_s)
        yield
        aw_h = each(lambda m, a, x: _mm(m, hcat(a, top(x))), inv, a_t, lv_rkv)
        yield
        l_aw = each(_mm, l_rb, aw_h)
        r_h = each(lambda x, y: x + left(y), r_t, l_aw)
        y_h = each(lambda y, x: right(y) + bot(x), l_aw, lv_rkv)
        yield
        tg = each(lambda x, bp: _mm(x, bp, _TN), aw_h, b_p)
        t_m = each(lambda ge, x: eye * jnp.exp(ge) + top(x), lg_end, tg)
        yield
        g_m = each(lambda x, v, kp: bot(x) + _mm(v, kp, _TN), tg, v_s, k_p)
        for i, (c, p) in enumerate(units):
            rows = slice(c * CHUNK, (c + 1) * CHUNK)
            lanes = slice(p * PAIR, (p + 1) * PAIR)
            t_out[0, c, p] = t_m[i].astype(t_out.dtype)
            g_out[0, c, p] = g_m[i]
            rh_out[0, rows, lanes] = _unstack_heads(r_h[i]).astype(rh_out.dtype)
            yh_out[0, rows, lanes] = _unstack_heads(y_h[i])

    last = slice(n_waves * wave_rows - PREV_ROWS, n_waves * wave_rows)
    for w in range(n_waves):
        nxt = prep(w + 1) if w + 1 < n_waves else prep_first(zr_next, zl_next, f32(zr_ref, last), f32(zl_ref, last))
        _interleave(wave(w), nxt)


def _rwkv_local(z3, p, ts):
    b, s, _ = z3.shape
    n_chunks = ts // CHUNK
    wr, wl = 3 * WIDTH_A, LORA_PAD
    lora_block = wr // wl
    wave_rows = LOCAL_WAVE * CHUNK
    n_waves = ts // wave_rows
    assert n_waves >= 2 and ts % wave_rows == 0
    nxt = lambda c: (lambda bi, i: (bi, jnp.minimum((i + 1) * n_waves, s // wave_rows - 1), c))
    cur = lambda bi, i: (bi, i, 0)
    mat = lambda bi, i: (bi, i, 0, 0, 0)
    const = lambda bi, i: (0, 0)
    vec = pl.BlockSpec((1, WIDTH_A), const)
    seq_spec = pl.BlockSpec((1, ts, WIDTH_A), cur)
    mat_spec = pl.BlockSpec((1, n_chunks, N_PAIRS, PAIR, PAIR), mat)
    seq_shape = lambda dt: jax.ShapeDtypeStruct((b, s, WIDTH_A), dt)
    mat_shape = lambda dt: jax.ShapeDtypeStruct((b, s // CHUNK, N_PAIRS, PAIR, PAIR), dt)
    return pl.pallas_call(
        functools.partial(_rwkv_local_kernel, n_chunks=n_chunks),
        grid=(b, s // ts),
        in_specs=[pl.BlockSpec((1, ts, wr), cur),
                  pl.BlockSpec((1, ts, wl), lambda bi, i: (bi, i, lora_block)),
                  pl.BlockSpec((1, wave_rows, wr), nxt(0)),
                  pl.BlockSpec((1, wave_rows, wl), nxt(lora_block)),
                  pl.BlockSpec((1, wr), const), pl.BlockSpec((1, wl), const),
                  vec, pl.BlockSpec((LORA_W + LORA_A, WIDTH_A), const),
                  vec, pl.BlockSpec((LORA_W + LORA_A, WIDTH_A), const),
                  pl.BlockSpec((LORA_PAD - LORA_W - LORA_A, WIDTH_A), const),
                  vec, vec, vec, pl.BlockSpec((WIDTH_A, WIDTH_A), const)],
        out_specs=[mat_spec, mat_spec, seq_spec, seq_spec, seq_spec, seq_spec],
        out_shape=[mat_shape(BF16), mat_shape(F32), seq_shape(BF16), seq_shape(F32), seq_shape(F32), seq_shape(F32)],
        scratch_shapes=[pltpu.VMEM((1, ts, WIDTH_A), F32)] * 7 + [pltpu.VMEM((wave_rows, WIDTH_A), F32)] * 2,
        compiler_params=_cparams(("parallel", "arbitrary")),
        name="rwkv_local",
    )(z3, z3, z3, z3, p["mu_r"], p["mu_l"], p["w0"], p["w_up"], p["a0"], p["a_up"], p["g_up"],
      p["k_k"], p["k_a"], p["r_k"], p["ones"])


def _rwkv_state_kernel(t_ref, gm_ref, rh_ref, yh_ref, g_ref, bonus_ref, lng_ref, lnb_ref, ones_ref,
                       y_out, s_ref, *, n_chunks):
    @pl.when(pl.program_id(1) == 0)
    def _():
        s_ref[...] = jnp.zeros_like(s_ref)

    ones = ones_ref[...]
    state = [s_ref[p] for p in range(N_PAIRS)]
    entry = []
    for c in range(n_chunks):
        entry.append(list(state))
        state = [_mm(state[p], t_ref[0, c, p]) + gm_ref[0, c, p] for p in range(N_PAIRS)]
    for p in range(N_PAIRS):
        s_ref[p] = state[p]
    chunks = range(n_chunks)
    rows = [slice(c * CHUNK, (c + 1) * CHUNK) for c in chunks]
    y = [jnp.concatenate([_mm(rh_ref[0, rows[c], p * PAIR:(p + 1) * PAIR], entry[c][p], _NT)
                          for p in range(N_PAIRS)], axis=1) + yh_ref[0, rows[c], :] for c in chunks]
    mean = [_dot(v.astype(BF16), ones) * (1.0 / HEAD) for v in y]
    yc = [v - m for v, m in zip(y, mean)]
    var = [_dot((v * v).astype(BF16), ones) * (1.0 / HEAD) for v in yc]
    for c in chunks:
        yn = yc[c] * lax.rsqrt(var[c] + GN_EPS) * lng_ref[...] + lnb_ref[...]
        y_out[0, rows[c], :] = ((yn + bonus_ref[0, rows[c], :]) * g_ref[0, rows[c], :]).astype(y_out.dtype)


def _rwkv_state(t, gm, rh, yh, g, bonus, p, ts):
    b, s, _ = rh.shape
    n_chunks = ts // CHUNK
    cur = lambda bi, i: (bi, i, 0)
    mat = lambda bi, i: (bi, i, 0, 0, 0)
    const = lambda bi, i: (0, 0)
    seq_spec = pl.BlockSpec((1, ts, WIDTH_A), cur)
    mat_spec = pl.BlockSpec((1, n_chunks, N_PAIRS, PAIR, PAIR), mat)
    vec = pl.BlockSpec((1, WIDTH_A), const)
    return pl.pallas_call(
        functools.partial(_rwkv_state_kernel, n_chunks=n_chunks),
        grid=(b, s // ts),
        in_specs=[mat_spec, mat_spec, seq_spec, seq_spec, seq_spec, seq_spec, vec, vec,
                  pl.BlockSpec((WIDTH_A, WIDTH_A), const)],
        out_specs=seq_spec,
        out_shape=jax.ShapeDtypeStruct((b, s, WIDTH_A), BF16),
        scratch_shapes=[pltpu.VMEM((N_PAIRS, PAIR, PAIR), F32)],
        compiler_params=_cparams(("parallel", "arbitrary")),
        name="rwkv_state",
    )(t, gm, rh, yh, g, bonus, p["ln_g"], p["ln_b"], p["ones"])


def _rwkv_mixer(z3, p):
    t, gm, rh, yh, g, bonus = _rwkv_local(z3, p, ts=512)
    return _rwkv_state(t, gm, rh, yh, g, bonus, p, ts=512)


def _lru_slab(x, yb, prev, seq_start, h_prev, params, emit):
    cw_ref, cb_ref, wa_ref, ba_ref, wx_ref, bx_ref, lam_ref = params
    n_rows = x.shape[0]
    ext = jnp.concatenate([prev, x], axis=0)
    xc = x * cw_ref[CONV_TAPS - 1:CONV_TAPS, :] + cb_ref[...]
    for back in range(1, CONV_TAPS):
        tap = CONV_TAPS - 1 - back
        xc = xc + pltpu.roll(ext, back, axis=0)[PREV_ROWS:, :] * cw_ref[tap:tap + 1, :]

    half = WIDTH_B // 2
    xcb = xc.astype(BF16)
    ga = jnp.concatenate([_dot(xcb[:, j * half:(j + 1) * half], wa_ref[j]) for j in range(2)], axis=1)
    gx = jnp.concatenate([_dot(xcb[:, j * half:(j + 1) * half], wx_ref[j]) for j in range(2)], axis=1)
    gate_a = _sigmoid(ga + ba_ref[...])
    gate_x = _sigmoid(gx + bx_ref[...])
    log_a = -LRU_C * gate_a * _softplus(-lam_ref[...])
    a = jnp.exp(log_a)
    mult = jnp.sqrt(jnp.maximum(-jnp.tanh(log_a) * (1.0 + a * a), 0.0))
    xg = xc * gate_x
    b = xg * mult
    yield

    row8 = lax.broadcasted_iota(jnp.int32, (8, WIDTH_B), 0)
    h, hs = h_prev, []
    for g in range(n_rows // 8):
        a8, b8 = a[8 * g:8 * g + 8, :], b[8 * g:8 * g + 8, :]
        if g == 0:
            b8 = jnp.where(jnp.logical_and(row8 == 0, seq_start), xg[0:8, :], b8)
        for sh in (1, 2, 4):
            ar = pltpu.roll(a8, sh, axis=0)
            br = pltpu.roll(b8, sh, axis=0)
            m = row8 >= sh
            b8 = jnp.where(m, a8 * br + b8, b8)
            a8 = jnp.where(m, a8 * ar, a8)
        h8 = a8 * h + b8
        hs.append(h8)
        h = h8[7:8, :]
    yield
    gelu = 0.5 * yb * (1.0 + jnp.tanh(math.sqrt(2.0 / math.pi) * (yb + 0.044715 * (yb * yb * yb))))
    emit(x, jnp.concatenate(hs, axis=0) * gelu, h)


def _attn_kernel(q0, q1, k0, k1, kp0, kp1, v0, v1, vp0, vp1, bias_ref, qg_ref, kg_ref, ones_ref,
                 o0, o1, l0, l1, *, dil, n_sub):
    j = pl.program_id(1)
    ones = ones_ref[...]
    lane = lax.broadcasted_iota(jnp.int32, (QBLK, GROUP_W), 1)
    in_head = [(lane >= h * HEAD) & (lane < (h + 1) * HEAD) for h in range(HEADS_PER_GROUP)]
    prev_valid = (lax.broadcasted_iota(jnp.int32, (HEADS_PER_GROUP * QBLK, 2 * QBLK), 1) >= QBLK) | (j > 0)

    def rows(start):
        return pl.ds(start, QBLK, stride=dil) if dil > 1 else pl.ds(start, QBLK)

    def take(lo, hi, start):
        return jnp.concatenate([lo[0, rows(start), :], hi[0, rows(start), :]], axis=1)

    def head_sumsq(x):
        return _dot((x * x).astype(BF16), ones) * (1.0 / HEAD)

    def select_heads(x):
        out = jnp.zeros((QBLK, GROUP_W), F32)
        for h, m in enumerate(in_head):
            out = jnp.where(m, x[h * QBLK:(h + 1) * QBLK, :], out)
        return out

    def wave(units):
        each = lambda f, *ls: [f(*xs) for xs in zip(*ls)]
        span = dil * QBLK
        starts = [s for s, _ in units]
        before = [s - span if dil > 1 or isinstance(s, int) else pl.multiple_of(s - span, QBLK) for s in starts]
        q_raw = [take(q0, q1, s) for s in starts]
        k_raw = [jnp.concatenate([take(kp0, kp1, s) if far else take(k0, k1, p), take(k0, k1, s)], axis=0)
                 for (s, far), p in zip(units, before)]
        vv = [jnp.concatenate([take(vp0, vp1, s) if far else take(v0, v1, p), take(v0, v1, s)],
                              axis=0).astype(BF16) for (s, far), p in zip(units, before)]
        q_ms = each(head_sumsq, q_raw)
        k_ms = each(head_sumsq, k_raw)
        q = each(lambda x, ms: x * lax.rsqrt(ms + RMS_EPS) * qg_ref[...] * (HEAD ** -0.5), q_raw, q_ms)
        kk = each(lambda x, ms: (x * lax.rsqrt(ms + RMS_EPS) * kg_ref[...]).astype(BF16), k_raw, k_ms)
        qs = each(lambda x: jnp.concatenate([jnp.where(m, x, 0.0) for m in in_head], axis=0).astype(BF16), q)
        logits = each(lambda a, b: _dot_nt(a, b) + bias_ref[...], qs, kk)
        logits = [jnp.where(prev_valid, lg, NEG_INF) if far else lg for lg, (_, far) in zip(logits, units)]
        mx = each(lambda lg: jnp.max(lg, axis=-1, keepdims=True), logits)
        pr = each(lambda lg, m: jnp.exp(lg - m), logits, mx)
        den = each(lambda p: jnp.sum(p, axis=-1, keepdims=True), pr)
        pv = each(lambda p, v, dn: _dot(p.astype(BF16), v) / dn, pr, vv, den)
        out = each(select_heads, pv)
        lse = each(lambda m, dn: select_heads(jnp.broadcast_to(m + jnp.log(dn), (HEADS_PER_GROUP * QBLK, GROUP_W))),
                   mx, den)
        for s, o, l in zip(starts, out, lse):
            o0[0, rows(s), :] = o[:, 0:PAIR]
            o1[0, rows(s), :] = o[:, PAIR:GROUP_W]
            l0[0, rows(s), :] = l[:, 0:PAIR]
            l1[0, rows(s), :] = l[:, PAIR:GROUP_W]

    def loop(lo, hi, body):
        def step(i, carry):
            body(i)
            return carry
        lax.fori_loop(lo, hi, step, 0)

    span = dil * QBLK
    if n_sub == 1:
        loop(0, dil // ATTN_WAVE, lambda i: wave([(i * ATTN_WAVE + u, True) for u in range(ATTN_WAVE)]))
    elif dil > 1:
        wave([(r, True) for r in range(dil)])
        loop(1, n_sub, lambda n: wave([(r + n * span, False) for r in range(dil)]))
    else:
        wave([(u * QBLK, u == 0) for u in range(ATTN_WAVE)])
        loop(1, n_sub // ATTN_WAVE,
             lambda i: wave([(pl.multiple_of((i * ATTN_WAVE + u) * QBLK, QBLK), False) for u in range(ATTN_WAVE)]))


def _attn_group(z3, col0, bias, qg, kg, ones, gi, dil):
    b, s, _ = z3.shape
    span = dil * QBLK
    n_sub = ATTN_TILE // span
    halves = GROUP_W // PAIR
    per_part = len(GROUPS) * halves

    def cur(part, half):
        c = col0 + part * per_part + gi * halves + half
        return pl.BlockSpec((1, ATTN_TILE, PAIR), lambda bi, j: (bi, j, c))

    def prev(part, half):
        c = col0 + part * per_part + gi * halves + half
        return pl.BlockSpec((1, span, PAIR), lambda bi, j: (bi, jnp.maximum(j * n_sub - 1, 0), c))

    const2 = lambda bi, j: (0, 0)
    out_spec = pl.BlockSpec((1, ATTN_TILE, PAIR), lambda bi, j: (bi, j, 0))
    return pl.pallas_call(
        functools.partial(_attn_kernel, dil=dil, n_sub=n_sub),
        grid=(b, s // ATTN_TILE),
        in_specs=[cur(0, 0), cur(0, 1), cur(1, 0), cur(1, 1), prev(1, 0), prev(1, 1),
                  cur(2, 0), cur(2, 1), prev(2, 0), prev(2, 1),
                  pl.BlockSpec((HEADS_PER_GROUP * QBLK, 2 * QBLK), const2),
                  pl.BlockSpec((1, GROUP_W), const2), pl.BlockSpec((1, GROUP_W), const2),
                  pl.BlockSpec((GROUP_W, GROUP_W), const2)],
        out_specs=[out_spec] * 4,
        out_shape=[jax.ShapeDtypeStruct((b, s, PAIR), F32)] * 4,
        compiler_params=_cparams(("parallel", "arbitrary")),
        name=f"dilated_attn_g{gi}",
    )(*([z3] * 10), bias, qg, kg, ones)


def _t5_bucket(dist):
    max_exact = N_BUCKETS // 2
    d = jnp.maximum(dist, 0)
    large = max_exact + (jnp.log(jnp.maximum(d, 1).astype(F32) / max_exact)
                         / math.log(MAX_DISTANCE / max_exact) * (N_BUCKETS - max_exact)).astype(jnp.int32)
    large = jnp.minimum(large, N_BUCKETS - 1)
    return jnp.where(d < max_exact, d, large)


def _attn_bias_tiles(rel_bias):
    tiles = []
    kj = jnp.arange(2 * QBLK)[None, :]
    rel = (jnp.arange(QBLK)[:, None] + QBLK) - kj
    for gi, (window, dil) in enumerate(GROUPS):
        band = (rel >= 0) & (rel <= window // dil)
        tab = rel_bias.astype(F32)[:, gi * HEADS_PER_GROUP:(gi + 1) * HEADS_PER_GROUP]
        onehot = (_t5_bucket(rel * dil)[..., None] == jnp.arange(N_BUCKETS)).astype(F32)
        bias = jnp.einsum("qkn,nh->hqk", onehot, tab, precision=lax.Precision.HIGHEST)
        tiles.append(jnp.where(band[None], bias, NEG_INF).reshape(HEADS_PER_GROUP * QBLK, 2 * QBLK))
    return tiles


def _merge_kernel(x_ref, ya_ref, yb_ref, *rest):
    n_g = len(GROUPS)
    attn = rest[:4 * n_g]
    zg_ref, pa_ref, pb_ref, pc_ref, wo_ref, out_ref = rest[4 * n_g:]
    d = x_ref.shape[-1]
    outs = [jnp.concatenate([attn[4 * g][...], attn[4 * g + 1][...]], axis=1) for g in range(n_g)]
    lses = [jnp.concatenate([attn[4 * g + 2][...], attn[4 * g + 3][...]], axis=1) for g in range(n_g)]
    m = functools.reduce(jnp.maximum, lses)
    es = [jnp.exp(l - m) for l in lses]
    yc = sum(o * e for o, e in zip(outs, es)) / sum(es)
    gate = lambda n: _sigmoid(zg_ref[:, n * d:(n + 1) * d].astype(F32))
    merged = (gate(0) * _dot(ya_ref[...].astype(BF16), pa_ref[...])
              + gate(1) * _dot(yb_ref[...].astype(BF16), pb_ref[...])
              + gate(2) * _dot(yc.astype(BF16), pc_ref[...]))
    out_ref[...] = x_ref[...] + _dot(merged.astype(BF16), wo_ref[...])


def _merge(x2d, ya, yb, attn, z2d, gate_block, p, tm):
    m, d = x2d.shape
    row = lambda w: pl.BlockSpec((tm, w), lambda i: (i, 0))
    full = lambda a: pl.BlockSpec(a.shape, lambda i: (0, 0))
    return pl.pallas_call(
        _merge_kernel,
        grid=(m // tm,),
        in_specs=[row(d), row(WIDTH_A), row(WIDTH_B)] + [row(PAIR)] * len(attn)
                 + [pl.BlockSpec((tm, 3 * d), lambda i: (i, gate_block)),
                    full(p["proj_a"]), full(p["proj_b"]), full(p["proj_c"]), full(p["w_out"])],
        out_specs=row(d),
        out_shape=jax.ShapeDtypeStruct((m, d), F32),
        compiler_params=_cparams(("parallel",)),
        name="merge",
    )(x2d, ya, yb, *attn, z2d, p["proj_a"], p["proj_b"], p["proj_c"], p["w_out"])


def _mlp_kernel(x_ref, g_ref, wu_ref, wd_ref, o_ref, h_ref, acc_ref, u_ref, *, nj):
    j = pl.program_id(1)
    cur = lax.rem(j, 2)

    def up():
        u = jnp.maximum(_dot(h_ref[...], wu_ref[...]), 0.0)
        u_ref[cur] = (u * u).astype(BF16)

    def down():
        acc_ref[...] += _dot(u_ref[1 - cur], wd_ref[...])

    @pl.when(j == 0)
    def _():
        x = x_ref[...]
        ms = jnp.mean(x * x, axis=-1, keepdims=True)
        h_ref[...] = (x * lax.rsqrt(ms + RMS_EPS) * g_ref[...]).astype(BF16)
        acc_ref[...] = jnp.zeros_like(acc_ref)
        up()

    @pl.when(jnp.logical_and(j > 0, j < nj))
    def _():
        down()
        up()

    @pl.when(j == nj)
    def _():
        down()
        o_ref[...] = x_ref[...] + acc_ref[...]


def _mlp(x2d, g, wu, wd, tm, tf):
    m, d = x2d.shape
    f = wu.shape[1]
    nj = f // tf
    return pl.pallas_call(
        functools.partial(_mlp_kernel, nj=nj),
        grid=(m // tm, nj + 1),
        in_specs=[pl.BlockSpec((tm, d), lambda i, j: (i, 0)),
                  pl.BlockSpec((1, d), lambda i, j: (0, 0)),
                  pl.BlockSpec((d, tf), lambda i, j: (0, jnp.minimum(j, nj - 1))),
                  pl.BlockSpec((tf, d), lambda i, j: (jnp.maximum(j - 1, 0), 0))],
        out_specs=pl.BlockSpec((tm, d), lambda i, j: (i, 0)),
        out_shape=jax.ShapeDtypeStruct((m, d), F32),
        scratch_shapes=[pltpu.VMEM((tm, d), BF16), pltpu.VMEM((tm, d), F32), pltpu.VMEM((2, tm, tf), BF16)],
        compiler_params=_cparams(("parallel", "arbitrary")),
        name="mlp",
    )(x2d, g, wu, wd)


def _pad_rows(w, lo, total):
    return jnp.pad(w, ((lo, total - lo - w.shape[0]), (0, 0)))


def _block_diag_halves(w):
    n, bd, _ = w.shape
    per = n // 2
    out = jnp.zeros((2, per * bd, per * bd), w.dtype)
    for i in range(n):
        j, q = divmod(i, per)
        out = out.at[j, q * bd:(q + 1) * bd, q * bd:(q + 1) * bd].set(w[i])
    return out.astype(BF16)


def _layer(x, l, bias_tiles, proj, prm):
    (norm_mix_g, _, rwkv_mu, rwkv_w0, rwkv_w_up, rwkv_a0, rwkv_a_up, rwkv_g_up, rwkv_k_k, rwkv_k_a,
     rwkv_r_k, rwkv_ln_g, rwkv_ln_b, proj_a, conv_w, conv_b, lru_wa, lru_ba, lru_wx, lru_bx, lru_lambda,
     proj_b, q_norm_g, k_norm_g, proj_c, w_out, norm_mlp_g, mlp_up, mlp_down) = [t[l] for t in prm]
    b, s, d = x.shape
    x2d = x.reshape(b * s, d)
    row = lambda t: t.reshape(1, -1).astype(F32)

    wz_all, z_b, z_g, z_c = proj
    c_rkv, c_lora = 3 * WIDTH_A, LORA_W + LORA_A + LORA_G
    pb = dict(conv_w=conv_w.astype(F32), conv_b=row(conv_b), wa=_block_diag_halves(lru_wa), ba=row(lru_ba),
              wx=_block_diag_halves(lru_wx), bx=row(lru_bx), lam=row(lru_lambda))
    z2d, zc2d, yb = _projection(x2d, row(norm_mix_g), wz_all, l, pb, z_b, z_c, s, 2048, 768)
    z3 = z2d.reshape(b, s, z_c)
    zc3 = zc2d.reshape(b, s, 3 * WIDTH_C)

    ones_a = _head_ones(WIDTH_A)
    pa = dict(
        mu_r=row(rwkv_mu[0:c_rkv]),
        mu_l=jnp.pad(row(rwkv_mu[c_rkv:]), ((0, 0), (0, LORA_PAD - c_lora))),
        w0=row(rwkv_w0), a0=row(rwkv_a0),
        w_up=_pad_rows(rwkv_w_up, 0, LORA_W + LORA_A).astype(BF16),
        a_up=_pad_rows(rwkv_a_up, LORA_W, LORA_W + LORA_A).astype(BF16),
        g_up=_pad_rows(rwkv_g_up, 0, LORA_PAD - LORA_W - LORA_A).astype(BF16),
        k_k=row(rwkv_k_k), k_a=row(rwkv_k_a), r_k=row(rwkv_r_k),
        ln_g=row(rwkv_ln_g), ln_b=row(rwkv_ln_b), ones=ones_a)
    ya = _rwkv_mixer(z3, pa)

    qg = jnp.tile(row(q_norm_g), (1, HEADS_PER_GROUP))
    kg = jnp.tile(row(k_norm_g), (1, HEADS_PER_GROUP))
    ones_c = _head_ones(GROUP_W)
    attn = []
    for gi, (_, dil) in enumerate(GROUPS):
        parts = _attn_group(zc3, 0, bias_tiles[gi], qg, kg, ones_c, gi, dil)
        attn += [t.reshape(b * s, PAIR) for t in parts]

    pm = dict(proj_a=proj_a.astype(BF16), proj_b=proj_b.astype(BF16), proj_c=proj_c.astype(BF16),
              w_out=w_out.astype(BF16))
    x1 = _merge(x2d, ya.reshape(b * s, WIDTH_A), yb, attn, z2d, z_g // (3 * d), pm, 512)
    x2 = _mlp(x1, row(norm_mlp_g), mlp_up.astype(BF16), mlp_down.astype(BF16), 1024, 1024)
    return x2.reshape(b, s, d)


def kernel(x, rel_bias, norm_mix_g, w_in, rwkv_mu, rwkv_w0, rwkv_w_up, rwkv_a0, rwkv_a_up, rwkv_g_up, rwkv_k_k, rwkv_k_a, rwkv_r_k, rwkv_ln_g, rwkv_ln_b, proj_a, conv_w, conv_b, lru_wa, lru_ba, lru_wx, lru_bx, lru_lambda, proj_b, q_norm_g, k_norm_g, proj_c, w_out, norm_mlp_g, mlp_up, mlp_down):
    prm = (norm_mix_g, w_in, rwkv_mu, rwkv_w0, rwkv_w_up, rwkv_a0, rwkv_a_up, rwkv_g_up, rwkv_k_k, rwkv_k_a,
           rwkv_r_k, rwkv_ln_g, rwkv_ln_b, proj_a, conv_w, conv_b, lru_wa, lru_ba, lru_wx, lru_bx, lru_lambda,
           proj_b, q_norm_g, k_norm_g, proj_c, w_out, norm_mlp_g, mlp_up, mlp_down)
    bias_tiles = _attn_bias_tiles(rel_bias)
    proj = _projection_weights(jnp.swapaxes(w_in.astype(F32), 1, 2))
    x = x.astype(F32)
    for l in range(norm_mix_g.shape[0]):
        x = _layer(x, l, bias_tiles, proj, prm)
    return x
```

```python
import functools
import math

import jax
import jax.numpy as jnp
from jax import lax
from jax.experimental import pallas as pl
from jax.experimental.pallas import tpu as pltpu

F32 = jnp.float32
BF16 = jnp.bfloat16

N_HEADS_A = 8
HEAD = 64
PAIR = 2 * HEAD
WIDTH_A = N_HEADS_A * HEAD
N_PAIRS = WIDTH_A // PAIR
CHUNK = 64
LOCAL_WAVE = 4
PREV_ROWS = 16
LRU_SLAB = 256
MXU_N = 256
LORA_W, LORA_A, LORA_G = 64, 64, 160
LORA_PAD = 384
GN_EPS = 64e-5
WIDTH_B = 512
LRU_BLOCK = 64
CONV_TAPS = 4
LRU_C = 8.0
GROUPS = ((128, 1), (512, 4), (2048, 16))
HEADS_PER_GROUP = 4
GROUP_W = HEADS_PER_GROUP * HEAD
WIDTH_C = len(GROUPS) * GROUP_W
QBLK = 128
ATTN_TILE = 2048
ATTN_WAVE = 4
N_BUCKETS = 32
MAX_DISTANCE = 2048
NEG_INF = -1e30
RMS_EPS = 1e-6
VMEM_LIMIT = 56 * 1024 * 1024


def _cparams(sem):
    return pltpu.CompilerParams(dimension_semantics=sem, vmem_limit_bytes=VMEM_LIMIT)


def _dot(a, b):
    return jnp.dot(a, b, preferred_element_type=F32)


def _dot_nt(a, b):
    return lax.dot_general(a, b, (((1,), (1,)), ((), ())), preferred_element_type=F32)


_NN = (((1,), (0,)), ((), ()))
_NT = (((1,), (1,)), ((), ()))
_TN = (((0,), (0,)), ((), ()))


def _mm(a, b, dims=_NN):
    return lax.dot_general(a.astype(BF16), b.astype(BF16), dims, preferred_element_type=F32)


def _sigmoid(x):
    return 1.0 / (1.0 + jnp.exp(-x))


def _softplus(x):
    return jnp.maximum(x, 0.0) + jnp.log1p(jnp.exp(-jnp.abs(x)))


def _head_ones(width):
    i = jnp.arange(width) // HEAD
    return (i[:, None] == i[None, :]).astype(BF16)


def _projection_kernel(x_ref, g_ref, w_ref, *rest, n_lo, copies, first_slab, tiles_per_seq):
    lru, (lo_ref, hi_ref, yb_ref, h_ref, zb_ref, hist_ref, carry_ref) = rest[:7], rest[7:]
    i, j = pl.program_id(0), pl.program_id(1)

    @pl.when(j == 0)
    def _():
        x = x_ref[...]
        ms = jnp.mean(x * x, axis=-1, keepdims=True)
        h_ref[...] = (x * lax.rsqrt(ms + RMS_EPS) * g_ref[...]).astype(BF16)

    s = j - first_slab
    rows = pl.ds(pl.multiple_of(s * LRU_SLAB, LRU_SLAB), LRU_SLAB)
    seq_start = jnp.logical_and(s == 0, lax.rem(i, tiles_per_seq) == 0)

    def slab_load():
        return (zb_ref[rows, 0:WIDTH_B].astype(F32), zb_ref[rows, WIDTH_B:2 * WIDTH_B].astype(F32),
                jnp.where(seq_start, 0.0, hist_ref[...]), jnp.where(seq_start, 0.0, carry_ref[0:1, :]))

    def slab_store(x, out, h_last):
        yb_ref[rows, :] = out.astype(yb_ref.dtype)
        hist_ref[...] = x[LRU_SLAB - PREV_ROWS:, :]
        carry_ref[0:1, :] = h_last

    def matmul(out_ref, copy):
        h = h_ref[...]
        for c0 in range(0, out_ref.shape[1], MXU_N):
            z = _dot_nt(h, w_ref[c0:c0 + MXU_N, :])
            out_ref[:, c0:c0 + MXU_N] = z.astype(out_ref.dtype)
            if copy is not None:
                src, dst, n = copy
                lo_c, hi_c = max(src, c0), min(src + n, c0 + MXU_N)
                if lo_c < hi_c:
                    zb_ref[:, dst + lo_c - src:dst + hi_c - src] = z[:, lo_c - c0:hi_c - c0].astype(zb_ref.dtype)
            yield

    def step(out_ref, copy, with_slab):
        if not with_slab:
            _interleave(matmul(out_ref, copy))
        elif copy is None:
            x, yb, prev, h_prev = slab_load()
            _interleave(matmul(out_ref, None), _lru_slab(x, yb, prev, seq_start, h_prev, lru, slab_store))
        else:
            _interleave(matmul(out_ref, copy))
            x, yb, prev, h_prev = slab_load()
            _interleave(_lru_slab(x, yb, prev, seq_start, h_prev, lru, slab_store))

    plain = j < first_slab
    for jb in copies:
        plain = jnp.logical_and(plain, j != jb)
        pl.when(j == jb)(functools.partial(step, lo_ref, copies[jb], jb == first_slab))
    pl.when(plain)(functools.partial(step, lo_ref, None, False))
    pl.when(jnp.logical_and(j > first_slab, j < n_lo))(functools.partial(step, lo_ref, None, True))
    pl.when(j >= n_lo)(functools.partial(step, hi_ref, None, True))


def _projection(x2d, g, w_all, layer, lru, z_b, n_lo_cols, seq_len, tm, tn):
    m, d = x2d.shape
    n = w_all.shape[1]
    n_lo, n_blocks = n_lo_cols // tn, n // tn
    copies = {}
    for jb in range(n_blocks):
        lo_c, hi_c = max(jb * tn, z_b), min((jb + 1) * tn, z_b + 2 * WIDTH_B)
        if lo_c < hi_c:
            copies[jb] = (lo_c - jb * tn, lo_c - z_b, hi_c - lo_c)
    first_slab = max(copies)
    assert first_slab < n_lo and n_blocks - first_slab == tm // LRU_SLAB and seq_len % tm == 0
    const2 = lambda i, j: (0, 0)
    vec = pl.BlockSpec((1, WIDTH_B), const2)
    half = WIDTH_B // 2
    mat = pl.BlockSpec((2, half, half), lambda i, j: (0, 0, 0))
    return pl.pallas_call(
        functools.partial(_projection_kernel, n_lo=n_lo, copies=copies, first_slab=first_slab,
                          tiles_per_seq=seq_len // tm),
        grid=(m // tm, n_blocks),
        in_specs=[pl.BlockSpec((tm, d), lambda i, j: (i, 0)),
                  pl.BlockSpec((1, d), const2),
                  pl.BlockSpec((None, tn, d), lambda i, j: (layer, j, 0)),
                  pl.BlockSpec((CONV_TAPS, WIDTH_B), const2), vec, mat, vec, mat, vec, vec],
        out_specs=[pl.BlockSpec((tm, tn), lambda i, j: (i, jnp.minimum(j, n_lo - 1))),
                   pl.BlockSpec((tm, tn), lambda i, j: (i, jnp.maximum(j - n_lo, 0))),
                   pl.BlockSpec((tm, WIDTH_B), lambda i, j: (i, 0))],
        out_shape=[jax.ShapeDtypeStruct((m, n_lo_cols), BF16),
                   jax.ShapeDtypeStruct((m, n - n_lo_cols), F32),
                   jax.ShapeDtypeStruct((m, WIDTH_B), BF16)],
        scratch_shapes=[pltpu.VMEM((tm, d), BF16), pltpu.VMEM((tm, 2 * WIDTH_B), BF16),
                        pltpu.VMEM((PREV_ROWS, WIDTH_B), F32), pltpu.VMEM((8, WIDTH_B), F32)],
        compiler_params=_cparams(("arbitrary", "arbitrary")),
        name="projection",
    )(x2d, g, w_all, lru["conv_w"], lru["conv_b"], lru["wa"], lru["ba"], lru["wx"], lru["bx"], lru["lam"])


def _relayout_kernel(w_ref, o_ref, *, moves, width):
    cols = w_ref.shape[2]
    end = 0
    for src, dst, n in moves:
        if dst > end:
            o_ref[0, end:dst, :] = jnp.zeros((dst - end, cols), o_ref.dtype)
        o_ref[0, dst:dst + n, :] = w_ref[0, src:src + n, :].astype(o_ref.dtype)
        end = dst + n
    if end < width:
        o_ref[0, end:width, :] = jnp.zeros((width - end, cols), o_ref.dtype)


def _projection_weights(w_t, tc=256):
    n_layers, n_in, d = w_t.shape
    c_rkv, c_lora = 3 * WIDTH_A, LORA_W + LORA_A + LORA_G
    o_b = c_rkv + c_lora
    o_c = o_b + 2 * WIDTH_B
    o_g = o_c + 3 * WIDTH_C
    z_b = -(-(c_rkv + LORA_PAD) // WIDTH_B) * WIDTH_B
    z_g = -(-(z_b + 2 * WIDTH_B) // (3 * d)) * (3 * d)
    z_c = z_g + 3 * d
    width = z_c + 3 * WIDTH_C
    moves = ((0, 0, o_b), (o_b, z_b, o_c - o_b), (o_g, z_g, n_in - o_g), (o_c, z_c, o_g - o_c))
    wz = pl.pallas_call(
        functools.partial(_relayout_kernel, moves=moves, width=width),
        grid=(n_layers, d // tc),
        in_specs=[pl.BlockSpec((1, n_in, tc), lambda l, i: (l, 0, i))],
        out_specs=pl.BlockSpec((1, width, tc), lambda l, i: (l, 0, i)),
        out_shape=jax.ShapeDtypeStruct((n_layers, width, d), BF16),
        compiler_params=_cparams(("parallel", "parallel")),
        name="projection_weights",
    )(w_t)
    return wz, z_b, z_g, z_c


def _shift_rows(cur, prev):
    prev_row = prev[PREV_ROWS - 1:PREV_ROWS, :]
    rolled = pltpu.roll(cur, 1, axis=0)
    row = lax.broadcasted_iota(jnp.int32, cur.shape, 0)
    return jnp.where(row == 0, prev_row, rolled)


def _interleave(*stages):
    live = list(stages)
    while live:
        for gen in list(live):
            try:
                next(gen)
            except StopIteration:
                live.remove(gen)


def _rwkv_prep(zr, zl, prev_r, prev_l, rows, params, scan, put_gate, put_bonus):
    mur_ref, mul_ref, w0_ref, wup_ref, a0_ref, aup_ref, gup_ref, kk_ref, ka_ref, rk_ref, ones_ref = params
    r_out, k_out, v_out, lg_out, lgp_out, as_out, bs_out = scan
    fr = zr + (_shift_rows(zr, prev_r) - zr) * mur_ref[...]
    fl = zl + (_shift_rows(zl, prev_l) - zl) * mul_ref[...]
    r = fr[:, 0:WIDTH_A]
    k = fr[:, WIDTH_A:2 * WIDTH_A]
    v = fr[:, 2 * WIDTH_A:3 * WIDTH_A]
    x_wa = fl[:, 0:LORA_W + LORA_A]
    x_g = fl[:, LORA_W + LORA_A:LORA_PAD]
    ones = ones_ref[...]
    r_out[0, rows, :] = r
    v_out[0, rows, :] = v
    yield

    w = -_softplus(-(w0_ref[...] + _dot(jnp.tanh(x_wa).astype(BF16), wup_ref[...]))) - 0.5
    lw = -jnp.exp(w)
    pos = lax.broadcasted_iota(jnp.int32, lw.shape, 0) & (CHUNK - 1)
    lg = lw
    for sh in [1 << i for i in range(int(math.log2(CHUNK)))]:
        lg = lg + jnp.where(pos >= sh, pltpu.roll(lg, sh, axis=0), 0.0)
    lg_out[0, rows, :] = lg
    lgp_out[0, rows, :] = lg - lw
    yield

    a = _sigmoid(a0_ref[...] + _dot(x_wa.astype(BF16), aup_ref[...]))
    k2 = k * (1.0 + (a - 1.0) * ka_ref[...])
    k_out[0, rows, :] = k2
    yield

    kk = k * kk_ref[...]
    kk = kk / jnp.maximum(jnp.sqrt(_dot((kk * kk).astype(BF16), ones)), 1e-12)
    as_out[0, rows, :] = -kk
    bs_out[0, rows, :] = kk * a
    yield

    put_gate(_dot(_sigmoid(x_g).astype(BF16), gup_ref[...]))
    yield

    put_bonus(_dot((r * k2 * rk_ref[...]).astype(BF16), ones) * v)


def _stack_heads(x):
    lo = lax.broadcasted_iota(jnp.int32, x.shape, 1) < HEAD
    return jnp.concatenate([jnp.where(lo, x, 0.0), jnp.where(lo, 0.0, x)], axis=0)


def _unstack_heads(x):
    return x[0:CHUNK, :] + x[CHUNK:2 * CHUNK, :]


def _rwkv_local_kernel(*refs, n_chunks):
    (zr_ref, zl_ref, zr_next, zl_next), params = refs[:4], refs[4:15]
    t_out, g_out, rh_out, yh_out, gate_out, bonus_out = refs[15:21]
    scan, (gate_carry, bonus_carry) = refs[21:28], refs[28:]
    r_ref, k_ref, v_ref, lg_ref, lgp_ref, as_ref, bs_ref = scan
    two_c = 2 * CHUNK
    row = lax.broadcasted_iota(jnp.int32, (two_c, two_c), 0)
    col = lax.broadcasted_iota(jnp.int32, (two_c, two_c), 1)
    strict = col < row
    incl = col <= row
    eye = (col == row).astype(F32)
    wave_rows = LOCAL_WAVE * CHUNK
    n_waves = n_chunks // LOCAL_WAVE
    first_rows = slice(0, wave_rows)
    f32 = lambda ref, rows: ref[0, rows, :].astype(F32)

    def put(ref, rows):
        def store(val):
            ref[rows] = val.astype(ref.dtype)
        return store

    def prep(w):
        rows = slice(w * wave_rows, (w + 1) * wave_rows)
        before = slice(w * wave_rows - PREV_ROWS, w * wave_rows)
        return _rwkv_prep(f32(zr_ref, rows), f32(zl_ref, rows), f32(zr_ref, before), f32(zl_ref, before), rows,
                          params, scan, put(gate_out, (0, rows)), put(bonus_out, (0, rows)))

    def prep_first(zr_src, zl_src, prev_r, prev_l):
        return _rwkv_prep(f32(zr_src, first_rows), f32(zl_src, first_rows), prev_r, prev_l, first_rows,
                          params, scan, put(gate_carry, slice(None)), put(bonus_carry, slice(None)))

    @pl.when(pl.program_id(1) == 0)
    def _():
        zero = lambda ref: jnp.zeros((PREV_ROWS, ref.shape[-1]), F32)
        _interleave(prep_first(zr_ref, zl_ref, zero(zr_ref), zero(zl_ref)))

    gate_out[0, first_rows, :] = gate_carry[...].astype(gate_out.dtype)
    bonus_out[0, first_rows, :] = bonus_carry[...].astype(bonus_out.dtype)

    def wave(w):
        c0 = w * LOCAL_WAVE
        units = [(c, p) for c in range(c0, min(c0 + LOCAL_WAVE, n_chunks)) for p in range(N_PAIRS)]
        at = lambda ref: [ref[0, c * CHUNK:(c + 1) * CHUNK, p * PAIR:(p + 1) * PAIR] for c, p in units]
        each = lambda f, *ls: [f(*xs) for xs in zip(*ls)]
        lg = at(lg_ref)
        lg_end = each(lambda x: x[CHUNK - 1:CHUNK, :], lg)
        e_neg = each(lambda x: jnp.exp(-x), lg)
        e_end = each(lambda x, xe: jnp.exp(xe - x), lg, lg_end)
        a_s, b_s, kk = at(as_ref), at(bs_ref), at(k_ref)
        a_t = each(lambda x, gp: _stack_heads(x * jnp.exp(gp)), a_s, at(lgp_ref))
        r_t = each(lambda x, g: _stack_heads(x * jnp.exp(g)), at(r_ref), lg)
        b_t = each(lambda x, e: _stack_heads(x * e), b_s, e_neg)
        k_t = each(lambda x, e: _stack_heads(x * e), kk, e_neg)
        b_p = each(lambda x, e: _stack_heads(x * e), b_s, e_end)
        k_p = each(lambda x, e: _stack_heads(x * e), kk, e_end)
        v_s = each(_stack_heads, at(v_ref))
        yield

        vcat = lambda x, y: jnp.concatenate([x, y], axis=0)
        hcat = lambda x, y: jnp.concatenate([x, y], axis=1)
        top = lambda x: x[0:two_c]
        bot = lambda x: x[two_c:2 * two_c]
        left = lambda x: x[:, 0:two_c]
        right = lambda x: x[:, two_c:2 * two_c]

        prod = each(lambda a, r, b, k: _mm(vcat(a, r), vcat(b, k), _NT), a_t, r_t, b_t, k_t)
        l_ab = each(lambda x: jnp.where(strict, left(top(x)), 0.0), prod)
        l_ak = each(lambda x: jnp.where(strict, right(top(x)), 0.0), prod)
        l_rb = each(lambda x: jnp.where(incl, left(bot(x)), 0.0), prod)
        l_rk = each(lambda x: jnp.where(incl, right(bot(x)), 0.0), prod)
        yield

        inv = each(lambda x: eye + x, l_ab)
        pw = each(lambda x: _mm(x, x), l_ab)
        yield
        for _ in range(int(math.log2(CHUNK)) - 2):
            both = each(lambda p, x: _mm(vcat(p, x), p), pw, inv)
            pw = each(top, both)
            inv = each(lambda x, y: x + bot(y), inv, both)
            yield
        inv = each(lambda x, p: x + _mm(x, p), inv, pw)
        yield

        lv_rkv = each(lambda l1, l2, v: _mm(vcat(l1, l2), v), l_ak, l_rk, v_s)
        yield
        aw_h = each(lambda m, a, x: _mm(m, hcat(a, top(x))), inv, a_t, lv_rkv)
        yield
        l_aw = each(_mm, l_rb, aw_h)
        r_h = each(lambda x, y: x + left(y), r_t, l_aw)
        y_h = each(lambda y, x: right(y) + bot(x), l_aw, lv_rkv)
        yield
        tg = each(lambda x, bp: _mm(x, bp, _TN), aw_h, b_p)
        t_m = each(lambda ge, x: eye * jnp.exp(ge) + top(x), lg_end, tg)
        yield
        g_m = each(lambda x, v, kp: bot(x) + _mm(v, kp, _TN), tg, v_s, k_p)
        for i, (c, p) in enumerate(units):
            rows = slice(c * CHUNK, (c + 1) * CHUNK)
            lanes = slice(p * PAIR, (p + 1) * PAIR)
            t_out[0, c, p] = t_m[i].astype(t_out.dtype)
            g_out[0, c, p] = g_m[i].astype(g_out.dtype)
            rh_out[0, rows, lanes] = _unstack_heads(r_h[i]).astype(rh_out.dtype)
            yh_out[0, rows, lanes] = _unstack_heads(y_h[i]).astype(yh_out.dtype)

    last = slice(n_waves * wave_rows - PREV_ROWS, n_waves * wave_rows)
    for w in range(n_waves):
        nxt = prep(w + 1) if w + 1 < n_waves else prep_first(zr_next, zl_next, f32(zr_ref, last), f32(zl_ref, last))
        _interleave(wave(w), nxt)


def _rwkv_local(z3, p, ts):
    b, s, _ = z3.shape
    n_chunks = ts // CHUNK
    wr, wl = 3 * WIDTH_A, LORA_PAD
    lora_block = wr // wl
    wave_rows = LOCAL_WAVE * CHUNK
    n_waves = ts // wave_rows
    assert n_waves >= 2 and ts % wave_rows == 0
    nxt = lambda c: (lambda bi, i: (bi, jnp.minimum((i + 1) * n_waves, s // wave_rows - 1), c))
    cur = lambda bi, i: (bi, i, 0)
    mat = lambda bi, i: (bi, i, 0, 0, 0)
    const = lambda bi, i: (0, 0)
    vec = pl.BlockSpec((1, WIDTH_A), const)
    seq_spec = pl.BlockSpec((1, ts, WIDTH_A), cur)
    mat_spec = pl.BlockSpec((1, n_chunks, N_PAIRS, PAIR, PAIR), mat)
    seq_shape = lambda dt: jax.ShapeDtypeStruct((b, s, WIDTH_A), dt)
    mat_shape = lambda dt: jax.ShapeDtypeStruct((b, s // CHUNK, N_PAIRS, PAIR, PAIR), dt)
    return pl.pallas_call(
        functools.partial(_rwkv_local_kernel, n_chunks=n_chunks),
        grid=(b, s // ts),
        in_specs=[pl.BlockSpec((1, ts, wr), cur),
                  pl.BlockSpec((1, ts, wl), lambda bi, i: (bi, i, lora_block)),
                  pl.BlockSpec((1, wave_rows, wr), nxt(0)),
                  pl.BlockSpec((1, wave_rows, wl), nxt(lora_block)),
                  pl.BlockSpec((1, wr), const), pl.BlockSpec((1, wl), const),
                  vec, pl.BlockSpec((LORA_W + LORA_A, WIDTH_A), const),
                  vec, pl.BlockSpec((LORA_W + LORA_A, WIDTH_A), const),
                  pl.BlockSpec((LORA_PAD - LORA_W - LORA_A, WIDTH_A), const),
                  vec, vec, vec, pl.BlockSpec((WIDTH_A, WIDTH_A), const)],
        out_specs=[mat_spec, mat_spec, seq_spec, seq_spec, seq_spec, seq_spec],
        out_shape=[mat_shape(BF16)] * 2 + [seq_shape(BF16)] * 4,
        scratch_shapes=[pltpu.VMEM((1, ts, WIDTH_A), F32)] * 7 + [pltpu.VMEM((wave_rows, WIDTH_A), F32)] * 2,
        compiler_params=_cparams(("parallel", "arbitrary")),
        name="rwkv_local",
    )(z3, z3, z3, z3, p["mu_r"], p["mu_l"], p["w0"], p["w_up"], p["a0"], p["a_up"], p["g_up"],
      p["k_k"], p["k_a"], p["r_k"], p["ones"])


def _rwkv_state_kernel(t_ref, gm_ref, rh_ref, yh_ref, g_ref, bonus_ref, lng_ref, lnb_ref, ones_ref,
                       y_out, s_ref, *, n_chunks):
    @pl.when(pl.program_id(1) == 0)
    def _():
        s_ref[...] = jnp.zeros_like(s_ref)

    ones = ones_ref[...]
    state = [s_ref[p] for p in range(N_PAIRS)]
    entry = []
    for c in range(n_chunks):
        entry.append(list(state))
        state = [_mm(state[p], t_ref[0, c, p]) + gm_ref[0, c, p] for p in range(N_PAIRS)]
    for p in range(N_PAIRS):
        s_ref[p] = state[p]
    chunks = range(n_chunks)
    rows = [slice(c * CHUNK, (c + 1) * CHUNK) for c in chunks]
    y = [jnp.concatenate([_mm(rh_ref[0, rows[c], p * PAIR:(p + 1) * PAIR], entry[c][p], _NT)
                          for p in range(N_PAIRS)], axis=1) + yh_ref[0, rows[c], :] for c in chunks]
    mean = [_dot(v.astype(BF16), ones) * (1.0 / HEAD) for v in y]
    yc = [v - m for v, m in zip(y, mean)]
    var = [_dot((v * v).astype(BF16), ones) * (1.0 / HEAD) for v in yc]
    for c in chunks:
        yn = yc[c] * lax.rsqrt(var[c] + GN_EPS) * lng_ref[...] + lnb_ref[...]
        y_out[0, rows[c], :] = ((yn + bonus_ref[0, rows[c], :]) * g_ref[0, rows[c], :]).astype(y_out.dtype)


def _rwkv_state(t, gm, rh, yh, g, bonus, p, ts):
    b, s, _ = rh.shape
    n_chunks = ts // CHUNK
    cur = lambda bi, i: (bi, i, 0)
    mat = lambda bi, i: (bi, i, 0, 0, 0)
    const = lambda bi, i: (0, 0)
    seq_spec = pl.BlockSpec((1, ts, WIDTH_A), cur)
    mat_spec = pl.BlockSpec((1, n_chunks, N_PAIRS, PAIR, PAIR), mat)
    vec = pl.BlockSpec((1, WIDTH_A), const)
    return pl.pallas_call(
        functools.partial(_rwkv_state_kernel, n_chunks=n_chunks),
        grid=(b, s // ts),
        in_specs=[mat_spec, mat_spec, seq_spec, seq_spec, seq_spec, seq_spec, vec, vec,
                  pl.BlockSpec((WIDTH_A, WIDTH_A), const)],
        out_specs=seq_spec,
        out_shape=jax.ShapeDtypeStruct((b, s, WIDTH_A), BF16),
        scratch_shapes=[pltpu.VMEM((N_PAIRS, PAIR, PAIR), F32)],
        compiler_params=_cparams(("parallel", "arbitrary")),
        name="rwkv_state",
    )(t, gm, rh, yh, g, bonus, p["ln_g"], p["ln_b"], p["ones"])


def _rwkv_mixer(z3, p):
    t, gm, rh, yh, g, bonus = _rwkv_local(z3, p, ts=512)
    return _rwkv_state(t, gm, rh, yh, g, bonus, p, ts=512)


def _lru_slab(x, yb, prev, seq_start, h_prev, params, emit):
    cw_ref, cb_ref, wa_ref, ba_ref, wx_ref, bx_ref, lam_ref = params
    n_rows = x.shape[0]
    ext = jnp.concatenate([prev, x], axis=0)
    xc = x * cw_ref[CONV_TAPS - 1:CONV_TAPS, :] + cb_ref[...]
    for back in range(1, CONV_TAPS):
        tap = CONV_TAPS - 1 - back
        xc = xc + pltpu.roll(ext, back, axis=0)[PREV_ROWS:, :] * cw_ref[tap:tap + 1, :]

    half = WIDTH_B // 2
    xcb = xc.astype(BF16)
    ga = jnp.concatenate([_dot(xcb[:, j * half:(j + 1) * half], wa_ref[j]) for j in range(2)], axis=1)
    gx = jnp.concatenate([_dot(xcb[:, j * half:(j + 1) * half], wx_ref[j]) for j in range(2)], axis=1)
    gate_a = _sigmoid(ga + ba_ref[...])
    gate_x = _sigmoid(gx + bx_ref[...])
    log_a = -LRU_C * gate_a * _softplus(-lam_ref[...])
    a = jnp.exp(log_a)
    mult = jnp.sqrt(jnp.maximum(-jnp.tanh(log_a) * (1.0 + a * a), 0.0))
    xg = xc * gate_x
    b = xg * mult
    yield

    row8 = lax.broadcasted_iota(jnp.int32, (8, WIDTH_B), 0)
    h, hs = h_prev, []
    for g in range(n_rows // 8):
        a8, b8 = a[8 * g:8 * g + 8, :], b[8 * g:8 * g + 8, :]
        if g == 0:
            b8 = jnp.where(jnp.logical_and(row8 == 0, seq_start), xg[0:8, :], b8)
        for sh in (1, 2, 4):
            ar = pltpu.roll(a8, sh, axis=0)
            br = pltpu.roll(b8, sh, axis=0)
            m = row8 >= sh
            b8 = jnp.where(m, a8 * br + b8, b8)
            a8 = jnp.where(m, a8 * ar, a8)
        h8 = a8 * h + b8
        hs.append(h8)
        h = h8[7:8, :]
    yield
    gelu = 0.5 * yb * (1.0 + jnp.tanh(math.sqrt(2.0 / math.pi) * (yb + 0.044715 * (yb * yb * yb))))
    emit(x, jnp.concatenate(hs, axis=0) * gelu, h)


def _attn_kernel(q0, q1, k0, k1, kp0, kp1, v0, v1, vp0, vp1, bias_ref, qg_ref, kg_ref, ones_ref,
                 o0, o1, l0, l1, *, dil, n_sub):
    j = pl.program_id(1)
    ones = ones_ref[...]
    lane = lax.broadcasted_iota(jnp.int32, (QBLK, GROUP_W), 1)
    in_head = [(lane >= h * HEAD) & (lane < (h + 1) * HEAD) for h in range(HEADS_PER_GROUP)]
    prev_valid = (lax.broadcasted_iota(jnp.int32, (HEADS_PER_GROUP * QBLK, 2 * QBLK), 1) >= QBLK) | (j > 0)

    def rows(start):
        return pl.ds(start, QBLK, stride=dil) if dil > 1 else pl.ds(start, QBLK)

    def take(lo, hi, start):
        return jnp.concatenate([lo[0, rows(start), :], hi[0, rows(start), :]], axis=1)

    def head_sumsq(x):
        return _dot((x * x).astype(BF16), ones) * (1.0 / HEAD)

    def select_heads(x):
        out = jnp.zeros((QBLK, GROUP_W), F32)
        for h, m in enumerate(in_head):
            out = jnp.where(m, x[h * QBLK:(h + 1) * QBLK, :], out)
        return out

    def wave(units):
        each = lambda f, *ls: [f(*xs) for xs in zip(*ls)]
        span = dil * QBLK
        starts = [s for s, _ in units]
        before = [s - span if dil > 1 or isinstance(s, int) else pl.multiple_of(s - span, QBLK) for s in starts]
        q_raw = [take(q0, q1, s) for s in starts]
        k_raw = [jnp.concatenate([take(kp0, kp1, s) if far else take(k0, k1, p), take(k0, k1, s)], axis=0)
                 for (s, far), p in zip(units, before)]
        vv = [jnp.concatenate([take(vp0, vp1, s) if far else take(v0, v1, p), take(v0, v1, s)],
                              axis=0).astype(BF16) for (s, far), p in zip(units, before)]
        q_ms = each(head_sumsq, q_raw)
        k_ms = each(head_sumsq, k_raw)
        q = each(lambda x, ms: x * lax.rsqrt(ms + RMS_EPS) * qg_ref[...] * (HEAD ** -0.5), q_raw, q_ms)
        kk = each(lambda x, ms: (x * lax.rsqrt(ms + RMS_EPS) * kg_ref[...]).astype(BF16), k_raw, k_ms)
        qs = each(lambda x: jnp.concatenate([jnp.where(m, x, 0.0) for m in in_head], axis=0).astype(BF16), q)
        logits = each(lambda a, b: _dot_nt(a, b) + bias_ref[...], qs, kk)
        logits = [jnp.where(prev_valid, lg, NEG_INF) if far else lg for lg, (_, far) in zip(logits, units)]
        mx = each(lambda lg: jnp.max(lg, axis=-1, keepdims=True), logits)
        pr = each(lambda lg, m: jnp.exp(lg - m), logits, mx)
        den = each(lambda p: jnp.sum(p, axis=-1, keepdims=True), pr)
        pv = each(lambda p, v, dn: _dot(p.astype(BF16), v) / dn, pr, vv, den)
        out = each(select_heads, pv)
        lse = each(lambda m, dn: select_heads(jnp.broadcast_to(m + jnp.log(dn), (HEADS_PER_GROUP * QBLK, GROUP_W))),
                   mx, den)
        for s, o, l in zip(starts, out, lse):
            o0[0, rows(s), :] = o[:, 0:PAIR]
            o1[0, rows(s), :] = o[:, PAIR:GROUP_W]
            l0[0, rows(s), :] = l[:, 0:PAIR]
            l1[0, rows(s), :] = l[:, PAIR:GROUP_W]

    def loop(lo, hi, body):
        def step(i, carry):
            body(i)
            return carry
        lax.fori_loop(lo, hi, step, 0)

    span = dil * QBLK
    if n_sub == 1:
        loop(0, dil // ATTN_WAVE, lambda i: wave([(i * ATTN_WAVE + u, True) for u in range(ATTN_WAVE)]))
    elif dil > 1:
        wave([(r, True) for r in range(dil)])
        loop(1, n_sub, lambda n: wave([(r + n * span, False) for r in range(dil)]))
    else:
        wave([(u * QBLK, u == 0) for u in range(ATTN_WAVE)])
        loop(1, n_sub // ATTN_WAVE,
             lambda i: wave([(pl.multiple_of((i * ATTN_WAVE + u) * QBLK, QBLK), False) for u in range(ATTN_WAVE)]))


def _attn_group(z3, col0, bias, qg, kg, ones, gi, dil):
    b, s, _ = z3.shape
    span = dil * QBLK
    n_sub = ATTN_TILE // span
    halves = GROUP_W // PAIR
    per_part = len(GROUPS) * halves

    def cur(part, half):
        c = col0 + part * per_part + gi * halves + half
        return pl.BlockSpec((1, ATTN_TILE, PAIR), lambda bi, j: (bi, j, c))

    def prev(part, half):
        c = col0 + part * per_part + gi * halves + half
        return pl.BlockSpec((1, span, PAIR), lambda bi, j: (bi, jnp.maximum(j * n_sub - 1, 0), c))

    const2 = lambda bi, j: (0, 0)
    out_spec = pl.BlockSpec((1, ATTN_TILE, PAIR), lambda bi, j: (bi, j, 0))
    return pl.pallas_call(
        functools.partial(_attn_kernel, dil=dil, n_sub=n_sub),
        grid=(b, s // ATTN_TILE),
        in_specs=[cur(0, 0), cur(0, 1), cur(1, 0), cur(1, 1), prev(1, 0), prev(1, 1),
                  cur(2, 0), cur(2, 1), prev(2, 0), prev(2, 1),
                  pl.BlockSpec((HEADS_PER_GROUP * QBLK, 2 * QBLK), const2),
                  pl.BlockSpec((1, GROUP_W), const2), pl.BlockSpec((1, GROUP_W), const2),
                  pl.BlockSpec((GROUP_W, GROUP_W), const2)],
        out_specs=[out_spec] * 4,
        out_shape=[jax.ShapeDtypeStruct((b, s, PAIR), F32)] * 4,
        compiler_params=_cparams(("parallel", "arbitrary")),
        name=f"dilated_attn_g{gi}",
    )(*([z3] * 10), bias, qg, kg, ones)


def _t5_bucket(dist):
    max_exact = N_BUCKETS // 2
    d = jnp.maximum(dist, 0)
    large = max_exact + (jnp.log(jnp.maximum(d, 1).astype(F32) / max_exact)
                         / math.log(MAX_DISTANCE / max_exact) * (N_BUCKETS - max_exact)).astype(jnp.int32)
    large = jnp.minimum(large, N_BUCKETS - 1)
    return jnp.where(d < max_exact, d, large)


def _attn_bias_tiles(rel_bias):
    tiles = []
    kj = jnp.arange(2 * QBLK)[None, :]
    rel = (jnp.arange(QBLK)[:, None] + QBLK) - kj
    for gi, (window, dil) in enumerate(GROUPS):
        band = (rel >= 0) & (rel <= window // dil)
        tab = rel_bias.astype(F32)[:, gi * HEADS_PER_GROUP:(gi + 1) * HEADS_PER_GROUP]
        onehot = (_t5_bucket(rel * dil)[..., None] == jnp.arange(N_BUCKETS)).astype(F32)
        bias = jnp.einsum("qkn,nh->hqk", onehot, tab, precision=lax.Precision.HIGHEST)
        tiles.append(jnp.where(band[None], bias, NEG_INF).reshape(HEADS_PER_GROUP * QBLK, 2 * QBLK))
    return tiles


def _merge_kernel(x_ref, ya_ref, yb_ref, *rest):
    n_g = len(GROUPS)
    attn = rest[:4 * n_g]
    zg_ref, pa_ref, pb_ref, pc_ref, wo_ref, out_ref = rest[4 * n_g:]
    d = x_ref.shape[-1]
    outs = [jnp.concatenate([attn[4 * g][...], attn[4 * g + 1][...]], axis=1) for g in range(n_g)]
    lses = [jnp.concatenate([attn[4 * g + 2][...], attn[4 * g + 3][...]], axis=1) for g in range(n_g)]
    m = functools.reduce(jnp.maximum, lses)
    es = [jnp.exp(l - m) for l in lses]
    yc = sum(o * e for o, e in zip(outs, es)) / sum(es)
    gate = lambda n: _sigmoid(zg_ref[:, n * d:(n + 1) * d].astype(F32))
    merged = (gate(0) * _dot(ya_ref[...].astype(BF16), pa_ref[...])
              + gate(1) * _dot(yb_ref[...].astype(BF16), pb_ref[...])
              + gate(2) * _dot(yc.astype(BF16), pc_ref[...]))
    out_ref[...] = x_ref[...] + _dot(merged.astype(BF16), wo_ref[...])


def _merge(x2d, ya, yb, attn, z2d, gate_block, p, tm):
    m, d = x2d.shape
    row = lambda w: pl.BlockSpec((tm, w), lambda i: (i, 0))
    full = lambda a: pl.BlockSpec(a.shape, lambda i: (0, 0))
    return pl.pallas_call(
        _merge_kernel,
        grid=(m // tm,),
        in_specs=[row(d), row(WIDTH_A), row(WIDTH_B)] + [row(PAIR)] * len(attn)
                 + [pl.BlockSpec((tm, 3 * d), lambda i: (i, gate_block)),
                    full(p["proj_a"]), full(p["proj_b"]), full(p["proj_c"]), full(p["w_out"])],
        out_specs=row(d),
        out_shape=jax.ShapeDtypeStruct((m, d), F32),
        compiler_params=_cparams(("parallel",)),
        name="merge",
    )(x2d, ya, yb, *attn, z2d, p["proj_a"], p["proj_b"], p["proj_c"], p["w_out"])


def _mlp_kernel(x_ref, g_ref, wu_ref, wd_ref, o_ref, h_ref, acc_ref):
    j = pl.program_id(1)

    @pl.when(j == 0)
    def _():
        x = x_ref[...]
        ms = jnp.mean(x * x, axis=-1, keepdims=True)
        h_ref[...] = (x * lax.rsqrt(ms + RMS_EPS) * g_ref[...]).astype(BF16)
        acc_ref[...] = jnp.zeros_like(acc_ref)

    u = jnp.maximum(_dot(h_ref[...], wu_ref[...]), 0.0)
    acc_ref[...] += _dot((u * u).astype(BF16), wd_ref[...])

    @pl.when(j == pl.num_programs(1) - 1)
    def _():
        o_ref[...] = x_ref[...] + acc_ref[...]


def _mlp(x2d, g, wu, wd, tm, tf):
    m, d = x2d.shape
    f = wu.shape[1]
    return pl.pallas_call(
        _mlp_kernel,
        grid=(m // tm, f // tf),
        in_specs=[pl.BlockSpec((tm, d), lambda i, j: (i, 0)),
                  pl.BlockSpec((1, d), lambda i, j: (0, 0)),
                  pl.BlockSpec((d, tf), lambda i, j: (0, j)),
                  pl.BlockSpec((tf, d), lambda i, j: (j, 0))],
        out_specs=pl.BlockSpec((tm, d), lambda i, j: (i, 0)),
        out_shape=jax.ShapeDtypeStruct((m, d), F32),
        scratch_shapes=[pltpu.VMEM((tm, d), BF16), pltpu.VMEM((tm, d), F32)],
        compiler_params=_cparams(("parallel", "arbitrary")),
        name="mlp",
    )(x2d, g, wu, wd)


def _pad_rows(w, lo, total):
    return jnp.pad(w, ((lo, total - lo - w.shape[0]), (0, 0)))


def _block_diag_halves(w):
    n, bd, _ = w.shape
    per = n // 2
    out = jnp.zeros((2, per * bd, per * bd), w.dtype)
    for i in range(n):
        j, q = divmod(i, per)
        out = out.at[j, q * bd:(q + 1) * bd, q * bd:(q + 1) * bd].set(w[i])
    return out.astype(BF16)


def _layer(x, l, bias_tiles, proj, prm):
    (norm_mix_g, _, rwkv_mu, rwkv_w0, rwkv_w_up, rwkv_a0, rwkv_a_up, rwkv_g_up, rwkv_k_k, rwkv_k_a,
     rwkv_r_k, rwkv_ln_g, rwkv_ln_b, proj_a, conv_w, conv_b, lru_wa, lru_ba, lru_wx, lru_bx, lru_lambda,
     proj_b, q_norm_g, k_norm_g, proj_c, w_out, norm_mlp_g, mlp_up, mlp_down) = [t[l] for t in prm]
    b, s, d = x.shape
    x2d = x.reshape(b * s, d)
    row = lambda t: t.reshape(1, -1).astype(F32)

    wz_all, z_b, z_g, z_c = proj
    c_rkv, c_lora = 3 * WIDTH_A, LORA_W + LORA_A + LORA_G
    pb = dict(conv_w=conv_w.astype(F32), conv_b=row(conv_b), wa=_block_diag_halves(lru_wa), ba=row(lru_ba),
              wx=_block_diag_halves(lru_wx), bx=row(lru_bx), lam=row(lru_lambda))
    z2d, zc2d, yb = _projection(x2d, row(norm_mix_g), wz_all, l, pb, z_b, z_c, s, 2048, 768)
    z3 = z2d.reshape(b, s, z_c)
    zc3 = zc2d.reshape(b, s, 3 * WIDTH_C)

    ones_a = _head_ones(WIDTH_A)
    pa = dict(
        mu_r=row(rwkv_mu[0:c_rkv]),
        mu_l=jnp.pad(row(rwkv_mu[c_rkv:]), ((0, 0), (0, LORA_PAD - c_lora))),
        w0=row(rwkv_w0), a0=row(rwkv_a0),
        w_up=_pad_rows(rwkv_w_up, 0, LORA_W + LORA_A).astype(BF16),
        a_up=_pad_rows(rwkv_a_up, LORA_W, LORA_W + LORA_A).astype(BF16),
        g_up=_pad_rows(rwkv_g_up, 0, LORA_PAD - LORA_W - LORA_A).astype(BF16),
        k_k=row(rwkv_k_k), k_a=row(rwkv_k_a), r_k=row(rwkv_r_k),
        ln_g=row(rwkv_ln_g), ln_b=row(rwkv_ln_b), ones=ones_a)
    ya = _rwkv_mixer(z3, pa)

    qg = jnp.tile(row(q_norm_g), (1, HEADS_PER_GROUP))
    kg = jnp.tile(row(k_norm_g), (1, HEADS_PER_GROUP))
    ones_c = _head_ones(GROUP_W)
    attn = []
    for gi, (_, dil) in enumerate(GROUPS):
        parts = _attn_group(zc3, 0, bias_tiles[gi], qg, kg, ones_c, gi, dil)
        attn += [t.reshape(b * s, PAIR) for t in parts]

    pm = dict(proj_a=proj_a.astype(BF16), proj_b=proj_b.astype(BF16), proj_c=proj_c.astype(BF16),
              w_out=w_out.astype(BF16))
    x1 = _merge(x2d, ya.reshape(b * s, WIDTH_A), yb, attn, z2d, z_g // (3 * d), pm, 512)
    x2 = _mlp(x1, row(norm_mlp_g), mlp_up.astype(BF16), mlp_down.astype(BF16), 1024, 1024)
    return x2.reshape(b, s, d)


def kernel(x, rel_bias, norm_mix_g, w_in, rwkv_mu, rwkv_w0, rwkv_w_up, rwkv_a0, rwkv_a_up, rwkv_g_up, rwkv_k_k, rwkv_k_a, rwkv_r_k, rwkv_ln_g, rwkv_ln_b, proj_a, conv_w, conv_b, lru_wa, lru_ba, lru_wx, lru_bx, lru_lambda, proj_b, q_norm_g, k_norm_g, proj_c, w_out, norm_mlp_g, mlp_up, mlp_down):
    prm = (norm_mix_g, w_in, rwkv_mu, rwkv_w0, rwkv_w_up, rwkv_a0, rwkv_a_up, rwkv_g_up, rwkv_k_k, rwkv_k_a,
           rwkv_r_k, rwkv_ln_g, rwkv_ln_b, proj_a, conv_w, conv_b, lru_wa, lru_ba, lru_wx, lru_bx, lru_lambda,
           proj_b, q_norm_g, k_norm_g, proj_c, w_out, norm_mlp_g, mlp_up, mlp_down)
    bias_tiles = _attn_bias_tiles(rel_bias)
    proj = _projection_weights(jnp.swapaxes(w_in.astype(F32), 1, 2))
    x = x.astype(F32)
    for l in range(norm_mix_g.shape[0]):
        x = _layer(x, l, bias_tiles, proj, prm)
    return x
```

```python
import functools
import math

import jax
import jax.numpy as jnp
from jax import lax
from jax.experimental import pallas as pl
from jax.experimental.pallas import tpu as pltpu

F32 = jnp.float32
BF16 = jnp.bfloat16

N_HEADS_A = 8
HEAD = 64
PAIR = 2 * HEAD
WIDTH_A = N_HEADS_A * HEAD
N_PAIRS = WIDTH_A // PAIR
CHUNK = 64
LOCAL_WAVE = 4
PREV_ROWS = 16
LRU_SLAB = 256
MXU_N = 256
LORA_W, LORA_A, LORA_G = 64, 64, 160
LORA_PAD = 384
GN_EPS = 64e-5
WIDTH_B = 512
LRU_BLOCK = 64
CONV_TAPS = 4
LRU_C = 8.0
GROUPS = ((128, 1), (512, 4), (2048, 16))
HEADS_PER_GROUP = 4
GROUP_W = HEADS_PER_GROUP * HEAD
WIDTH_C = len(GROUPS) * GROUP_W
QBLK = 128
ATTN_TILE = 2048
ATTN_WAVE = 4
N_BUCKETS = 32
MAX_DISTANCE = 2048
NEG_INF = -1e30
RMS_EPS = 1e-6
VMEM_LIMIT = 56 * 1024 * 1024


def _cparams(sem):
    return pltpu.CompilerParams(dimension_semantics=sem, vmem_limit_bytes=VMEM_LIMIT)


def _dot(a, b):
    return jnp.dot(a, b, preferred_element_type=F32)


def _dot_nt(a, b):
    return lax.dot_general(a, b, (((1,), (1,)), ((), ())), preferred_element_type=F32)


_NN = (((1,), (0,)), ((), ()))
_NT = (((1,), (1,)), ((), ()))
_TN = (((0,), (0,)), ((), ()))


def _mm(a, b, dims=_NN):
    return lax.dot_general(a.astype(BF16), b.astype(BF16), dims, preferred_element_type=F32)


def _sigmoid(x):
    return 1.0 / (1.0 + jnp.exp(-x))


def _softplus(x):
    return jnp.maximum(x, 0.0) + jnp.log1p(jnp.exp(-jnp.abs(x)))


def _head_ones(width):
    i = jnp.arange(width) // HEAD
    return (i[:, None] == i[None, :]).astype(BF16)


def _projection_kernel(x_ref, g_ref, w_ref, *rest, n_lo, copies, first_slab, tiles_per_seq):
    lru, (lo_ref, hi_ref, yb_ref, h_ref, zb_ref, hist_ref, carry_ref) = rest[:7], rest[7:]
    i, j = pl.program_id(0), pl.program_id(1)

    @pl.when(j == 0)
    def _():
        x = x_ref[...]
        ms = jnp.mean(x * x, axis=-1, keepdims=True)
        h_ref[...] = (x * lax.rsqrt(ms + RMS_EPS) * g_ref[...]).astype(BF16)

    s = j - first_slab
    rows = pl.ds(pl.multiple_of(s * LRU_SLAB, LRU_SLAB), LRU_SLAB)
    seq_start = jnp.logical_and(s == 0, lax.rem(i, tiles_per_seq) == 0)

    def slab_load():
        return (zb_ref[rows, 0:WIDTH_B].astype(F32), zb_ref[rows, WIDTH_B:2 * WIDTH_B].astype(F32),
                jnp.where(seq_start, 0.0, hist_ref[...]), jnp.where(seq_start, 0.0, carry_ref[0:1, :]))

    def slab_store(x, out, h_last):
        yb_ref[rows, :] = out.astype(yb_ref.dtype)
        hist_ref[...] = x[LRU_SLAB - PREV_ROWS:, :]
        carry_ref[0:1, :] = h_last

    def matmul(out_ref, copy):
        h = h_ref[...]
        for c0 in range(0, out_ref.shape[1], MXU_N):
            z = _dot_nt(h, w_ref[c0:c0 + MXU_N, :])
            out_ref[:, c0:c0 + MXU_N] = z.astype(out_ref.dtype)
            if copy is not None:
                src, dst, n = copy
                lo_c, hi_c = max(src, c0), min(src + n, c0 + MXU_N)
                if lo_c < hi_c:
                    zb_ref[:, dst + lo_c - src:dst + hi_c - src] = z[:, lo_c - c0:hi_c - c0].astype(zb_ref.dtype)
            yield

    def step(out_ref, copy, with_slab):
        if not with_slab:
            _interleave(matmul(out_ref, copy))
        elif copy is None:
            x, yb, prev, h_prev = slab_load()
            _interleave(matmul(out_ref, None), _lru_slab(x, yb, prev, seq_start, h_prev, lru, slab_store))
        else:
            _interleave(matmul(out_ref, copy))
            x, yb, prev, h_prev = slab_load()
            _interleave(_lru_slab(x, yb, prev, seq_start, h_prev, lru, slab_store))

    plain = j < first_slab
    for jb in copies:
        plain = jnp.logical_and(plain, j != jb)
        pl.when(j == jb)(functools.partial(step, lo_ref, copies[jb], jb == first_slab))
    pl.when(plain)(functools.partial(step, lo_ref, None, False))
    pl.when(jnp.logical_and(j > first_slab, j < n_lo))(functools.partial(step, lo_ref, None, True))
    pl.when(j >= n_lo)(functools.partial(step, hi_ref, None, True))


def _projection(x2d, g, w_all, layer, lru, z_b, n_lo_cols, seq_len, tm, tn):
    m, d = x2d.shape
    n = w_all.shape[1]
    n_lo, n_blocks = n_lo_cols // tn, n // tn
    copies = {}
    for jb in range(n_blocks):
        lo_c, hi_c = max(jb * tn, z_b), min((jb + 1) * tn, z_b + 2 * WIDTH_B)
        if lo_c < hi_c:
            copies[jb] = (lo_c - jb * tn, lo_c - z_b, hi_c - lo_c)
    first_slab = max(copies)
    assert first_slab < n_lo and n_blocks - first_slab == tm // LRU_SLAB and seq_len % tm == 0
    const2 = lambda i, j: (0, 0)
    vec = pl.BlockSpec((1, WIDTH_B), const2)
    half = WIDTH_B // 2
    mat = pl.BlockSpec((2, half, half), lambda i, j: (0, 0, 0))
    return pl.pallas_call(
        functools.partial(_projection_kernel, n_lo=n_lo, copies=copies, first_slab=first_slab,
                          tiles_per_seq=seq_len // tm),
        grid=(m // tm, n_blocks),
        in_specs=[pl.BlockSpec((tm, d), lambda i, j: (i, 0)),
                  pl.BlockSpec((1, d), const2),
                  pl.BlockSpec((None, tn, d), lambda i, j: (layer, j, 0)),
                  pl.BlockSpec((CONV_TAPS, WIDTH_B), const2), vec, mat, vec, mat, vec, vec],
        out_specs=[pl.BlockSpec((tm, tn), lambda i, j: (i, jnp.minimum(j, n_lo - 1))),
                   pl.BlockSpec((tm, tn), lambda i, j: (i, jnp.maximum(j - n_lo, 0))),
                   pl.BlockSpec((tm, WIDTH_B), lambda i, j: (i, 0))],
        out_shape=[jax.ShapeDtypeStruct((m, n_lo_cols), BF16),
                   jax.ShapeDtypeStruct((m, n - n_lo_cols), F32),
                   jax.ShapeDtypeStruct((m, WIDTH_B), BF16)],
        scratch_shapes=[pltpu.VMEM((tm, d), BF16), pltpu.VMEM((tm, 2 * WIDTH_B), BF16),
                        pltpu.VMEM((PREV_ROWS, WIDTH_B), F32), pltpu.VMEM((8, WIDTH_B), F32)],
        compiler_params=_cparams(("arbitrary", "arbitrary")),
        name="projection",
    )(x2d, g, w_all, lru["conv_w"], lru["conv_b"], lru["wa"], lru["ba"], lru["wx"], lru["bx"], lru["lam"])


def _relayout_kernel(w_ref, o_ref, *, moves, width):
    cols = w_ref.shape[2]
    end = 0
    for src, dst, n in moves:
        if dst > end:
            o_ref[0, end:dst, :] = jnp.zeros((dst - end, cols), o_ref.dtype)
        o_ref[0, dst:dst + n, :] = w_ref[0, src:src + n, :].astype(o_ref.dtype)
        end = dst + n
    if end < width:
        o_ref[0, end:width, :] = jnp.zeros((width - end, cols), o_ref.dtype)


def _projection_weights(w_t, tc=256):
    n_layers, n_in, d = w_t.shape
    c_rkv, c_lora = 3 * WIDTH_A, LORA_W + LORA_A + LORA_G
    o_b = c_rkv + c_lora
    o_c = o_b + 2 * WIDTH_B
    o_g = o_c + 3 * WIDTH_C
    z_b = -(-(c_rkv + LORA_PAD) // WIDTH_B) * WIDTH_B
    z_g = -(-(z_b + 2 * WIDTH_B) // (3 * d)) * (3 * d)
    z_c = z_g + 3 * d
    width = z_c + 3 * WIDTH_C
    moves = ((0, 0, o_b), (o_b, z_b, o_c - o_b), (o_g, z_g, n_in - o_g), (o_c, z_c, o_g - o_c))
    wz = pl.pallas_call(
        functools.partial(_relayout_kernel, moves=moves, width=width),
        grid=(n_layers, d // tc),
        in_specs=[pl.BlockSpec((1, n_in, tc), lambda l, i: (l, 0, i))],
        out_specs=pl.BlockSpec((1, width, tc), lambda l, i: (l, 0, i)),
        out_shape=jax.ShapeDtypeStruct((n_layers, width, d), BF16),
        compiler_params=_cparams(("parallel", "parallel")),
        name="projection_weights",
    )(w_t)
    return wz, z_b, z_g, z_c


def _shift_rows(cur, prev):
    prev_row = prev[PREV_ROWS - 1:PREV_ROWS, :]
    rolled = pltpu.roll(cur, 1, axis=0)
    row = lax.broadcasted_iota(jnp.int32, cur.shape, 0)
    return jnp.where(row == 0, prev_row, rolled)


def _interleave(*stages):
    live = list(stages)
    while live:
        for gen in list(live):
            try:
                next(gen)
            except StopIteration:
                live.remove(gen)


def _rwkv_prep(zr, zl, prev_r, prev_l, rows, params, scan, put_gate, put_bonus):
    mur_ref, mul_ref, w0_ref, wup_ref, a0_ref, aup_ref, gup_ref, kk_ref, ka_ref, rk_ref, ones_ref = params
    r_out, k_out, v_out, lg_out, lgp_out, as_out, bs_out = scan
    fr = zr + (_shift_rows(zr, prev_r) - zr) * mur_ref[...]
    fl = zl + (_shift_rows(zl, prev_l) - zl) * mul_ref[...]
    r = fr[:, 0:WIDTH_A]
    k = fr[:, WIDTH_A:2 * WIDTH_A]
    v = fr[:, 2 * WIDTH_A:3 * WIDTH_A]
    x_wa = fl[:, 0:LORA_W + LORA_A]
    x_g = fl[:, LORA_W + LORA_A:LORA_PAD]
    ones = ones_ref[...]
    r_out[0, rows, :] = r
    v_out[0, rows, :] = v
    yield

    w = -_softplus(-(w0_ref[...] + _dot(jnp.tanh(x_wa).astype(BF16), wup_ref[...]))) - 0.5
    lw = -jnp.exp(w)
    pos = lax.broadcasted_iota(jnp.int32, lw.shape, 0) & (CHUNK - 1)
    lg = lw
    for sh in [1 << i for i in range(int(math.log2(CHUNK)))]:
        lg = lg + jnp.where(pos >= sh, pltpu.roll(lg, sh, axis=0), 0.0)
    lg_out[0, rows, :] = lg
    lgp_out[0, rows, :] = lg - lw
    yield

    a = _sigmoid(a0_ref[...] + _dot(x_wa.astype(BF16), aup_ref[...]))
    k2 = k * (1.0 + (a - 1.0) * ka_ref[...])
    k_out[0, rows, :] = k2
    yield

    kk = k * kk_ref[...]
    kk = kk / jnp.maximum(jnp.sqrt(_dot((kk * kk).astype(BF16), ones)), 1e-12)
    as_out[0, rows, :] = -kk
    bs_out[0, rows, :] = kk * a
    yield

    put_gate(_dot(_sigmoid(x_g).astype(BF16), gup_ref[...]))
    yield

    put_bonus(_dot((r * k2 * rk_ref[...]).astype(BF16), ones) * v)


def _stack_heads(x):
    lo = lax.broadcasted_iota(jnp.int32, x.shape, 1) < HEAD
    return jnp.concatenate([jnp.where(lo, x, 0.0), jnp.where(lo, 0.0, x)], axis=0)


def _unstack_heads(x):
    return x[0:CHUNK, :] + x[CHUNK:2 * CHUNK, :]


def _rwkv_local_kernel(*refs, n_chunks):
    (zr_ref, zl_ref, zr_next, zl_next), params = refs[:4], refs[4:15]
    t_out, g_out, rh_out, yh_out, gate_out, bonus_out = refs[15:21]
    scan, (gate_carry, bonus_carry) = refs[21:28], refs[28:]
    r_ref, k_ref, v_ref, lg_ref, lgp_ref, as_ref, bs_ref = scan
    two_c = 2 * CHUNK
    row = lax.broadcasted_iota(jnp.int32, (two_c, two_c), 0)
    col = lax.broadcasted_iota(jnp.int32, (two_c, two_c), 1)
    strict = col < row
    incl = col <= row
    eye = (col == row).astype(F32)
    wave_rows = LOCAL_WAVE * CHUNK
    n_waves = n_chunks // LOCAL_WAVE
    first_rows = slice(0, wave_rows)
    f32 = lambda ref, rows: ref[0, rows, :].astype(F32)

    def put(ref, rows):
        def store(val):
            ref[rows] = val
        return store

    def prep(w):
        rows = slice(w * wave_rows, (w + 1) * wave_rows)
        before = slice(w * wave_rows - PREV_ROWS, w * wave_rows)
        return _rwkv_prep(f32(zr_ref, rows), f32(zl_ref, rows), f32(zr_ref, before), f32(zl_ref, before), rows,
                          params, scan, put(gate_out, (0, rows)), put(bonus_out, (0, rows)))

    def prep_first(zr_src, zl_src, prev_r, prev_l):
        return _rwkv_prep(f32(zr_src, first_rows), f32(zl_src, first_rows), prev_r, prev_l, first_rows,
                          params, scan, put(gate_carry, slice(None)), put(bonus_carry, slice(None)))

    @pl.when(pl.program_id(1) == 0)
    def _():
        zero = lambda ref: jnp.zeros((PREV_ROWS, ref.shape[-1]), F32)
        _interleave(prep_first(zr_ref, zl_ref, zero(zr_ref), zero(zl_ref)))

    gate_out[0, first_rows, :] = gate_carry[...]
    bonus_out[0, first_rows, :] = bonus_carry[...]

    def wave(w):
        c0 = w * LOCAL_WAVE
        units = [(c, p) for c in range(c0, min(c0 + LOCAL_WAVE, n_chunks)) for p in range(N_PAIRS)]
        at = lambda ref: [ref[0, c * CHUNK:(c + 1) * CHUNK, p * PAIR:(p + 1) * PAIR] for c, p in units]
        each = lambda f, *ls: [f(*xs) for xs in zip(*ls)]
        lg = at(lg_ref)
        lg_end = each(lambda x: x[CHUNK - 1:CHUNK, :], lg)
        e_neg = each(lambda x: jnp.exp(-x), lg)
        e_end = each(lambda x, xe: jnp.exp(xe - x), lg, lg_end)
        a_s, b_s, kk = at(as_ref), at(bs_ref), at(k_ref)
        a_t = each(lambda x, gp: _stack_heads(x * jnp.exp(gp)), a_s, at(lgp_ref))
        r_t = each(lambda x, g: _stack_heads(x * jnp.exp(g)), at(r_ref), lg)
        b_t = each(lambda x, e: _stack_heads(x * e), b_s, e_neg)
        k_t = each(lambda x, e: _stack_heads(x * e), kk, e_neg)
        b_p = each(lambda x, e: _stack_heads(x * e), b_s, e_end)
        k_p = each(lambda x, e: _stack_heads(x * e), kk, e_end)
        v_s = each(_stack_heads, at(v_ref))
        yield

        vcat = lambda x, y: jnp.concatenate([x, y], axis=0)
        hcat = lambda x, y: jnp.concatenate([x, y], axis=1)
        top = lambda x: x[0:two_c]
        bot = lambda x: x[two_c:2 * two_c]
        left = lambda x: x[:, 0:two_c]
        right = lambda x: x[:, two_c:2 * two_c]

        prod = each(lambda a, r, b, k: _mm(vcat(a, r), vcat(b, k), _NT), a_t, r_t, b_t, k_t)
        l_ab = each(lambda x: jnp.where(strict, left(top(x)), 0.0), prod)
        l_ak = each(lambda x: jnp.where(strict, right(top(x)), 0.0), prod)
        l_rb = each(lambda x: jnp.where(incl, left(bot(x)), 0.0), prod)
        l_rk = each(lambda x: jnp.where(incl, right(bot(x)), 0.0), prod)
        yield

        inv = each(lambda x: eye + x, l_ab)
        pw = each(lambda x: _mm(x, x), l_ab)
        yield
        for _ in range(int(math.log2(CHUNK)) - 2):
            both = each(lambda p, x: _mm(vcat(p, x), p), pw, inv)
            pw = each(top, both)
            inv = each(lambda x, y: x + bot(y), inv, both)
            yield
        inv = each(lambda x, p: x + _mm(x, p), inv, pw)
        yield

        lv_rkv = each(lambda l1, l2, v: _mm(vcat(l1, l2), v), l_ak, l_rk, v_s)
        yield
        aw_h = each(lambda m, a, x: _mm(m, hcat(a, top(x))), inv, a_t, lv_rkv)
        yield
        l_aw = each(_mm, l_rb, aw_h)
        r_h = each(lambda x, y: x + left(y), r_t, l_aw)
        y_h = each(lambda y, x: right(y) + bot(x), l_aw, lv_rkv)
        yield
        tg = each(lambda x, bp: _mm(x, bp, _TN), aw_h, b_p)
        t_m = each(lambda ge, x: eye * jnp.exp(ge) + top(x), lg_end, tg)
        yield
        g_m = each(lambda x, v, kp: bot(x) + _mm(v, kp, _TN), tg, v_s, k_p)
        for i, (c, p) in enumerate(units):
            rows = slice(c * CHUNK, (c + 1) * CHUNK)
            lanes = slice(p * PAIR, (p + 1) * PAIR)
            t_out[0, c, p] = t_m[i].astype(t_out.dtype)
            g_out[0, c, p] = g_m[i]
            rh_out[0, rows, lanes] = _unstack_heads(r_h[i]).astype(rh_out.dtype)
            yh_out[0, rows, lanes] = _unstack_heads(y_h[i])

    last = slice(n_waves * wave_rows - PREV_ROWS, n_waves * wave_rows)
    for w in range(n_waves):
        nxt = prep(w + 1) if w + 1 < n_waves else prep_first(zr_next, zl_next, f32(zr_ref, last), f32(zl_ref, last))
        _interleave(wave(w), nxt)


def _rwkv_local(z3, p, ts):
    b, s, _ = z3.shape
    n_chunks = ts // CHUNK
    wr, wl = 3 * WIDTH_A, LORA_PAD
    lora_block = wr // wl
    wave_rows = LOCAL_WAVE * CHUNK
    n_waves = ts // wave_rows
    assert n_waves >= 2 and ts % wave_rows == 0
    nxt = lambda c: (lambda bi, i: (bi, jnp.minimum((i + 1) * n_waves, s // wave_rows - 1), c))
    cur = lambda bi, i: (bi, i, 0)
    mat = lambda bi, i: (bi, i, 0, 0, 0)
    const = lambda bi, i: (0, 0)
    vec = pl.BlockSpec((1, WIDTH_A), const)
    seq_spec = pl.BlockSpec((1, ts, WIDTH_A), cur)
    mat_spec = pl.BlockSpec((1, n_chunks, N_PAIRS, PAIR, PAIR), mat)
    seq_shape = lambda dt: jax.ShapeDtypeStruct((b, s, WIDTH_A), dt)
    mat_shape = lambda dt: jax.ShapeDtypeStruct((b, s // CHUNK, N_PAIRS, PAIR, PAIR), dt)
    return pl.pallas_call(
        functools.partial(_rwkv_local_kernel, n_chunks=n_chunks),
        grid=(b, s // ts),
        in_specs=[pl.BlockSpec((1, ts, wr), cur),
                  pl.BlockSpec((1, ts, wl), lambda bi, i: (bi, i, lora_block)),
                  pl.BlockSpec((1, wave_rows, wr), nxt(0)),
                  pl.BlockSpec((1, wave_rows, wl), nxt(lora_block)),
                  pl.BlockSpec((1, wr), const), pl.BlockSpec((1, wl), const),
                  vec, pl.BlockSpec((LORA_W + LORA_A, WIDTH_A), const),
                  vec, pl.BlockSpec((LORA_W + LORA_A, WIDTH_A), const),
                  pl.BlockSpec((LORA_PAD - LORA_W - LORA_A, WIDTH_A), const),
                  vec, vec, vec, pl.BlockSpec((WIDTH_A, WIDTH_A), const)],
        out_specs=[mat_spec, mat_spec, seq_spec, seq_spec, seq_spec, seq_spec],
        out_shape=[mat_shape(BF16), mat_shape(F32), seq_shape(BF16), seq_shape(F32), seq_shape(F32), seq_shape(F32)],
        scratch_shapes=[pltpu.VMEM((1, ts, WIDTH_A), F32)] * 7 + [pltpu.VMEM((wave_rows, WIDTH_A), F32)] * 2,
        compiler_params=_cparams(("parallel", "arbitrary")),
        name="rwkv_local",
    )(z3, z3, z3, z3, p["mu_r"], p["mu_l"], p["w0"], p["w_up"], p["a0"], p["a_up"], p["g_up"],
      p["k_k"], p["k_a"], p["r_k"], p["ones"])


def _rwkv_state_kernel(t_ref, gm_ref, rh_ref, yh_ref, g_ref, bonus_ref, lng_ref, lnb_ref, ones_ref,
                       y_out, s_ref, *, n_chunks):
    n_seq = s_ref.shape[0]

    @pl.when(pl.program_id(0) == 0)
    def _():
        s_ref[...] = jnp.zeros_like(s_ref)

    ones = ones_ref[...]
    units = [(b, p) for b in range(n_seq) for p in range(N_PAIRS)]
    rows = [slice(c * CHUNK, (c + 1) * CHUNK) for c in range(n_chunks)]
    state = {u: s_ref[u] for u in units}
    y = []
    for c in range(n_chunks):
        entry = dict(state)
        state = {(b, p): _mm(entry[b, p], t_ref[b, c, p]) + gm_ref[b, c, p] for b, p in units}
        y.append([jnp.concatenate([_mm(rh_ref[b, rows[c], p * PAIR:(p + 1) * PAIR], entry[b, p], _NT)
                                   for p in range(N_PAIRS)], axis=1) + yh_ref[b, rows[c], :]
                  for b in range(n_seq)])
    for u in units:
        s_ref[u] = state[u]
    flat = [(c, b) for c in range(n_chunks) for b in range(n_seq)]
    yv = [y[c][b] for c, b in flat]
    mean = [_dot(v.astype(BF16), ones) * (1.0 / HEAD) for v in yv]
    yc = [v - m for v, m in zip(yv, mean)]
    var = [_dot((v * v).astype(BF16), ones) * (1.0 / HEAD) for v in yc]
    for (c, b), v, vr in zip(flat, yc, var):
        yn = v * lax.rsqrt(vr + GN_EPS) * lng_ref[...] + lnb_ref[...]
        y_out[b, rows[c], :] = ((yn + bonus_ref[b, rows[c], :]) * g_ref[b, rows[c], :]).astype(y_out.dtype)


def _rwkv_state(t, gm, rh, yh, g, bonus, p, ts):
    b, s, _ = rh.shape
    n_chunks = ts // CHUNK
    cur = lambda i: (0, i, 0)
    mat = lambda i: (0, i, 0, 0, 0)
    const = lambda i: (0, 0)
    seq_spec = pl.BlockSpec((b, ts, WIDTH_A), cur)
    mat_spec = pl.BlockSpec((b, n_chunks, N_PAIRS, PAIR, PAIR), mat)
    vec = pl.BlockSpec((1, WIDTH_A), const)
    return pl.pallas_call(
        functools.partial(_rwkv_state_kernel, n_chunks=n_chunks),
        grid=(s // ts,),
        in_specs=[mat_spec, mat_spec, seq_spec, seq_spec, seq_spec, seq_spec, vec, vec,
                  pl.BlockSpec((WIDTH_A, WIDTH_A), const)],
        out_specs=seq_spec,
        out_shape=jax.ShapeDtypeStruct((b, s, WIDTH_A), BF16),
        scratch_shapes=[pltpu.VMEM((b, N_PAIRS, PAIR, PAIR), F32)],
        compiler_params=_cparams(("arbitrary",)),
        name="rwkv_state",
    )(t, gm, rh, yh, g, bonus, p["ln_g"], p["ln_b"], p["ones"])


def _rwkv_mixer(z3, p):
    t, gm, rh, yh, g, bonus = _rwkv_local(z3, p, ts=512)
    return _rwkv_state(t, gm, rh, yh, g, bonus, p, ts=512)


def _lru_slab(x, yb, prev, seq_start, h_prev, params, emit):
    cw_ref, cb_ref, wa_ref, ba_ref, wx_ref, bx_ref, lam_ref = params
    n_rows = x.shape[0]
    ext = jnp.concatenate([prev, x], axis=0)
    xc = x * cw_ref[CONV_TAPS - 1:CONV_TAPS, :] + cb_ref[...]
    for back in range(1, CONV_TAPS):
        tap = CONV_TAPS - 1 - back
        xc = xc + pltpu.roll(ext, back, axis=0)[PREV_ROWS:, :] * cw_ref[tap:tap + 1, :]

    half = WIDTH_B // 2
    xcb = xc.astype(BF16)
    ga = jnp.concatenate([_dot(xcb[:, j * half:(j + 1) * half], wa_ref[j]) for j in range(2)], axis=1)
    gx = jnp.concatenate([_dot(xcb[:, j * half:(j + 1) * half], wx_ref[j]) for j in range(2)], axis=1)
    gate_a = _sigmoid(ga + ba_ref[...])
    gate_x = _sigmoid(gx + bx_ref[...])
    log_a = -LRU_C * gate_a * _softplus(-lam_ref[...])
    a = jnp.exp(log_a)
    mult = jnp.sqrt(jnp.maximum(-jnp.tanh(log_a) * (1.0 + a * a), 0.0))
    xg = xc * gate_x
    b = xg * mult
    yield

    row8 = lax.broadcasted_iota(jnp.int32, (8, WIDTH_B), 0)
    h, hs = h_prev, []
    for g in range(n_rows // 8):
        a8, b8 = a[8 * g:8 * g + 8, :], b[8 * g:8 * g + 8, :]
        if g == 0:
            b8 = jnp.where(jnp.logical_and(row8 == 0, seq_start), xg[0:8, :], b8)
        for sh in (1, 2, 4):
            ar = pltpu.roll(a8, sh, axis=0)
            br = pltpu.roll(b8, sh, axis=0)
            m = row8 >= sh
            b8 = jnp.where(m, a8 * br + b8, b8)
            a8 = jnp.where(m, a8 * ar, a8)
        h8 = a8 * h + b8
        hs.append(h8)
        h = h8[7:8, :]
    yield
    gelu = 0.5 * yb * (1.0 + jnp.tanh(math.sqrt(2.0 / math.pi) * (yb + 0.044715 * (yb * yb * yb))))
    emit(x, jnp.concatenate(hs, axis=0) * gelu, h)


def _attn_kernel(q0, q1, k0, k1, kp0, kp1, v0, v1, vp0, vp1, bias_ref, qg_ref, kg_ref, ones_ref,
                 o0, o1, l0, l1, *, dil, n_sub):
    j = pl.program_id(1)
    ones = ones_ref[...]
    lane = lax.broadcasted_iota(jnp.int32, (QBLK, GROUP_W), 1)
    in_head = [(lane >= h * HEAD) & (lane < (h + 1) * HEAD) for h in range(HEADS_PER_GROUP)]
    prev_valid = (lax.broadcasted_iota(jnp.int32, (HEADS_PER_GROUP * QBLK, 2 * QBLK), 1) >= QBLK) | (j > 0)

    def rows(start):
        return pl.ds(start, QBLK, stride=dil) if dil > 1 else pl.ds(start, QBLK)

    def take(lo, hi, start):
        return jnp.concatenate([lo[0, rows(start), :], hi[0, rows(start), :]], axis=1)

    def head_sumsq(x):
        return _dot((x * x).astype(BF16), ones) * (1.0 / HEAD)

    def select_heads(x):
        out = jnp.zeros((QBLK, GROUP_W), F32)
        for h, m in enumerate(in_head):
            out = jnp.where(m, x[h * QBLK:(h + 1) * QBLK, :], out)
        return out

    def wave(units):
        each = lambda f, *ls: [f(*xs) for xs in zip(*ls)]
        span = dil * QBLK
        starts = [s for s, _ in units]
        before = [s - span if dil > 1 or isinstance(s, int) else pl.multiple_of(s - span, QBLK) for s in starts]
        q_raw = [take(q0, q1, s) for s in starts]
        k_raw = [jnp.concatenate([take(kp0, kp1, s) if far else take(k0, k1, p), take(k0, k1, s)], axis=0)
                 for (s, far), p in zip(units, before)]
        vv = [jnp.concatenate([take(vp0, vp1, s) if far else take(v0, v1, p), take(v0, v1, s)],
                              axis=0).astype(BF16) for (s, far), p in zip(units, before)]
        q_ms = each(head_sumsq, q_raw)
        k_ms = each(head_sumsq, k_raw)
        q = each(lambda x, ms: x * lax.rsqrt(ms + RMS_EPS) * qg_ref[...] * (HEAD ** -0.5), q_raw, q_ms)
        kk = each(lambda x, ms: (x * lax.rsqrt(ms + RMS_EPS) * kg_ref[...]).astype(BF16), k_raw, k_ms)
        qs = each(lambda x: jnp.concatenate([jnp.where(m, x, 0.0) for m in in_head], axis=0).astype(BF16), q)
        logits = each(lambda a, b: _dot_nt(a, b) + bias_ref[...], qs, kk)
        logits = [jnp.where(prev_valid, lg, NEG_INF) if far else lg for lg, (_, far) in zip(logits, units)]
        mx = each(lambda lg: jnp.max(lg, axis=-1, keepdims=True), logits)
        pr = each(lambda lg, m: jnp.exp(lg - m), logits, mx)
        den = each(lambda p: jnp.sum(p, axis=-1, keepdims=True), pr)
        pv = each(lambda p, v, dn: _dot(p.astype(BF16), v) / dn, pr, vv, den)
        out = each(select_heads, pv)
        lse = each(lambda m, dn: select_heads(jnp.broadcast_to(m + jnp.log(dn), (HEADS_PER_GROUP * QBLK, GROUP_W))),
                   mx, den)
        for s, o, l in zip(starts, out, lse):
            o0[0, rows(s), :] = o[:, 0:PAIR]
            o1[0, rows(s), :] = o[:, PAIR:GROUP_W]
            l0[0, rows(s), :] = l[:, 0:PAIR]
            l1[0, rows(s), :] = l[:, PAIR:GROUP_W]

    def loop(lo, hi, body):
        def step(i, carry):
            body(i)
            return carry
        lax.fori_loop(lo, hi, step, 0)

    span = dil * QBLK
    if n_sub == 1:
        loop(0, dil // ATTN_WAVE, lambda i: wave([(i * ATTN_WAVE + u, True) for u in range(ATTN_WAVE)]))
    elif dil > 1:
        wave([(r, True) for r in range(dil)])
        loop(1, n_sub, lambda n: wave([(r + n * span, False) for r in range(dil)]))
    else:
        wave([(u * QBLK, u == 0) for u in range(ATTN_WAVE)])
        loop(1, n_sub // ATTN_WAVE,
             lambda i: wave([(pl.multiple_of((i * ATTN_WAVE + u) * QBLK, QBLK), False) for u in range(ATTN_WAVE)]))


def _attn_group(z3, col0, bias, qg, kg, ones, gi, dil):
    b, s, _ = z3.shape
    span = dil * QBLK
    n_sub = ATTN_TILE // span
    halves = GROUP_W // PAIR
    per_part = len(GROUPS) * halves

    def cur(part, half):
        c = col0 + part * per_part + gi * halves + half
        return pl.BlockSpec((1, ATTN_TILE, PAIR), lambda bi, j: (bi, j, c))

    def prev(part, half):
        c = col0 + part * per_part + gi * halves + half
        return pl.BlockSpec((1, span, PAIR), lambda bi, j: (bi, jnp.maximum(j * n_sub - 1, 0), c))

    const2 = lambda bi, j: (0, 0)
    out_spec = pl.BlockSpec((1, ATTN_TILE, PAIR), lambda bi, j: (bi, j, 0))
    return pl.pallas_call(
        functools.partial(_attn_kernel, dil=dil, n_sub=n_sub),
        grid=(b, s // ATTN_TILE),
        in_specs=[cur(0, 0), cur(0, 1), cur(1, 0), cur(1, 1), prev(1, 0), prev(1, 1),
                  cur(2, 0), cur(2, 1), prev(2, 0), prev(2, 1),
                  pl.BlockSpec((HEADS_PER_GROUP * QBLK, 2 * QBLK), const2),
                  pl.BlockSpec((1, GROUP_W), const2), pl.BlockSpec((1, GROUP_W), const2),
                  pl.BlockSpec((GROUP_W, GROUP_W), const2)],
        out_specs=[out_spec] * 4,
        out_shape=[jax.ShapeDtypeStruct((b, s, PAIR), F32)] * 4,
        compiler_params=_cparams(("parallel", "arbitrary")),
        name=f"dilated_attn_g{gi}",
    )(*([z3] * 10), bias, qg, kg, ones)


def _t5_bucket(dist):
    max_exact = N_BUCKETS // 2
    d = jnp.maximum(dist, 0)
    large = max_exact + (jnp.log(jnp.maximum(d, 1).astype(F32) / max_exact)
                         / math.log(MAX_DISTANCE / max_exact) * (N_BUCKETS - max_exact)).astype(jnp.int32)
    large = jnp.minimum(large, N_BUCKETS - 1)
    return jnp.where(d < max_exact, d, large)


def _attn_bias_tiles(rel_bias):
    tiles = []
    kj = jnp.arange(2 * QBLK)[None, :]
    rel = (jnp.arange(QBLK)[:, None] + QBLK) - kj
    for gi, (window, dil) in enumerate(GROUPS):
        band = (rel >= 0) & (rel <= window // dil)
        tab = rel_bias.astype(F32)[:, gi * HEADS_PER_GROUP:(gi + 1) * HEADS_PER_GROUP]
        onehot = (_t5_bucket(rel * dil)[..., None] == jnp.arange(N_BUCKETS)).astype(F32)
        bias = jnp.einsum("qkn,nh->hqk", onehot, tab, precision=lax.Precision.HIGHEST)
        tiles.append(jnp.where(band[None], bias, NEG_INF).reshape(HEADS_PER_GROUP * QBLK, 2 * QBLK))
    return tiles


def _merge_kernel(x_ref, ya_ref, yb_ref, *rest):
    n_g = len(GROUPS)
    attn = rest[:4 * n_g]
    zg_ref, pa_ref, pb_ref, pc_ref, wo_ref, out_ref = rest[4 * n_g:]
    d = x_ref.shape[-1]
    outs = [jnp.concatenate([attn[4 * g][...], attn[4 * g + 1][...]], axis=1) for g in range(n_g)]
    lses = [jnp.concatenate([attn[4 * g + 2][...], attn[4 * g + 3][...]], axis=1) for g in range(n_g)]
    m = functools.reduce(jnp.maximum, lses)
    es = [jnp.exp(l - m) for l in lses]
    yc = sum(o * e for o, e in zip(outs, es)) / sum(es)
    gate = lambda n: _sigmoid(zg_ref[:, n * d:(n + 1) * d].astype(F32))
    merged = (gate(0) * _dot(ya_ref[...].astype(BF16), pa_ref[...])
              + gate(1) * _dot(yb_ref[...].astype(BF16), pb_ref[...])
              + gate(2) * _dot(yc.astype(BF16), pc_ref[...]))
    out_ref[...] = x_ref[...] + _dot(merged.astype(BF16), wo_ref[...])


def _merge(x2d, ya, yb, attn, z2d, gate_block, p, tm):
    m, d = x2d.shape
    row = lambda w: pl.BlockSpec((tm, w), lambda i: (i, 0))
    full = lambda a: pl.BlockSpec(a.shape, lambda i: (0, 0))
    return pl.pallas_call(
        _merge_kernel,
        grid=(m // tm,),
        in_specs=[row(d), row(WIDTH_A), row(WIDTH_B)] + [row(PAIR)] * len(attn)
                 + [pl.BlockSpec((tm, 3 * d), lambda i: (i, gate_block)),
                    full(p["proj_a"]), full(p["proj_b"]), full(p["proj_c"]), full(p["w_out"])],
        out_specs=row(d),
        out_shape=jax.ShapeDtypeStruct((m, d), F32),
        compiler_params=_cparams(("parallel",)),
        name="merge",
    )(x2d, ya, yb, *attn, z2d, p["proj_a"], p["proj_b"], p["proj_c"], p["w_out"])


def _mlp_kernel(x_ref, g_ref, wu_ref, wd_ref, o_ref, h_ref, acc_ref):
    j = pl.program_id(1)

    @pl.when(j == 0)
    def _():
        x = x_ref[...]
        ms = jnp.mean(x * x, axis=-1, keepdims=True)
        h_ref[...] = (x * lax.rsqrt(ms + RMS_EPS) * g_ref[...]).astype(BF16)
        acc_ref[...] = jnp.zeros_like(acc_ref)

    u = jnp.maximum(_dot(h_ref[...], wu_ref[...]), 0.0)
    acc_ref[...] += _dot((u * u).astype(BF16), wd_ref[...])

    @pl.when(j == pl.num_programs(1) - 1)
    def _():
        o_ref[...] = x_ref[...] + acc_ref[...]


def _mlp(x2d, g, wu, wd, tm, tf):
    m, d = x2d.shape
    f = wu.shape[1]
    return pl.pallas_call(
        _mlp_kernel,
        grid=(m // tm, f // tf),
        in_specs=[pl.BlockSpec((tm, d), lambda i, j: (i, 0)),
                  pl.BlockSpec((1, d), lambda i, j: (0, 0)),
                  pl.BlockSpec((d, tf), lambda i, j: (0, j)),
                  pl.BlockSpec((tf, d), lambda i, j: (j, 0))],
        out_specs=pl.BlockSpec((tm, d), lambda i, j: (i, 0)),
        out_shape=jax.ShapeDtypeStruct((m, d), F32),
        scratch_shapes=[pltpu.VMEM((tm, d), BF16), pltpu.VMEM((tm, d), F32)],
        compiler_params=_cparams(("parallel", "arbitrary")),
        name="mlp",
    )(x2d, g, wu, wd)


def _pad_rows(w, lo, total):
    return jnp.pad(w, ((lo, total - lo - w.shape[0]), (0, 0)))


def _block_diag_halves(w):
    n, bd, _ = w.shape
    per = n // 2
    out = jnp.zeros((2, per * bd, per * bd), w.dtype)
    for i in range(n):
        j, q = divmod(i, per)
        out = out.at[j, q * bd:(q + 1) * bd, q * bd:(q + 1) * bd].set(w[i])
    return out.astype(BF16)


def _layer(x, l, bias_tiles, proj, prm):
    (norm_mix_g, _, rwkv_mu, rwkv_w0, rwkv_w_up, rwkv_a0, rwkv_a_up, rwkv_g_up, rwkv_k_k, rwkv_k_a,
     rwkv_r_k, rwkv_ln_g, rwkv_ln_b, proj_a, conv_w, conv_b, lru_wa, lru_ba, lru_wx, lru_bx, lru_lambda,
     proj_b, q_norm_g, k_norm_g, proj_c, w_out, norm_mlp_g, mlp_up, mlp_down) = [t[l] for t in prm]
    b, s, d = x.shape
    x2d = x.reshape(b * s, d)
    row = lambda t: t.reshape(1, -1).astype(F32)

    wz_all, z_b, z_g, z_c = proj
    c_rkv, c_lora = 3 * WIDTH_A, LORA_W + LORA_A + LORA_G
    pb = dict(conv_w=conv_w.astype(F32), conv_b=row(conv_b), wa=_block_diag_halves(lru_wa), ba=row(lru_ba),
              wx=_block_diag_halves(lru_wx), bx=row(lru_bx), lam=row(lru_lambda))
    z2d, zc2d, yb = _projection(x2d, row(norm_mix_g), wz_all, l, pb, z_b, z_c, s, 2048, 768)
    z3 = z2d.reshape(b, s, z_c)
    zc3 = zc2d.reshape(b, s, 3 * WIDTH_C)

    ones_a = _head_ones(WIDTH_A)
    pa = dict(
        mu_r=row(rwkv_mu[0:c_rkv]),
        mu_l=jnp.pad(row(rwkv_mu[c_rkv:]), ((0, 0), (0, LORA_PAD - c_lora))),
        w0=row(rwkv_w0), a0=row(rwkv_a0),
        w_up=_pad_rows(rwkv_w_up, 0, LORA_W + LORA_A).astype(BF16),
        a_up=_pad_rows(rwkv_a_up, LORA_W, LORA_W + LORA_A).astype(BF16),
        g_up=_pad_rows(rwkv_g_up, 0, LORA_PAD - LORA_W - LORA_A).astype(BF16),
        k_k=row(rwkv_k_k), k_a=row(rwkv_k_a), r_k=row(rwkv_r_k),
        ln_g=row(rwkv_ln_g), ln_b=row(rwkv_ln_b), ones=ones_a)
    ya = _rwkv_mixer(z3, pa)

    qg = jnp.tile(row(q_norm_g), (1, HEADS_PER_GROUP))
    kg = jnp.tile(row(k_norm_g), (1, HEADS_PER_GROUP))
    ones_c = _head_ones(GROUP_W)
    attn = []
    for gi, (_, dil) in enumerate(GROUPS):
        parts = _attn_group(zc3, 0, bias_tiles[gi], qg, kg, ones_c, gi, dil)
        attn += [t.reshape(b * s, PAIR) for t in parts]

    pm = dict(proj_a=proj_a.astype(BF16), proj_b=proj_b.astype(BF16), proj_c=proj_c.astype(BF16),
              w_out=w_out.astype(BF16))
    x1 = _merge(x2d, ya.reshape(b * s, WIDTH_A), yb, attn, z2d, z_g // (3 * d), pm, 512)
    x2 = _mlp(x1, row(norm_mlp_g), mlp_up.astype(BF16), mlp_down.astype(BF16), 1024, 1024)
    return x2.reshape(b, s, d)


def kernel(x, rel_bias, norm_mix_g, w_in, rwkv_mu, rwkv_w0, rwkv_w_up, rwkv_a0, rwkv_a_up, rwkv_g_up, rwkv_k_k, rwkv_k_a, rwkv_r_k, rwkv_ln_g, rwkv_ln_b, proj_a, conv_w, conv_b, lru_wa, lru_ba, lru_wx, lru_bx, lru_lambda, proj_b, q_norm_g, k_norm_g, proj_c, w_out, norm_mlp_g, mlp_up, mlp_down):
    prm = (norm_mix_g, w_in, rwkv_mu, rwkv_w0, rwkv_w_up, rwkv_a0, rwkv_a_up, rwkv_g_up, rwkv_k_k, rwkv_k_a,
           rwkv_r_k, rwkv_ln_g, rwkv_ln_b, proj_a, conv_w, conv_b, lru_wa, lru_ba, lru_wx, lru_bx, lru_lambda,
           proj_b, q_norm_g, k_norm_g, proj_c, w_out, norm_mlp_g, mlp_up, mlp_down)
    bias_tiles = _attn_bias_tiles(rel_bias)
    proj = _projection_weights(jnp.swapaxes(w_in.astype(F32), 1, 2))
    x = x.astype(F32)
    for l in range(norm_mix_g.shape[0]):
        x = _layer(x, l, bias_tiles, proj, prm)
    return x
```

```python
import functools
import math

import jax
import jax.numpy as jnp
from jax import lax
from jax.experimental import pallas as pl
from jax.experimental.pallas import tpu as pltpu

F32 = jnp.float32
BF16 = jnp.bfloat16

N_HEADS_A = 8
HEAD = 64
PAIR = 2 * HEAD
WIDTH_A = N_HEADS_A * HEAD
N_PAIRS = WIDTH_A // PAIR
CHUNK = 64
LOCAL_WAVE = 4
PREV_ROWS = 16
LRU_SLAB = 256
MXU_N = 256
LORA_W, LORA_A, LORA_G = 64, 64, 160
LORA_PAD = 384
GN_EPS = 64e-5
WIDTH_B = 512
LRU_BLOCK = 64
CONV_TAPS = 4
LRU_C = 8.0
GROUPS = ((128, 1), (512, 4), (2048, 16))
HEADS_PER_GROUP = 4
GROUP_W = HEADS_PER_GROUP * HEAD
WIDTH_C = len(GROUPS) * GROUP_W
QBLK = 128
ATTN_TILE = 2048
ATTN_WAVE = 4
N_BUCKETS = 32
MAX_DISTANCE = 2048
NEG_INF = -1e30
RMS_EPS = 1e-6
VMEM_LIMIT = 56 * 1024 * 1024


def _cparams(sem):
    return pltpu.CompilerParams(dimension_semantics=sem, vmem_limit_bytes=VMEM_LIMIT)


def _dot(a, b):
    return jnp.dot(a, b, preferred_element_type=F32)


def _dot_nt(a, b):
    return lax.dot_general(a, b, (((1,), (1,)), ((), ())), preferred_element_type=F32)


_NN = (((1,), (0,)), ((), ()))
_NT = (((1,), (1,)), ((), ()))
_TN = (((0,), (0,)), ((), ()))


def _mm(a, b, dims=_NN):
    return lax.dot_general(a.astype(BF16), b.astype(BF16), dims, preferred_element_type=F32)


def _sigmoid(x):
    return 1.0 / (1.0 + jnp.exp(-x))


def _softplus(x):
    return jnp.maximum(x, 0.0) + jnp.log1p(jnp.exp(-jnp.abs(x)))


def _head_ones(width):
    i = jnp.arange(width) // HEAD
    return (i[:, None] == i[None, :]).astype(BF16)


def _projection_kernel(x_ref, g_ref, w_ref, *rest, n_lo, copies, first_slab, tiles_per_seq):
    lru, (lo_ref, hi_ref, yb_ref, h_ref, zb_ref, hist_ref, carry_ref) = rest[:7], rest[7:]
    i, j = pl.program_id(0), pl.program_id(1)

    @pl.when(j == 0)
    def _():
        x = x_ref[...]
        ms = jnp.mean(x * x, axis=-1, keepdims=True)
        h_ref[...] = (x * lax.rsqrt(ms + RMS_EPS) * g_ref[...]).astype(BF16)

    s = j - first_slab
    rows = pl.ds(pl.multiple_of(s * LRU_SLAB, LRU_SLAB), LRU_SLAB)
    seq_start = jnp.logical_and(s == 0, lax.rem(i, tiles_per_seq) == 0)

    def slab_load():
        return (zb_ref[rows, 0:WIDTH_B].astype(F32), zb_ref[rows, WIDTH_B:2 * WIDTH_B].astype(F32),
                jnp.where(seq_start, 0.0, hist_ref[...]), jnp.where(seq_start, 0.0, carry_ref[0:1, :]))

    def slab_store(x, out, h_last):
        yb_ref[rows, :] = out.astype(yb_ref.dtype)
        hist_ref[...] = x[LRU_SLAB - PREV_ROWS:, :]
        carry_ref[0:1, :] = h_last

    def matmul(out_ref, copy):
        h = h_ref[...]
        for c0 in range(0, out_ref.shape[1], MXU_N):
            z = _dot_nt(h, w_ref[c0:c0 + MXU_N, :])
            out_ref[:, c0:c0 + MXU_N] = z.astype(out_ref.dtype)
            if copy is not None:
                src, dst, n = copy
                lo_c, hi_c = max(src, c0), min(src + n, c0 + MXU_N)
                if lo_c < hi_c:
                    zb_ref[:, dst + lo_c - src:dst + hi_c - src] = z[:, lo_c - c0:hi_c - c0].astype(zb_ref.dtype)
            yield

    def step(out_ref, copy, with_slab):
        if not with_slab:
            _interleave(matmul(out_ref, copy))
        elif copy is None:
            x, yb, prev, h_prev = slab_load()
            _interleave(matmul(out_ref, None), _lru_slab(x, yb, prev, seq_start, h_prev, lru, slab_store))
        else:
            _interleave(matmul(out_ref, copy))
            x, yb, prev, h_prev = slab_load()
            _interleave(_lru_slab(x, yb, prev, seq_start, h_prev, lru, slab_store))

    plain = j < first_slab
    for jb in copies:
        plain = jnp.logical_and(plain, j != jb)
        pl.when(j == jb)(functools.partial(step, lo_ref, copies[jb], jb == first_slab))
    pl.when(plain)(functools.partial(step, lo_ref, None, False))
    pl.when(jnp.logical_and(j > first_slab, j < n_lo))(functools.partial(step, lo_ref, None, True))
    pl.when(j >= n_lo)(functools.partial(step, hi_ref, None, True))


def _projection(x2d, g, w_all, layer, lru, z_b, n_lo_cols, seq_len, tm, tn):
    m, d = x2d.shape
    n = w_all.shape[1]
    n_lo, n_blocks = n_lo_cols // tn, n // tn
    copies = {}
    for jb in range(n_blocks):
        lo_c, hi_c = max(jb * tn, z_b), min((jb + 1) * tn, z_b + 2 * WIDTH_B)
        if lo_c < hi_c:
            copies[jb] = (lo_c - jb * tn, lo_c - z_b, hi_c - lo_c)
    first_slab = max(copies)
    assert first_slab < n_lo and n_blocks - first_slab == tm // LRU_SLAB and seq_len % tm == 0
    const2 = lambda i, j: (0, 0)
    vec = pl.BlockSpec((1, WIDTH_B), const2)
    half = WIDTH_B // 2
    mat = pl.BlockSpec((2, half, half), lambda i, j: (0, 0, 0))
    return pl.pallas_call(
        functools.partial(_projection_kernel, n_lo=n_lo, copies=copies, first_slab=first_slab,
                          tiles_per_seq=seq_len // tm),
        grid=(m // tm, n_blocks),
        in_specs=[pl.BlockSpec((tm, d), lambda i, j: (i, 0)),
                  pl.BlockSpec((1, d), const2),
                  pl.BlockSpec((None, tn, d), lambda i, j: (layer, j, 0)),
                  pl.BlockSpec((CONV_TAPS, WIDTH_B), const2), vec, mat, vec, mat, vec, vec],
        out_specs=[pl.BlockSpec((tm, tn), lambda i, j: (i, jnp.minimum(j, n_lo - 1))),
                   pl.BlockSpec((tm, tn), lambda i, j: (i, jnp.maximum(j - n_lo, 0))),
                   pl.BlockSpec((tm, WIDTH_B), lambda i, j: (i, 0))],
        out_shape=[jax.ShapeDtypeStruct((m, n_lo_cols), BF16),
                   jax.ShapeDtypeStruct((m, n - n_lo_cols), F32),
                   jax.ShapeDtypeStruct((m, WIDTH_B), BF16)],
        scratch_shapes=[pltpu.VMEM((tm, d), BF16), pltpu.VMEM((tm, 2 * WIDTH_B), BF16),
                        pltpu.VMEM((PREV_ROWS, WIDTH_B), F32), pltpu.VMEM((8, WIDTH_B), F32)],
        compiler_params=_cparams(("arbitrary", "arbitrary")),
        name="projection",
    )(x2d, g, w_all, lru["conv_w"], lru["conv_b"], lru["wa"], lru["ba"], lru["wx"], lru["bx"], lru["lam"])


def _relayout_kernel(w_ref, o_ref, *, moves, width):
    cols = w_ref.shape[2]
    end = 0
    for src, dst, n in moves:
        if dst > end:
            o_ref[0, end:dst, :] = jnp.zeros((dst - end, cols), o_ref.dtype)
        o_ref[0, dst:dst + n, :] = w_ref[0, src:src + n, :].astype(o_ref.dtype)
        end = dst + n
    if end < width:
        o_ref[0, end:width, :] = jnp.zeros((width - end, cols), o_ref.dtype)


def _projection_weights(w_t, tc=256):
    n_layers, n_in, d = w_t.shape
    c_rkv, c_lora = 3 * WIDTH_A, LORA_W + LORA_A + LORA_G
    o_b = c_rkv + c_lora
    o_c = o_b + 2 * WIDTH_B
    o_g = o_c + 3 * WIDTH_C
    z_b = -(-(c_rkv + LORA_PAD) // WIDTH_B) * WIDTH_B
    z_g = -(-(z_b + 2 * WIDTH_B) // (3 * d)) * (3 * d)
    z_c = z_g + 3 * d
    width = z_c + 3 * WIDTH_C
    moves = ((0, 0, o_b), (o_b, z_b, o_c - o_b), (o_g, z_g, n_in - o_g), (o_c, z_c, o_g - o_c))
    wz = pl.pallas_call(
        functools.partial(_relayout_kernel, moves=moves, width=width),
        grid=(n_layers, d // tc),
        in_specs=[pl.BlockSpec((1, n_in, tc), lambda l, i: (l, 0, i))],
        out_specs=pl.BlockSpec((1, width, tc), lambda l, i: (l, 0, i)),
        out_shape=jax.ShapeDtypeStruct((n_layers, width, d), BF16),
        compiler_params=_cparams(("parallel", "parallel")),
        name="projection_weights",
    )(w_t)
    return wz, z_b, z_g, z_c


def _shift_rows(cur, prev):
    prev_row = prev[PREV_ROWS - 1:PREV_ROWS, :]
    rolled = pltpu.roll(cur, 1, axis=0)
    row = lax.broadcasted_iota(jnp.int32, cur.shape, 0)
    return jnp.where(row == 0, prev_row, rolled)


def _interleave(*stages):
    live = list(stages)
    while live:
        for gen in list(live):
            try:
                next(gen)
            except StopIteration:
                live.remove(gen)


def _rwkv_prep(zr, zl, prev_r, prev_l, rows, params, scan, put_gate, put_bonus):
    mur_ref, mul_ref, w0_ref, wup_ref, a0_ref, aup_ref, gup_ref, kk_ref, ka_ref, rk_ref, ones_ref = params
    r_out, k_out, v_out, lg_out, lgp_out, as_out, bs_out = scan
    fr = zr + (_shift_rows(zr, prev_r) - zr) * mur_ref[...]
    fl = zl + (_shift_rows(zl, prev_l) - zl) * mul_ref[...]
    r = fr[:, 0:WIDTH_A]
    k = fr[:, WIDTH_A:2 * WIDTH_A]
    v = fr[:, 2 * WIDTH_A:3 * WIDTH_A]
    x_wa = fl[:, 0:LORA_W + LORA_A]
    x_g = fl[:, LORA_W + LORA_A:LORA_PAD]
    ones = ones_ref[...]
    r_out[0, rows, :] = r
    v_out[0, rows, :] = v
    yield

    w = -_softplus(-(w0_ref[...] + _dot(jnp.tanh(x_wa).astype(BF16), wup_ref[...]))) - 0.5
    lw = -jnp.exp(w)
    pos = lax.broadcasted_iota(jnp.int32, lw.shape, 0) & (CHUNK - 1)
    lg = lw
    for sh in [1 << i for i in range(int(math.log2(CHUNK)))]:
        lg = lg + jnp.where(pos >= sh, pltpu.roll(lg, sh, axis=0), 0.0)
    lg_out[0, rows, :] = lg
    lgp_out[0, rows, :] = lg - lw
    yield

    a = _sigmoid(a0_ref[...] + _dot(x_wa.astype(BF16), aup_ref[...]))
    k2 = k * (1.0 + (a - 1.0) * ka_ref[...])
    k_out[0, rows, :] = k2
    yield

    kk = k * kk_ref[...]
    kk = kk / jnp.maximum(jnp.sqrt(_dot((kk * kk).astype(BF16), ones)), 1e-12)
    as_out[0, rows, :] = -kk
    bs_out[0, rows, :] = kk * a
    yield

    put_gate(_dot(_sigmoid(x_g).astype(BF16), gup_ref[...]))
    yield

    put_bonus(_dot((r * k2 * rk_ref[...]).astype(BF16), ones) * v)


def _stack_heads(x):
    lo = lax.broadcasted_iota(jnp.int32, x.shape, 1) < HEAD
    return jnp.concatenate([jnp.where(lo, x, 0.0), jnp.where(lo, 0.0, x)], axis=0)


def _unstack_heads(x):
    return x[0:CHUNK, :] + x[CHUNK:2 * CHUNK, :]


def _rwkv_local_kernel(*refs, n_chunks):
    (zr_ref, zl_ref, zr_next, zl_next), params = refs[:4], refs[4:15]
    t_out, g_out, rh_out, yh_out, gate_out, bonus_out = refs[15:21]
    scan, (gate_carry, bonus_carry) = refs[21:28], refs[28:]
    r_ref, k_ref, v_ref, lg_ref, lgp_ref, as_ref, bs_ref = scan
    two_c = 2 * CHUNK
    row = lax.broadcasted_iota(jnp.int32, (two_c, two_c), 0)
    col = lax.broadcasted_iota(jnp.int32, (two_c, two_c), 1)
    strict = col < row
    incl = col <= row
    eye = (col == row).astype(F32)
    wave_rows = LOCAL_WAVE * CHUNK
    n_waves = n_chunks // LOCAL_WAVE
    first_rows = slice(0, wave_rows)
    f32 = lambda ref, rows: ref[0, rows, :].astype(F32)

    def put(ref, rows):
        def store(val):
            ref[rows] = val
        return store

    def prep(w):
        rows = slice(w * wave_rows, (w + 1) * wave_rows)
        before = slice(w * wave_rows - PREV_ROWS, w * wave_rows)
        return _rwkv_prep(f32(zr_ref, rows), f32(zl_ref, rows), f32(zr_ref, before), f32(zl_ref, before), rows,
                          params, scan, put(gate_out, (0, rows)), put(bonus_out, (0, rows)))

    def prep_first(zr_src, zl_src, prev_r, prev_l):
        return _rwkv_prep(f32(zr_src, first_rows), f32(zl_src, first_rows), prev_r, prev_l, first_rows,
                          params, scan, put(gate_carry, slice(None)), put(bonus_carry, slice(None)))

    @pl.when(pl.program_id(1) == 0)
    def _():
        zero = lambda ref: jnp.zeros((PREV_ROWS, ref.shape[-1]), F32)
        _interleave(prep_first(zr_ref, zl_ref, zero(zr_ref), zero(zl_ref)))

    gate_out[0, first_rows, :] = gate_carry[...]
    bonus_out[0, first_rows, :] = bonus_carry[...]

    def wave(w):
        c0 = w * LOCAL_WAVE
        units = [(c, p) for c in range(c0, min(c0 + LOCAL_WAVE, n_chunks)) for p in range(N_PAIRS)]
        at = lambda ref: [ref[0, c * CHUNK:(c + 1) * CHUNK, p * PAIR:(p + 1) * PAIR] for c, p in units]
        each = lambda f, *ls: [f(*xs) for xs in zip(*ls)]
        lg = at(lg_ref)
        lg_end = each(lambda x: x[CHUNK - 1:CHUNK, :], lg)
        e_neg = each(lambda x: jnp.exp(-x), lg)
        e_end = each(lambda x, xe: jnp.exp(xe - x), lg, lg_end)
        a_s, b_s, kk = at(as_ref), at(bs_ref), at(k_ref)
        a_t = each(lambda x, gp: _stack_heads(x * jnp.exp(gp)), a_s, at(lgp_ref))
        r_t = each(lambda x, g: _stack_heads(x * jnp.exp(g)), at(r_ref), lg)
        b_t = each(lambda x, e: _stack_heads(x * e), b_s, e_neg)
        k_t = each(lambda x, e: _stack_heads(x * e), kk, e_neg)
        b_p = each(lambda x, e: _stack_heads(x * e), b_s, e_end)
        k_p = each(lambda x, e: _stack_heads(x * e), kk, e_end)
        v_s = each(_stack_heads, at(v_ref))
        yield

        vcat = lambda x, y: jnp.concatenate([x, y], axis=0)
        hcat = lambda x, y: jnp.concatenate([x, y], axis=1)
        top = lambda x: x[0:two_c]
        bot = lambda x: x[two_c:2 * two_c]
        left = lambda x: x[:, 0:two_c]
        right = lambda x: x[:, two_c:2 * two_c]

        prod = each(lambda a, r, b, k: _mm(vcat(a, r), vcat(b, k), _NT), a_t, r_t, b_t, k_t)
        l_ab = each(lambda x: jnp.where(strict, left(top(x)), 0.0), prod)
        l_ak = each(lambda x: jnp.where(strict, right(top(x)), 0.0), prod)
        l_rb = each(lambda x: jnp.where(incl, left(bot(x)), 0.0), prod)
        l_rk = each(lambda x: jnp.where(incl, right(bot(x)), 0.0), prod)
        yield

        inv = each(lambda x: eye + x, l_ab)
        pw = each(lambda x: _mm(x, x), l_ab)
        yield
        for _ in range(int(math.log2(CHUNK)) - 2):
            both = each(lambda p, x: _mm(vcat(p, x), p), pw, inv)
            pw = each(top, both)
            inv = each(lambda x, y: x + bot(y), inv, both)
            yield
        inv = each(lambda x, p: x + _mm(x, p), inv, pw)
        yield

        lv_rkv = each(lambda l1, l2, v: _mm(vcat(l1, l2), v), l_ak, l_rk, v_s)
        yield
        aw_h = each(lambda m, a, x: _mm(m, hcat(a, top(x))), inv, a_t, lv_rkv)
        yield
        l_aw = each(_mm, l_rb, aw_h)
        r_h = each(lambda x, y: x + left(y), r_t, l_aw)
        y_h = each(lambda y, x: right(y) + bot(x), l_aw, lv_rkv)
        yield
        tg = each(lambda x, bp: _mm(x, bp, _TN), aw_h, b_p)
        t_m = each(lambda ge, x: eye * jnp.exp(ge) + top(x), lg_end, tg)
        yield
        g_m = each(lambda x, v, kp: bot(x) + _mm(v, kp, _TN), tg, v_s, k_p)
        for i, (c, p) in enumerate(units):
            rows = slice(c * CHUNK, (c + 1) * CHUNK)
            lanes = slice(p * PAIR, (p + 1) * PAIR)
            t_out[0, c, p] = t_m[i].astype(t_out.dtype)
            g_out[0, c, p] = g_m[i]
            rh_out[0, rows, lanes] = _unstack_heads(r_h[i]).astype(rh_out.dtype)
            yh_out[0, rows, lanes] = _unstack_heads(y_h[i])

    last = slice(n_waves * wave_rows - PREV_ROWS, n_waves * wave_rows)
    for w in range(n_waves):
        nxt = prep(w + 1) if w + 1 < n_waves else prep_first(zr_next, zl_next, f32(zr_ref, last), f32(zl_ref, last))
        _interleave(wave(w), nxt)


def _rwkv_local(z3, p, ts):
    b, s, _ = z3.shape
    n_chunks = ts // CHUNK
    wr, wl = 3 * WIDTH_A, LORA_PAD
    lora_block = wr // wl
    wave_rows = LOCAL_WAVE * CHUNK
    n_waves = ts // wave_rows
    assert n_waves >= 2 and ts % wave_rows == 0
    nxt = lambda c: (lambda bi, i: (bi, jnp.minimum((i + 1) * n_waves, s // wave_rows - 1), c))
    cur = lambda bi, i: (bi, i, 0)
    mat = lambda bi, i: (bi, i, 0, 0, 0)
    const = lambda bi, i: (0, 0)
    vec = pl.BlockSpec((1, WIDTH_A), const)
    seq_spec = pl.BlockSpec((1, ts, WIDTH_A), cur)
    mat_spec = pl.BlockSpec((1, n_chunks, N_PAIRS, PAIR, PAIR), mat)
    seq_shape = lambda dt: jax.ShapeDtypeStruct((b, s, WIDTH_A), dt)
    mat_shape = lambda dt: jax.ShapeDtypeStruct((b, s // CHUNK, N_PAIRS, PAIR, PAIR), dt)
    return pl.pallas_call(
        functools.partial(_rwkv_local_kernel, n_chunks=n_chunks),
        grid=(b, s // ts),
        in_specs=[pl.BlockSpec((1, ts, wr), cur),
                  pl.BlockSpec((1, ts, wl), lambda bi, i: (bi, i, lora_block)),
                  pl.BlockSpec((1, wave_rows, wr), nxt(0)),
                  pl.BlockSpec((1, wave_rows, wl), nxt(lora_block)),
                  pl.BlockSpec((1, wr), const), pl.BlockSpec((1, wl), const),
                  vec, pl.BlockSpec((LORA_W + LORA_A, WIDTH_A), const),
                  vec, pl.BlockSpec((LORA_W + LORA_A, WIDTH_A), const),
                  pl.BlockSpec((LORA_PAD - LORA_W - LORA_A, WIDTH_A), const),
                  vec, vec, vec, pl.BlockSpec((WIDTH_A, WIDTH_A), const)],
        out_specs=[mat_spec, mat_spec, seq_spec, seq_spec, seq_spec, seq_spec],
        out_shape=[mat_shape(BF16), mat_shape(F32), seq_shape(BF16), seq_shape(F32), seq_shape(F32), seq_shape(F32)],
        scratch_shapes=[pltpu.VMEM((1, ts, WIDTH_A), F32)] * 7 + [pltpu.VMEM((wave_rows, WIDTH_A), F32)] * 2,
        compiler_params=_cparams(("parallel", "arbitrary")),
        name="rwkv_local",
    )(z3, z3, z3, z3, p["mu_r"], p["mu_l"], p["w0"], p["w_up"], p["a0"], p["a_up"], p["g_up"],
      p["k_k"], p["k_a"], p["r_k"], p["ones"])


def _rwkv_state_kernel(t_ref, gm_ref, rh_ref, yh_ref, g_ref, bonus_ref, lng_ref, lnb_ref, ones_ref,
                       y_out, s_ref, *, n_chunks):
    n_seq = s_ref.shape[0]

    @pl.when(pl.program_id(0) == 0)
    def _():
        s_ref[...] = jnp.zeros_like(s_ref)

    ones = ones_ref[...]
    units = [(b, p) for b in range(n_seq) for p in range(N_PAIRS)]
    rows = [slice(c * CHUNK, (c + 1) * CHUNK) for c in range(n_chunks)]
    state = {u: s_ref[u] for u in units}
    y = []
    for c in range(n_chunks):
        entry = dict(state)
        state = {(b, p): _mm(entry[b, p], t_ref[b, c, p]) + gm_ref[b, c, p] for b, p in units}
        y.append([jnp.concatenate([_mm(rh_ref[b, rows[c], p * PAIR:(p + 1) * PAIR], entry[b, p], _NT)
                                   for p in range(N_PAIRS)], axis=1) + yh_ref[b, rows[c], :]
                  for b in range(n_seq)])
    for u in units:
        s_ref[u] = state[u]
    flat = [(c, b) for c in range(n_chunks) for b in range(n_seq)]
    yv = [y[c][b] for c, b in flat]
    mean = [_dot(v.astype(BF16), ones) * (1.0 / HEAD) for v in yv]
    yc = [v - m for v, m in zip(yv, mean)]
    var = [_dot((v * v).astype(BF16), ones) * (1.0 / HEAD) for v in yc]
    for (c, b), v, vr in zip(flat, yc, var):
        yn = v * lax.rsqrt(vr + GN_EPS) * lng_ref[...] + lnb_ref[...]
        y_out[b, rows[c], :] = ((yn + bonus_ref[b, rows[c], :]) * g_ref[b, rows[c], :]).astype(y_out.dtype)


def _rwkv_state(t, gm, rh, yh, g, bonus, p, ts):
    b, s, _ = rh.shape
    n_chunks = ts // CHUNK
    cur = lambda i: (0, i, 0)
    mat = lambda i: (0, i, 0, 0, 0)
    const = lambda i: (0, 0)
    seq_spec = pl.BlockSpec((b, ts, WIDTH_A), cur)
    mat_spec = pl.BlockSpec((b, n_chunks, N_PAIRS, PAIR, PAIR), mat)
    vec = pl.BlockSpec((1, WIDTH_A), const)
    return pl.pallas_call(
        functools.partial(_rwkv_state_kernel, n_chunks=n_chunks),
        grid=(s // ts,),
        in_specs=[mat_spec, mat_spec, seq_spec, seq_spec, seq_spec, seq_spec, vec, vec,
                  pl.BlockSpec((WIDTH_A, WIDTH_A), const)],
        out_specs=seq_spec,
        out_shape=jax.ShapeDtypeStruct((b, s, WIDTH_A), BF16),
        scratch_shapes=[pltpu.VMEM((b, N_PAIRS, PAIR, PAIR), F32)],
        compiler_params=_cparams(("arbitrary",)),
        name="rwkv_state",
    )(t, gm, rh, yh, g, bonus, p["ln_g"], p["ln_b"], p["ones"])


def _rwkv_mixer(z3, p):
    t, gm, rh, yh, g, bonus = _rwkv_local(z3, p, ts=512)
    return _rwkv_state(t, gm, rh, yh, g, bonus, p, ts=512)


def _lru_slab(x, yb, prev, seq_start, h_prev, params, emit):
    cw_ref, cb_ref, wa_ref, ba_ref, wx_ref, bx_ref, lam_ref = params
    n_rows = x.shape[0]
    ext = jnp.concatenate([prev, x], axis=0)
    xc = x * cw_ref[CONV_TAPS - 1:CONV_TAPS, :] + cb_ref[...]
    for back in range(1, CONV_TAPS):
        tap = CONV_TAPS - 1 - back
        xc = xc + pltpu.roll(ext, back, axis=0)[PREV_ROWS:, :] * cw_ref[tap:tap + 1, :]

    half = WIDTH_B // 2
    xcb = xc.astype(BF16)
    ga = jnp.concatenate([_dot(xcb[:, j * half:(j + 1) * half], wa_ref[j]) for j in range(2)], axis=1)
    gx = jnp.concatenate([_dot(xcb[:, j * half:(j + 1) * half], wx_ref[j]) for j in range(2)], axis=1)
    gate_a = _sigmoid(ga + ba_ref[...])
    gate_x = _sigmoid(gx + bx_ref[...])
    log_a = -LRU_C * gate_a * _softplus(-lam_ref[...])
    a = jnp.exp(log_a)
    mult = jnp.sqrt(jnp.maximum(-jnp.tanh(log_a) * (1.0 + a * a), 0.0))
    xg = xc * gate_x
    b = xg * mult
    yield

    row8 = lax.broadcasted_iota(jnp.int32, (8, WIDTH_B), 0)
    h, hs = h_prev, []
    for g in range(n_rows // 8):
        a8, b8 = a[8 * g:8 * g + 8, :], b[8 * g:8 * g + 8, :]
        if g == 0:
            b8 = jnp.where(jnp.logical_and(row8 == 0, seq_start), xg[0:8, :], b8)
        for sh in (1, 2, 4):
            ar = pltpu.roll(a8, sh, axis=0)
            br = pltpu.roll(b8, sh, axis=0)
            m = row8 >= sh
            b8 = jnp.where(m, a8 * br + b8, b8)
            a8 = jnp.where(m, a8 * ar, a8)
        h8 = a8 * h + b8
        hs.append(h8)
        h = h8[7:8, :]
    yield
    gelu = 0.5 * yb * (1.0 + jnp.tanh(math.sqrt(2.0 / math.pi) * (yb + 0.044715 * (yb * yb * yb))))
    emit(x, jnp.concatenate(hs, axis=0) * gelu, h)


def _attn_kernel(q0, q1, k0, k1, kp0, kp1, v0, v1, vp0, vp1, bias_ref, qg_ref, kg_ref, ones_ref,
                 o0, o1, l0, l1, *, dil, n_sub):
    j = pl.program_id(1)
    ones = ones_ref[...]
    lane = lax.broadcasted_iota(jnp.int32, (QBLK, GROUP_W), 1)
    in_head = [(lane >= h * HEAD) & (lane < (h + 1) * HEAD) for h in range(HEADS_PER_GROUP)]
    prev_valid = (lax.broadcasted_iota(jnp.int32, (HEADS_PER_GROUP * QBLK, 2 * QBLK), 1) >= QBLK) | (j > 0)

    def rows(start):
        return pl.ds(start, QBLK, stride=dil) if dil > 1 else pl.ds(start, QBLK)

    def take(lo, hi, start):
        return jnp.concatenate([lo[0, rows(start), :], hi[0, rows(start), :]], axis=1)

    def head_sumsq(x):
        return _dot((x * x).astype(BF16), ones) * (1.0 / HEAD)

    def select_heads(x):
        out = jnp.zeros((QBLK, GROUP_W), F32)
        for h, m in enumerate(in_head):
            out = jnp.where(m, x[h * QBLK:(h + 1) * QBLK, :], out)
        return out

    def wave(units):
        each = lambda f, *ls: [f(*xs) for xs in zip(*ls)]
        span = dil * QBLK
        starts = [s for s, _ in units]
        before = [s - span if dil > 1 or isinstance(s, int) else pl.multiple_of(s - span, QBLK) for s in starts]
        q_raw = [take(q0, q1, s) for s in starts]
        k_raw = [jnp.concatenate([take(kp0, kp1, s) if far else take(k0, k1, p), take(k0, k1, s)], axis=0)
                 for (s, far), p in zip(units, before)]
        vv = [jnp.concatenate([take(vp0, vp1, s) if far else take(v0, v1, p), take(v0, v1, s)],
                              axis=0).astype(BF16) for (s, far), p in zip(units, before)]
        q_ms = each(head_sumsq, q_raw)
        k_ms = each(head_sumsq, k_raw)
        q = each(lambda x, ms: x * lax.rsqrt(ms + RMS_EPS) * qg_ref[...] * (HEAD ** -0.5), q_raw, q_ms)
        kk = each(lambda x, ms: (x * lax.rsqrt(ms + RMS_EPS) * kg_ref[...]).astype(BF16), k_raw, k_ms)
        qs = each(lambda x: jnp.concatenate([jnp.where(m, x, 0.0) for m in in_head], axis=0).astype(BF16), q)
        logits = each(lambda a, b: _dot_nt(a, b) + bias_ref[...], qs, kk)
        logits = [jnp.where(prev_valid, lg, NEG_INF) if far else lg for lg, (_, far) in zip(logits, units)]
        mx = each(lambda lg: jnp.max(lg, axis=-1, keepdims=True), logits)
        pr = each(lambda lg, m: jnp.exp(lg - m), logits, mx)
        den = each(lambda p: jnp.sum(p, axis=-1, keepdims=True), pr)
        pv = each(lambda p, v, dn: _dot(p.astype(BF16), v) / dn, pr, vv, den)
        out = each(select_heads, pv)
        lse = each(lambda m, dn: select_heads(jnp.broadcast_to(m + jnp.log(dn), (HEADS_PER_GROUP * QBLK, GROUP_W))),
                   mx, den)
        for s, o, l in zip(starts, out, lse):
            o0[0, rows(s), :] = o[:, 0:PAIR]
            o1[0, rows(s), :] = o[:, PAIR:GROUP_W]
            l0[0, rows(s), :] = l[:, 0:PAIR]
            l1[0, rows(s), :] = l[:, PAIR:GROUP_W]

    def loop(lo, hi, body):
        def step(i, carry):
            body(i)
            return carry
        lax.fori_loop(lo, hi, step, 0)

    span = dil * QBLK
    if n_sub == 1:
        loop(0, dil // ATTN_WAVE, lambda i: wave([(i * ATTN_WAVE + u, True) for u in range(ATTN_WAVE)]))
    elif dil > 1:
        wave([(r, True) for r in range(dil)])
        loop(1, n_sub, lambda n: wave([(r + n * span, False) for r in range(dil)]))
    else:
        wave([(u * QBLK, u == 0) for u in range(ATTN_WAVE)])
        loop(1, n_sub // ATTN_WAVE,
             lambda i: wave([(pl.multiple_of((i * ATTN_WAVE + u) * QBLK, QBLK), False) for u in range(ATTN_WAVE)]))


def _attn_group(z3, col0, bias, qg, kg, ones, gi, dil):
    b, s, _ = z3.shape
    span = dil * QBLK
    n_sub = ATTN_TILE // span
    halves = GROUP_W // PAIR
    per_part = len(GROUPS) * halves

    def cur(part, half):
        c = col0 + part * per_part + gi * halves + half
        return pl.BlockSpec((1, ATTN_TILE, PAIR), lambda bi, j: (bi, j, c))

    def prev(part, half):
        c = col0 + part * per_part + gi * halves + half
        return pl.BlockSpec((1, span, PAIR), lambda bi, j: (bi, jnp.maximum(j * n_sub - 1, 0), c))

    const2 = lambda bi, j: (0, 0)
    out_spec = pl.BlockSpec((1, ATTN_TILE, PAIR), lambda bi, j: (bi, j, 0))
    return pl.pallas_call(
        functools.partial(_attn_kernel, dil=dil, n_sub=n_sub),
        grid=(b, s // ATTN_TILE),
        in_specs=[cur(0, 0), cur(0, 1), cur(1, 0), cur(1, 1), prev(1, 0), prev(1, 1),
                  cur(2, 0), cur(2, 1), prev(2, 0), prev(2, 1),
                  pl.BlockSpec((HEADS_PER_GROUP * QBLK, 2 * QBLK), const2),
                  pl.BlockSpec((1, GROUP_W), const2), pl.BlockSpec((1, GROUP_W), const2),
                  pl.BlockSpec((GROUP_W, GROUP_W), const2)],
        out_specs=[out_spec] * 4,
        out_shape=[jax.ShapeDtypeStruct((b, s, PAIR), F32)] * 4,
        compiler_params=_cparams(("parallel", "arbitrary")),
        name=f"dilated_attn_g{gi}",
    )(*([z3] * 10), bias, qg, kg, ones)


def _t5_bucket(dist):
    max_exact = N_BUCKETS // 2
    d = jnp.maximum(dist, 0)
    large = max_exact + (jnp.log(jnp.maximum(d, 1).astype(F32) / max_exact)
                         / math.log(MAX_DISTANCE / max_exact) * (N_BUCKETS - max_exact)).astype(jnp.int32)
    large = jnp.minimum(large, N_BUCKETS - 1)
    return jnp.where(d < max_exact, d, large)


def _attn_bias_tiles(rel_bias):
    tiles = []
    kj = jnp.arange(2 * QBLK)[None, :]
    rel = (jnp.arange(QBLK)[:, None] + QBLK) - kj
    for gi, (window, dil) in enumerate(GROUPS):
        band = (rel >= 0) & (rel <= window // dil)
        tab = rel_bias.astype(F32)[:, gi * HEADS_PER_GROUP:(gi + 1) * HEADS_PER_GROUP]
        onehot = (_t5_bucket(rel * dil)[..., None] == jnp.arange(N_BUCKETS)).astype(F32)
        bias = jnp.einsum("qkn,nh->hqk", onehot, tab, precision=lax.Precision.HIGHEST)
        tiles.append(jnp.where(band[None], bias, NEG_INF).reshape(HEADS_PER_GROUP * QBLK, 2 * QBLK))
    return tiles


def _merge_kernel(x_ref, ya_ref, yb_ref, *rest):
    n_g = len(GROUPS)
    attn = rest[:4 * n_g]
    zg_ref, pa_ref, pb_ref, pc_ref, wo_ref, out_ref = rest[4 * n_g:]
    d = x_ref.shape[-1]
    outs = [jnp.concatenate([attn[4 * g][...], attn[4 * g + 1][...]], axis=1) for g in range(n_g)]
    lses = [jnp.concatenate([attn[4 * g + 2][...], attn[4 * g + 3][...]], axis=1) for g in range(n_g)]
    m = functools.reduce(jnp.maximum, lses)
    es = [jnp.exp(l - m) for l in lses]
    yc = sum(o * e for o, e in zip(outs, es)) / sum(es)
    gate = lambda n: _sigmoid(zg_ref[:, n * d:(n + 1) * d].astype(F32))
    merged = (gate(0) * _dot(ya_ref[...].astype(BF16), pa_ref[...])
              + gate(1) * _dot(yb_ref[...].astype(BF16), pb_ref[...])
              + gate(2) * _dot(yc.astype(BF16), pc_ref[...]))
    out_ref[...] = x_ref[...] + _dot(merged.astype(BF16), wo_ref[...])


def _merge(x2d, ya, yb, attn, z2d, gate_block, p, tm):
    m, d = x2d.shape
    row = lambda w: pl.BlockSpec((tm, w), lambda i: (i, 0))
    full = lambda a: pl.BlockSpec(a.shape, lambda i: (0, 0))
    return pl.pallas_call(
        _merge_kernel,
        grid=(m // tm,),
        in_specs=[row(d), row(WIDTH_A), row(WIDTH_B)] + [row(PAIR)] * len(attn)
                 + [pl.BlockSpec((tm, 3 * d), lambda i: (i, gate_block)),
                    full(p["proj_a"]), full(p["proj_b"]), full(p["proj_c"]), full(p["w_out"])],
        out_specs=row(d),
        out_shape=jax.ShapeDtypeStruct((m, d), F32),
        compiler_params=_cparams(("parallel",)),
        name="merge",
    )(x2d, ya, yb, *attn, z2d, p["proj_a"], p["proj_b"], p["proj_c"], p["w_out"])


def _mlp_kernel(x_ref, g_ref, wu_ref, wd_ref, o_ref, h_ref, acc_ref):
    j = pl.program_id(1)

    @pl.when(j == 0)
    def _():
        x = x_ref[...]
        ms = jnp.mean(x * x, axis=-1, keepdims=True)
        h_ref[...] = (x * lax.rsqrt(ms + RMS_EPS) * g_ref[...]).astype(BF16)
        acc_ref[...] = jnp.zeros_like(acc_ref)

    u = jnp.maximum(_dot(h_ref[...], wu_ref[...]), 0.0)
    acc_ref[...] += _dot((u * u).astype(BF16), wd_ref[...])

    @pl.when(j == pl.num_programs(1) - 1)
    def _():
        o_ref[...] = x_ref[...] + acc_ref[...]


def _mlp(x2d, g, wu, wd, tm, tf):
    m, d = x2d.shape
    f = wu.shape[1]
    return pl.pallas_call(
        _mlp_kernel,
        grid=(m // tm, f // tf),
        in_specs=[pl.BlockSpec((tm, d), lambda i, j: (i, 0)),
                  pl.BlockSpec((1, d), lambda i, j: (0, 0)),
                  pl.BlockSpec((d, tf), lambda i, j: (0, j)),
                  pl.BlockSpec((tf, d), lambda i, j: (j, 0))],
        out_specs=pl.BlockSpec((tm, d), lambda i, j: (i, 0)),
        out_shape=jax.ShapeDtypeStruct((m, d), F32),
        scratch_shapes=[pltpu.VMEM((tm, d), BF16), pltpu.VMEM((tm, d), F32)],
        compiler_params=_cparams(("parallel", "arbitrary")),
        name="mlp",
    )(x2d, g, wu, wd)


def _pad_rows(w, lo, total):
    return jnp.pad(w, ((lo, total - lo - w.shape[0]), (0, 0)))


def _block_diag_halves(w):
    n, bd, _ = w.shape
    per = n // 2
    out = jnp.zeros((2, per * bd, per * bd), w.dtype)
    for i in range(n):
        j, q = divmod(i, per)
        out = out.at[j, q * bd:(q + 1) * bd, q * bd:(q + 1) * bd].set(w[i])
    return out.astype(BF16)


def _layer(x, l, bias_tiles, proj, prm):
    (norm_mix_g, _, rwkv_mu, rwkv_w0, rwkv_w_up, rwkv_a0, rwkv_a_up, rwkv_g_up, rwkv_k_k, rwkv_k_a,
     rwkv_r_k, rwkv_ln_g, rwkv_ln_b, proj_a, conv_w, conv_b, lru_wa, lru_ba, lru_wx, lru_bx, lru_lambda,
     proj_b, q_norm_g, k_norm_g, proj_c, w_out, norm_mlp_g, mlp_up, mlp_down) = [t[l] for t in prm]
    b, s, d = x.shape
    x2d = x.reshape(b * s, d)
    row = lambda t: t.reshape(1, -1).astype(F32)

    wz_all, z_b, z_g, z_c = proj
    c_rkv, c_lora = 3 * WIDTH_A, LORA_W + LORA_A + LORA_G
    pb = dict(conv_w=conv_w.astype(F32), conv_b=row(conv_b), wa=_block_diag_halves(lru_wa), ba=row(lru_ba),
              wx=_block_diag_halves(lru_wx), bx=row(lru_bx), lam=row(lru_lambda))
    z2d, zc2d, yb = _projection(x2d, row(norm_mix_g), wz_all, l, pb, z_b, z_c, s, 2048, 768)
    z3 = z2d.reshape(b, s, z_c)
    zc3 = zc2d.reshape(b, s, 3 * WIDTH_C)

    ones_a = _head_ones(WIDTH_A)
    pa = dict(
        mu_r=row(rwkv_mu[0:c_rkv]),
        mu_l=jnp.pad(row(rwkv_mu[c_rkv:]), ((0, 0), (0, LORA_PAD - c_lora))),
        w0=row(rwkv_w0), a0=row(rwkv_a0),
        w_up=_pad_rows(rwkv_w_up, 0, LORA_W + LORA_A).astype(BF16),
        a_up=_pad_rows(rwkv_a_up, LORA_W, LORA_W + LORA_A).astype(BF16),
        g_up=_pad_rows(rwkv_g_up, 0, LORA_PAD - LORA_W - LORA_A).astype(BF16),
        k_k=row(rwkv_k_k), k_a=row(rwkv_k_a), r_k=row(rwkv_r_k),
        ln_g=row(rwkv_ln_g), ln_b=row(rwkv_ln_b), ones=ones_a)
    ya = _rwkv_mixer(z3, pa)

    qg = jnp.tile(row(q_norm_g), (1, HEADS_PER_GROUP))
    kg = jnp.tile(row(k_norm_g), (1, HEADS_PER_GROUP))
    ones_c = _head_ones(GROUP_W)
    attn = []
    for gi, (_, dil) in enumerate(GROUPS):
        parts = _attn_group(zc3, 0, bias_tiles[gi], qg, kg, ones_c, gi, dil)
        attn += [t.reshape(b * s, PAIR) for t in parts]

    pm = dict(proj_a=proj_a.astype(BF16), proj_b=proj_b.astype(BF16), proj_c=proj_c.astype(BF16),
              w_out=w_out.astype(BF16))
    x1 = _merge(x2d, ya.reshape(b * s, WIDTH_A), yb, attn, z2d, z_g // (3 * d), pm, 512)
    x2 = _mlp(x1, row(norm_mlp_g), mlp_up.astype(BF16), mlp_down.astype(BF16), 1024, 2048)
    return x2.reshape(b, s, d)


def kernel(x, rel_bias, norm_mix_g, w_in, rwkv_mu, rwkv_w0, rwkv_w_up, rwkv_a0, rwkv_a_up, rwkv_g_up, rwkv_k_k, rwkv_k_a, rwkv_r_k, rwkv_ln_g, rwkv_ln_b, proj_a, conv_w, conv_b, lru_wa, lru_ba, lru_wx, lru_bx, lru_lambda, proj_b, q_norm_g, k_norm_g, proj_c, w_out, norm_mlp_g, mlp_up, mlp_down):
    prm = (norm_mix_g, w_in, rwkv_mu, rwkv_w0, rwkv_w_up, rwkv_a0, rwkv_a_up, rwkv_g_up, rwkv_k_k, rwkv_k_a,
           rwkv_r_k, rwkv_ln_g, rwkv_ln_b, proj_a, conv_w, conv_b, lru_wa, lru_ba, lru_wx, lru_bx, lru_lambda,
           proj_b, q_norm_g, k_norm_g, proj_c, w_out, norm_mlp_g, mlp_up, mlp_down)
    bias_tiles = _attn_bias_tiles(rel_bias)
    proj = _projection_weights(jnp.swapaxes(w_in.astype(F32), 1, 2))
    x = x.astype(F32)
    for l in range(norm_mix_g.shape[0]):
        x = _layer(x, l, bias_tiles, proj, prm)
    return x
```

```python
import functools
import math

import jax
import jax.numpy as jnp
from jax import lax
from jax.experimental import pallas as pl
from jax.experimental.pallas import tpu as pltpu

F32 = jnp.float32
BF16 = jnp.bfloat16

N_HEADS_A = 8
HEAD = 64
PAIR = 2 * HEAD
WIDTH_A = N_HEADS_A * HEAD
N_PAIRS = WIDTH_A // PAIR
CHUNK = 64
LOCAL_WAVE = 4
PREV_ROWS = 16
LRU_SLAB = 256
MXU_N = 256
LORA_W, LORA_A, LORA_G = 64, 64, 160
LORA_PAD = 384
GN_EPS = 64e-5
WIDTH_B = 512
LRU_BLOCK = 64
CONV_TAPS = 4
LRU_C = 8.0
GROUPS = ((128, 1), (512, 4), (2048, 16))
HEADS_PER_GROUP = 4
GROUP_W = HEADS_PER_GROUP * HEAD
WIDTH_C = len(GROUPS) * GROUP_W
QBLK = 128
ATTN_TILE = 2048
ATTN_WAVE = 4
N_BUCKETS = 32
MAX_DISTANCE = 2048
NEG_INF = -1e30
RMS_EPS = 1e-6
VMEM_LIMIT = 56 * 1024 * 1024


def _cparams(sem):
    return pltpu.CompilerParams(dimension_semantics=sem, vmem_limit_bytes=VMEM_LIMIT)


def _dot(a, b):
    return jnp.dot(a, b, preferred_element_type=F32)


def _dot_nt(a, b):
    return lax.dot_general(a, b, (((1,), (1,)), ((), ())), preferred_element_type=F32)


_NN = (((1,), (0,)), ((), ()))
_NT = (((1,), (1,)), ((), ()))
_TN = (((0,), (0,)), ((), ()))


def _mm(a, b, dims=_NN):
    return lax.dot_general(a.astype(BF16), b.astype(BF16), dims, preferred_element_type=F32)


def _sigmoid(x):
    return 1.0 / (1.0 + jnp.exp(-x))


def _softplus(x):
    return jnp.maximum(x, 0.0) + jnp.log1p(jnp.exp(-jnp.abs(x)))


def _head_ones(width):
    i = jnp.arange(width) // HEAD
    return (i[:, None] == i[None, :]).astype(BF16)


def _projection_kernel(x_ref, g_ref, w_ref, *rest, n_lo, copies, first_slab, tiles_per_seq):
    lru, (lo_ref, hi_ref, yb_ref, h_ref, zb_ref, hist_ref, carry_ref) = rest[:7], rest[7:]
    i, j = pl.program_id(0), pl.program_id(1)

    @pl.when(j == 0)
    def _():
        x = x_ref[...]
        ms = jnp.mean(x * x, axis=-1, keepdims=True)
        h_ref[...] = (x * lax.rsqrt(ms + RMS_EPS) * g_ref[...]).astype(BF16)

    s = j - first_slab
    rows = pl.ds(pl.multiple_of(s * LRU_SLAB, LRU_SLAB), LRU_SLAB)
    seq_start = jnp.logical_and(s == 0, lax.rem(i, tiles_per_seq) == 0)

    def slab_load():
        return (zb_ref[rows, 0:WIDTH_B].astype(F32), zb_ref[rows, WIDTH_B:2 * WIDTH_B].astype(F32),
                jnp.where(seq_start, 0.0, hist_ref[...]), jnp.where(seq_start, 0.0, carry_ref[0:1, :]))

    def slab_store(x, out, h_last):
        yb_ref[rows, :] = out.astype(yb_ref.dtype)
        hist_ref[...] = x[LRU_SLAB - PREV_ROWS:, :]
        carry_ref[0:1, :] = h_last

    def matmul(out_ref, copy):
        h = h_ref[...]
        for c0 in range(0, out_ref.shape[1], MXU_N):
            z = _dot_nt(h, w_ref[c0:c0 + MXU_N, :])
            out_ref[:, c0:c0 + MXU_N] = z.astype(out_ref.dtype)
            if copy is not None:
                src, dst, n = copy
                lo_c, hi_c = max(src, c0), min(src + n, c0 + MXU_N)
                if lo_c < hi_c:
                    zb_ref[:, dst + lo_c - src:dst + hi_c - src] = z[:, lo_c - c0:hi_c - c0].astype(zb_ref.dtype)
            yield

    def step(out_ref, copy, with_slab):
        if not with_slab:
            _interleave(matmul(out_ref, copy))
        elif copy is None:
            x, yb, prev, h_prev = slab_load()
            _interleave(matmul(out_ref, None), _lru_slab(x, yb, prev, seq_start, h_prev, lru, slab_store))
        else:
            _interleave(matmul(out_ref, copy))
            x, yb, prev, h_prev = slab_load()
            _interleave(_lru_slab(x, yb, prev, seq_start, h_prev, lru, slab_store))

    plain = j < first_slab
    for jb in copies:
        plain = jnp.logical_and(plain, j != jb)
        pl.when(j == jb)(functools.partial(step, lo_ref, copies[jb], jb == first_slab))
    pl.when(plain)(functools.partial(step, lo_ref, None, False))
    pl.when(jnp.logical_and(j > first_slab, j < n_lo))(functools.partial(step, lo_ref, None, True))
    pl.when(j >= n_lo)(functools.partial(step, hi_ref, None, True))


def _projection(x2d, g, w_all, layer, lru, z_b, n_lo_cols, seq_len, tm, tn):
    m, d = x2d.shape
    n = w_all.shape[1]
    n_lo, n_blocks = n_lo_cols // tn, n // tn
    copies = {}
    for jb in range(n_blocks):
        lo_c, hi_c = max(jb * tn, z_b), min((jb + 1) * tn, z_b + 2 * WIDTH_B)
        if lo_c < hi_c:
            copies[jb] = (lo_c - jb * tn, lo_c - z_b, hi_c - lo_c)
    first_slab = max(copies)
    assert first_slab < n_lo and n_blocks - first_slab == tm // LRU_SLAB and seq_len % tm == 0
    const2 = lambda i, j: (0, 0)
    vec = pl.BlockSpec((1, WIDTH_B), const2)
    half = WIDTH_B // 2
    mat = pl.BlockSpec((2, half, half), lambda i, j: (0, 0, 0))
    return pl.pallas_call(
        functools.partial(_projection_kernel, n_lo=n_lo, copies=copies, first_slab=first_slab,
                          tiles_per_seq=seq_len // tm),
        grid=(m // tm, n_blocks),
        in_specs=[pl.BlockSpec((tm, d), lambda i, j: (i, 0)),
                  pl.BlockSpec((1, d), const2),
                  pl.BlockSpec((None, tn, d), lambda i, j: (layer, j, 0)),
                  pl.BlockSpec((CONV_TAPS, WIDTH_B), const2), vec, mat, vec, mat, vec, vec],
        out_specs=[pl.BlockSpec((tm, tn), lambda i, j: (i, jnp.minimum(j, n_lo - 1))),
                   pl.BlockSpec((tm, tn), lambda i, j: (i, jnp.maximum(j - n_lo, 0))),
                   pl.BlockSpec((tm, WIDTH_B), lambda i, j: (i, 0))],
        out_shape=[jax.ShapeDtypeStruct((m, n_lo_cols), BF16),
                   jax.ShapeDtypeStruct((m, n - n_lo_cols), F32),
                   jax.ShapeDtypeStruct((m, WIDTH_B), BF16)],
        scratch_shapes=[pltpu.VMEM((tm, d), BF16), pltpu.VMEM((tm, 2 * WIDTH_B), BF16),
                        pltpu.VMEM((PREV_ROWS, WIDTH_B), F32), pltpu.VMEM((8, WIDTH_B), F32)],
        compiler_params=_cparams(("arbitrary", "arbitrary")),
        name="projection",
    )(x2d, g, w_all, lru["conv_w"], lru["conv_b"], lru["wa"], lru["ba"], lru["wx"], lru["bx"], lru["lam"])


def _relayout_kernel(w_ref, o_ref, *, moves, width):
    cols = w_ref.shape[2]
    end = 0
    for src, dst, n in moves:
        if dst > end:
            o_ref[0, end:dst, :] = jnp.zeros((dst - end, cols), o_ref.dtype)
        o_ref[0, dst:dst + n, :] = w_ref[0, src:src + n, :].astype(o_ref.dtype)
        end = dst + n
    if end < width:
        o_ref[0, end:width, :] = jnp.zeros((width - end, cols), o_ref.dtype)


def _projection_weights(w_t, tc=256):
    n_layers, n_in, d = w_t.shape
    c_rkv, c_lora = 3 * WIDTH_A, LORA_W + LORA_A + LORA_G
    o_b = c_rkv + c_lora
    o_c = o_b + 2 * WIDTH_B
    o_g = o_c + 3 * WIDTH_C
    z_b = -(-(c_rkv + LORA_PAD) // WIDTH_B) * WIDTH_B
    z_g = -(-(z_b + 2 * WIDTH_B) // (3 * d)) * (3 * d)
    z_c = z_g + 3 * d
    width = z_c + 3 * WIDTH_C
    moves = ((0, 0, o_b), (o_b, z_b, o_c - o_b), (o_g, z_g, n_in - o_g), (o_c, z_c, o_g - o_c))
    wz = pl.pallas_call(
        functools.partial(_relayout_kernel, moves=moves, width=width),
        grid=(n_layers, d // tc),
        in_specs=[pl.BlockSpec((1, n_in, tc), lambda l, i: (l, 0, i))],
        out_specs=pl.BlockSpec((1, width, tc), lambda l, i: (l, 0, i)),
        out_shape=jax.ShapeDtypeStruct((n_layers, width, d), BF16),
        compiler_params=_cparams(("parallel", "parallel")),
        name="projection_weights",
    )(w_t)
    return wz, z_b, z_g, z_c


def _shift_rows(cur, prev):
    prev_row = prev[PREV_ROWS - 1:PREV_ROWS, :]
    rolled = pltpu.roll(cur, 1, axis=0)
    row = lax.broadcasted_iota(jnp.int32, cur.shape, 0)
    return jnp.where(row == 0, prev_row, rolled)


def _interleave(*stages):
    live = list(stages)
    while live:
        for gen in list(live):
            try:
                next(gen)
            except StopIteration:
                live.remove(gen)


def _rwkv_prep(zr, zl, prev_r, prev_l, rows, params, scan, put_gate, put_bonus):
    mur_ref, mul_ref, w0_ref, wup_ref, a0_ref, aup_ref, gup_ref, kk_ref, ka_ref, rk_ref, ones_ref = params
    r_out, k_out, v_out, lg_out, lgp_out, as_out, bs_out = scan
    fr = zr + (_shift_rows(zr, prev_r) - zr) * mur_ref[...]
    fl = zl + (_shift_rows(zl, prev_l) - zl) * mul_ref[...]
    r = fr[:, 0:WIDTH_A]
    k = fr[:, WIDTH_A:2 * WIDTH_A]
    v = fr[:, 2 * WIDTH_A:3 * WIDTH_A]
    x_wa = fl[:, 0:LORA_W + LORA_A]
    x_g = fl[:, LORA_W + LORA_A:LORA_PAD]
    ones = ones_ref[...]
    r_out[0, rows, :] = r
    v_out[0, rows, :] = v
    yield

    w = -_softplus(-(w0_ref[...] + _dot(jnp.tanh(x_wa).astype(BF16), wup_ref[...]))) - 0.5
    lw = -jnp.exp(w)
    pos = lax.broadcasted_iota(jnp.int32, lw.shape, 0) & (CHUNK - 1)
    lg = lw
    for sh in [1 << i for i in range(int(math.log2(CHUNK)))]:
        lg = lg + jnp.where(pos >= sh, pltpu.roll(lg, sh, axis=0), 0.0)
    lg_out[0, rows, :] = lg
    lgp_out[0, rows, :] = lg - lw
    yield

    a = _sigmoid(a0_ref[...] + _dot(x_wa.astype(BF16), aup_ref[...]))
    k2 = k * (1.0 + (a - 1.0) * ka_ref[...])
    k_out[0, rows, :] = k2
    yield

    kk = k * kk_ref[...]
    kk = kk / jnp.maximum(jnp.sqrt(_dot((kk * kk).astype(BF16), ones)), 1e-12)
    as_out[0, rows, :] = -kk
    bs_out[0, rows, :] = kk * a
    yield

    put_gate(_dot(_sigmoid(x_g).astype(BF16), gup_ref[...]))
    yield

    put_bonus(_dot((r * k2 * rk_ref[...]).astype(BF16), ones) * v)


def _stack_heads(x):
    lo = lax.broadcasted_iota(jnp.int32, x.shape, 1) < HEAD
    return jnp.concatenate([jnp.where(lo, x, 0.0), jnp.where(lo, 0.0, x)], axis=0)


def _unstack_heads(x):
    return x[0:CHUNK, :] + x[CHUNK:2 * CHUNK, :]


def _rwkv_local_kernel(*refs, n_chunks):
    (zr_ref, zl_ref, zr_next, zl_next), params = refs[:4], refs[4:15]
    t_out, g_out, rh_out, yh_out, gate_out, bonus_out = refs[15:21]
    scan, (gate_carry, bonus_carry) = refs[21:28], refs[28:]
    r_ref, k_ref, v_ref, lg_ref, lgp_ref, as_ref, bs_ref = scan
    two_c = 2 * CHUNK
    row = lax.broadcasted_iota(jnp.int32, (two_c, two_c), 0)
    col = lax.broadcasted_iota(jnp.int32, (two_c, two_c), 1)
    strict = col < row
    incl = col <= row
    eye = (col == row).astype(F32)
    wave_rows = LOCAL_WAVE * CHUNK
    n_waves = n_chunks // LOCAL_WAVE
    first_rows = slice(0, wave_rows)
    f32 = lambda ref, rows: ref[0, rows, :].astype(F32)

    def put(ref, rows):
        def store(val):
            ref[rows] = val
        return store

    def prep(w):
        rows = slice(w * wave_rows, (w + 1) * wave_rows)
        before = slice(w * wave_rows - PREV_ROWS, w * wave_rows)
        return _rwkv_prep(f32(zr_ref, rows), f32(zl_ref, rows), f32(zr_ref, before), f32(zl_ref, before), rows,
                          params, scan, put(gate_out, (0, rows)), put(bonus_out, (0, rows)))

    def prep_first(zr_src, zl_src, prev_r, prev_l):
        return _rwkv_prep(f32(zr_src, first_rows), f32(zl_src, first_rows), prev_r, prev_l, first_rows,
                          params, scan, put(gate_carry, slice(None)), put(bonus_carry, slice(None)))

    @pl.when(pl.program_id(1) == 0)
    def _():
        zero = lambda ref: jnp.zeros((PREV_ROWS, ref.shape[-1]), F32)
        _interleave(prep_first(zr_ref, zl_ref, zero(zr_ref), zero(zl_ref)))

    gate_out[0, first_rows, :] = gate_carry[...]
    bonus_out[0, first_rows, :] = bonus_carry[...]

    def wave(w):
        c0 = w * LOCAL_WAVE
        units = [(c, p) for c in range(c0, min(c0 + LOCAL_WAVE, n_chunks)) for p in range(N_PAIRS)]
        at = lambda ref: [ref[0, c * CHUNK:(c + 1) * CHUNK, p * PAIR:(p + 1) * PAIR] for c, p in units]
        each = lambda f, *ls: [f(*xs) for xs in zip(*ls)]
        lg = at(lg_ref)
        lg_end = each(lambda x: x[CHUNK - 1:CHUNK, :], lg)
        e_neg = each(lambda x: jnp.exp(-x), lg)
        e_end = each(lambda x, xe: jnp.exp(xe - x), lg, lg_end)
        a_s, b_s, kk = at(as_ref), at(bs_ref), at(k_ref)
        a_t = each(lambda x, gp: _stack_heads(x * jnp.exp(gp)), a_s, at(lgp_ref))
        r_t = each(lambda x, g: _stack_heads(x * jnp.exp(g)), at(r_ref), lg)
        b_t = each(lambda x, e: _stack_heads(x * e), b_s, e_neg)
        k_t = each(lambda x, e: _stack_heads(x * e), kk, e_neg)
        b_p = each(lambda x, e: _stack_heads(x * e), b_s, e_end)
        k_p = each(lambda x, e: _stack_heads(x * e), kk, e_end)
        v_s = each(_stack_heads, at(v_ref))
        yield

        vcat = lambda x, y: jnp.concatenate([x, y], axis=0)
        hcat = lambda x, y: jnp.concatenate([x, y], axis=1)
        top = lambda x: x[0:two_c]
        bot = lambda x: x[two_c:2 * two_c]
        left = lambda x: x[:, 0:two_c]
        right = lambda x: x[:, two_c:2 * two_c]

        prod = each(lambda a, r, b, k: _mm(vcat(a, r), vcat(b, k), _NT), a_t, r_t, b_t, k_t)
        l_ab = each(lambda x: jnp.where(strict, left(top(x)), 0.0), prod)
        l_ak = each(lambda x: jnp.where(strict, right(top(x)), 0.0), prod)
        l_rb = each(lambda x: jnp.where(incl, left(bot(x)), 0.0), prod)
        l_rk = each(lambda x: jnp.where(incl, right(bot(x)), 0.0), prod)
        yield

        inv = each(lambda x: eye + x, l_ab)
        pw = each(lambda x: _mm(x, x), l_ab)
        yield
        for _ in range(int(math.log2(CHUNK)) - 2):
            both = each(lambda p, x: _mm(vcat(p, x), p), pw, inv)
            pw = each(top, both)
            inv = each(lambda x, y: x + bot(y), inv, both)
            yield
        inv = each(lambda x, p: x + _mm(x, p), inv, pw)
        yield

        lv_rkv = each(lambda l1, l2, v: _mm(vcat(l1, l2), v), l_ak, l_rk, v_s)
        yield
        aw_h = each(lambda m, a, x: _mm(m, hcat(a, top(x))), inv, a_t, lv_rkv)
        yield
        l_aw = each(_mm, l_rb, aw_h)
        r_h = each(lambda x, y: x + left(y), r_t, l_aw)
        y_h = each(lambda y, x: right(y) + bot(x), l_aw, lv_rkv)
        yield
        tg = each(lambda x, bp: _mm(x, bp, _TN), aw_h, b_p)
        t_m = each(lambda ge, x: eye * jnp.exp(ge) + top(x), lg_end, tg)
        yield
        g_m = each(lambda x, v, kp: bot(x) + _mm(v, kp, _TN), tg, v_s, k_p)
        for i, (c, p) in enumerate(units):
            rows = slice(c * CHUNK, (c + 1) * CHUNK)
            lanes = slice(p * PAIR, (p + 1) * PAIR)
            t_out[0, c, p] = t_m[i].astype(t_out.dtype)
            g_out[0, c, p] = g_m[i]
            rh_out[0, rows, lanes] = _unstack_heads(r_h[i]).astype(rh_out.dtype)
            yh_out[0, rows, lanes] = _unstack_heads(y_h[i])

    last = slice(n_waves * wave_rows - PREV_ROWS, n_waves * wave_rows)
    for w in range(n_waves):
        nxt = prep(w + 1) if w + 1 < n_waves else prep_first(zr_next, zl_next, f32(zr_ref, last), f32(zl_ref, last))
        _interleave(wave(w), nxt)


def _rwkv_local(z3, p, ts):
    b, s, _ = z3.shape
    n_chunks = ts // CHUNK
    wr, wl = 3 * WIDTH_A, LORA_PAD
    lora_block = wr // wl
    wave_rows = LOCAL_WAVE * CHUNK
    n_waves = ts // wave_rows
    assert n_waves >= 2 and ts % wave_rows == 0
    nxt = lambda c: (lambda bi, i: (bi, jnp.minimum((i + 1) * n_waves, s // wave_rows - 1), c))
    cur = lambda bi, i: (bi, i, 0)
    mat = lambda bi, i: (bi, i, 0, 0, 0)
    const = lambda bi, i: (0, 0)
    vec = pl.BlockSpec((1, WIDTH_A), const)
    seq_spec = pl.BlockSpec((1, ts, WIDTH_A), cur)
    mat_spec = pl.BlockSpec((1, n_chunks, N_PAIRS, PAIR, PAIR), mat)
    seq_shape = lambda dt: jax.ShapeDtypeStruct((b, s, WIDTH_A), dt)
    mat_shape = lambda dt: jax.ShapeDtypeStruct((b, s // CHUNK, N_PAIRS, PAIR, PAIR), dt)
    return pl.pallas_call(
        functools.partial(_rwkv_local_kernel, n_chunks=n_chunks),
        grid=(b, s // ts),
        in_specs=[pl.BlockSpec((1, ts, wr), cur),
                  pl.BlockSpec((1, ts, wl), lambda bi, i: (bi, i, lora_block)),
                  pl.BlockSpec((1, wave_rows, wr), nxt(0)),
                  pl.BlockSpec((1, wave_rows, wl), nxt(lora_block)),
                  pl.BlockSpec((1, wr), const), pl.BlockSpec((1, wl), const),
                  vec, pl.BlockSpec((LORA_W + LORA_A, WIDTH_A), const),
                  vec, pl.BlockSpec((LORA_W + LORA_A, WIDTH_A), const),
                  pl.BlockSpec((LORA_PAD - LORA_W - LORA_A, WIDTH_A), const),
                  vec, vec, vec, pl.BlockSpec((WIDTH_A, WIDTH_A), const)],
        out_specs=[mat_spec, mat_spec, seq_spec, seq_spec, seq_spec, seq_spec],
        out_shape=[mat_shape(BF16), mat_shape(F32), seq_shape(BF16), seq_shape(F32), seq_shape(F32), seq_shape(F32)],
        scratch_shapes=[pltpu.VMEM((1, ts, WIDTH_A), F32)] * 7 + [pltpu.VMEM((wave_rows, WIDTH_A), F32)] * 2,
        compiler_params=_cparams(("parallel", "arbitrary")),
        name="rwkv_local",
    )(z3, z3, z3, z3, p["mu_r"], p["mu_l"], p["w0"], p["w_up"], p["a0"], p["a_up"], p["g_up"],
      p["k_k"], p["k_a"], p["r_k"], p["ones"])


def _rwkv_state_kernel(t_ref, gm_ref, rh_ref, yh_ref, g_ref, bonus_ref, lng_ref, lnb_ref, ones_ref,
                       y_out, s_ref, *, n_chunks):
    n_seq = s_ref.shape[0]

    @pl.when(pl.program_id(0) == 0)
    def _():
        s_ref[...] = jnp.zeros_like(s_ref)

    ones = ones_ref[...]
    units = [(b, p) for b in range(n_seq) for p in range(N_PAIRS)]
    rows = [slice(c * CHUNK, (c + 1) * CHUNK) for c in range(n_chunks)]
    state = {u: s_ref[u] for u in units}
    y = []
    for c in range(n_chunks):
        entry = dict(state)
        state = {(b, p): _mm(entry[b, p], t_ref[b, c, p]) + gm_ref[b, c, p] for b, p in units}
        y.append([jnp.concatenate([_mm(rh_ref[b, rows[c], p * PAIR:(p + 1) * PAIR], entry[b, p], _NT)
                                   for p in range(N_PAIRS)], axis=1) + yh_ref[b, rows[c], :]
                  for b in range(n_seq)])
    for u in units:
        s_ref[u] = state[u]
    flat = [(c, b) for c in range(n_chunks) for b in range(n_seq)]
    yv = [y[c][b] for c, b in flat]
    mean = [_dot(v.astype(BF16), ones) * (1.0 / HEAD) for v in yv]
    yc = [v - m for v, m in zip(yv, mean)]
    var = [_dot((v * v).astype(BF16), ones) * (1.0 / HEAD) for v in yc]
    for (c, b), v, vr in zip(flat, yc, var):
        yn = v * lax.rsqrt(vr + GN_EPS) * lng_ref[...] + lnb_ref[...]
        y_out[b, rows[c], :] = ((yn + bonus_ref[b, rows[c], :]) * g_ref[b, rows[c], :]).astype(y_out.dtype)


def _rwkv_state(t, gm, rh, yh, g, bonus, p, ts):
    b, s, _ = rh.shape
    n_chunks = ts // CHUNK
    cur = lambda i: (0, i, 0)
    mat = lambda i: (0, i, 0, 0, 0)
    const = lambda i: (0, 0)
    seq_spec = pl.BlockSpec((b, ts, WIDTH_A), cur)
    mat_spec = pl.BlockSpec((b, n_chunks, N_PAIRS, PAIR, PAIR), mat)
    vec = pl.BlockSpec((1, WIDTH_A), const)
    return pl.pallas_call(
        functools.partial(_rwkv_state_kernel, n_chunks=n_chunks),
        grid=(s // ts,),
        in_specs=[mat_spec, mat_spec, seq_spec, seq_spec, seq_spec, seq_spec, vec, vec,
                  pl.BlockSpec((WIDTH_A, WIDTH_A), const)],
        out_specs=seq_spec,
        out_shape=jax.ShapeDtypeStruct((b, s, WIDTH_A), BF16),
        scratch_shapes=[pltpu.VMEM((b, N_PAIRS, PAIR, PAIR), F32)],
        compiler_params=_cparams(("arbitrary",)),
        name="rwkv_state",
    )(t, gm, rh, yh, g, bonus, p["ln_g"], p["ln_b"], p["ones"])


def _rwkv_mixer(z3, p):
    t, gm, rh, yh, g, bonus = _rwkv_local(z3, p, ts=512)
    return _rwkv_state(t, gm, rh, yh, g, bonus, p, ts=512)


def _lru_slab(x, yb, prev, seq_start, h_prev, params, emit):
    cw_ref, cb_ref, wa_ref, ba_ref, wx_ref, bx_ref, lam_ref = params
    n_rows = x.shape[0]
    ext = jnp.concatenate([prev, x], axis=0)
    xc = x * cw_ref[CONV_TAPS - 1:CONV_TAPS, :] + cb_ref[...]
    for back in range(1, CONV_TAPS):
        tap = CONV_TAPS - 1 - back
        xc = xc + pltpu.roll(ext, back, axis=0)[PREV_ROWS:, :] * cw_ref[tap:tap + 1, :]

    half = WIDTH_B // 2
    xcb = xc.astype(BF16)
    ga = jnp.concatenate([_dot(xcb[:, j * half:(j + 1) * half], wa_ref[j]) for j in range(2)], axis=1)
    gx = jnp.concatenate([_dot(xcb[:, j * half:(j + 1) * half], wx_ref[j]) for j in range(2)], axis=1)
    gate_a = _sigmoid(ga + ba_ref[...])
    gate_x = _sigmoid(gx + bx_ref[...])
    log_a = -LRU_C * gate_a * _softplus(-lam_ref[...])
    a = jnp.exp(log_a)
    mult = jnp.sqrt(jnp.maximum(-jnp.tanh(log_a) * (1.0 + a * a), 0.0))
    xg = xc * gate_x
    b = xg * mult
    yield

    row8 = lax.broadcasted_iota(jnp.int32, (8, WIDTH_B), 0)
    h, hs = h_prev, []
    for g in range(n_rows // 8):
        a8, b8 = a[8 * g:8 * g + 8, :], b[8 * g:8 * g + 8, :]
        if g == 0:
            b8 = jnp.where(jnp.logical_and(row8 == 0, seq_start), xg[0:8, :], b8)
        for sh in (1, 2, 4):
            ar = pltpu.roll(a8, sh, axis=0)
            br = pltpu.roll(b8, sh, axis=0)
            m = row8 >= sh
            b8 = jnp.where(m, a8 * br + b8, b8)
            a8 = jnp.where(m, a8 * ar, a8)
        h8 = a8 * h + b8
        hs.append(h8)
        h = h8[7:8, :]
    yield
    gelu = 0.5 * yb * (1.0 + jnp.tanh(math.sqrt(2.0 / math.pi) * (yb + 0.044715 * (yb * yb * yb))))
    emit(x, jnp.concatenate(hs, axis=0) * gelu, h)


def _attn_kernel(q0, q1, k0, k1, kp0, kp1, v0, v1, vp0, vp1, bias_ref, qg_ref, kg_ref, ones_ref,
                 o0, o1, l0, l1, *, dil, n_sub):
    j = pl.program_id(1)
    ones = ones_ref[...]
    lane = lax.broadcasted_iota(jnp.int32, (QBLK, GROUP_W), 1)
    in_head = [(lane >= h * HEAD) & (lane < (h + 1) * HEAD) for h in range(HEADS_PER_GROUP)]
    prev_valid = (lax.broadcasted_iota(jnp.int32, (HEADS_PER_GROUP * QBLK, 2 * QBLK), 1) >= QBLK) | (j > 0)

    def rows(start):
        return pl.ds(start, QBLK, stride=dil) if dil > 1 else pl.ds(start, QBLK)

    def take(lo, hi, start):
        return jnp.concatenate([lo[0, rows(start), :], hi[0, rows(start), :]], axis=1)

    def head_sumsq(x):
        return _dot((x * x).astype(BF16), ones) * (1.0 / HEAD)

    def select_heads(x):
        out = jnp.zeros((QBLK, GROUP_W), F32)
        for h, m in enumerate(in_head):
            out = jnp.where(m, x[h * QBLK:(h + 1) * QBLK, :], out)
        return out

    def wave(units):
        each = lambda f, *ls: [f(*xs) for xs in zip(*ls)]
        span = dil * QBLK
        starts = [s for s, _ in units]
        before = [s - span if dil > 1 or isinstance(s, int) else pl.multiple_of(s - span, QBLK) for s in starts]
        q_raw = [take(q0, q1, s) for s in starts]
        if dil == 1:
            far0 = units[0][1]
            k_raw = [take(kp0, kp1, starts[0]) if far0 else take(k0, k1, before[0])] + [take(k0, k1, s) for s in starts]
            v_blk = [(take(vp0, vp1, starts[0]) if far0 else take(v0, v1, before[0])).astype(BF16)]
            v_blk += [take(v0, v1, s).astype(BF16) for s in starts]
            pairs = lambda blk: [jnp.concatenate([blk[u], blk[u + 1]], axis=0) for u in range(len(units))]
            vv = pairs(v_blk)
        else:
            k_raw = [jnp.concatenate([take(kp0, kp1, s) if far else take(k0, k1, p), take(k0, k1, s)], axis=0)
                     for (s, far), p in zip(units, before)]
            vv = [jnp.concatenate([take(vp0, vp1, s) if far else take(v0, v1, p), take(v0, v1, s)],
                                  axis=0).astype(BF16) for (s, far), p in zip(units, before)]
            pairs = lambda blk: blk
        q_ms = each(head_sumsq, q_raw)
        k_ms = each(head_sumsq, k_raw)
        q = each(lambda x, ms: x * lax.rsqrt(ms + RMS_EPS) * qg_ref[...] * (HEAD ** -0.5), q_raw, q_ms)
        kk = pairs(each(lambda x, ms: (x * lax.rsqrt(ms + RMS_EPS) * kg_ref[...]).astype(BF16), k_raw, k_ms))
        qs = each(lambda x: jnp.concatenate([jnp.where(m, x, 0.0) for m in in_head], axis=0).astype(BF16), q)
        logits = each(lambda a, b: _dot_nt(a, b) + bias_ref[...], qs, kk)
        logits = [jnp.where(prev_valid, lg, NEG_INF) if far else lg for lg, (_, far) in zip(logits, units)]
        mx = each(lambda lg: jnp.max(lg, axis=-1, keepdims=True), logits)
        pr = each(lambda lg, m: jnp.exp(lg - m), logits, mx)
        den = each(lambda p: jnp.sum(p, axis=-1, keepdims=True), pr)
        pv = each(lambda p, v, dn: _dot(p.astype(BF16), v) / dn, pr, vv, den)
        out = each(select_heads, pv)
        lse = each(lambda m, dn: select_heads(jnp.broadcast_to(m + jnp.log(dn), (HEADS_PER_GROUP * QBLK, GROUP_W))),
                   mx, den)
        for s, o, l in zip(starts, out, lse):
            o0[0, rows(s), :] = o[:, 0:PAIR]
            o1[0, rows(s), :] = o[:, PAIR:GROUP_W]
            l0[0, rows(s), :] = l[:, 0:PAIR]
            l1[0, rows(s), :] = l[:, PAIR:GROUP_W]

    def loop(lo, hi, body):
        def step(i, carry):
            body(i)
            return carry
        lax.fori_loop(lo, hi, step, 0)

    span = dil * QBLK
    if n_sub == 1:
        loop(0, dil // ATTN_WAVE, lambda i: wave([(i * ATTN_WAVE + u, True) for u in range(ATTN_WAVE)]))
    elif dil > 1:
        wave([(r, True) for r in range(dil)])
        loop(1, n_sub, lambda n: wave([(r + n * span, False) for r in range(dil)]))
    else:
        wave([(u * QBLK, u == 0) for u in range(ATTN_WAVE)])
        loop(1, n_sub // ATTN_WAVE,
             lambda i: wave([(pl.multiple_of((i * ATTN_WAVE + u) * QBLK, QBLK), False) for u in range(ATTN_WAVE)]))


def _attn_group(z3, col0, bias, qg, kg, ones, gi, dil):
    b, s, _ = z3.shape
    span = dil * QBLK
    n_sub = ATTN_TILE // span
    halves = GROUP_W // PAIR
    per_part = len(GROUPS) * halves

    def cur(part, half):
        c = col0 + part * per_part + gi * halves + half
        return pl.BlockSpec((1, ATTN_TILE, PAIR), lambda bi, j: (bi, j, c))

    def prev(part, half):
        c = col0 + part * per_part + gi * halves + half
        return pl.BlockSpec((1, span, PAIR), lambda bi, j: (bi, jnp.maximum(j * n_sub - 1, 0), c))

    const2 = lambda bi, j: (0, 0)
    out_spec = pl.BlockSpec((1, ATTN_TILE, PAIR), lambda bi, j: (bi, j, 0))
    return pl.pallas_call(
        functools.partial(_attn_kernel, dil=dil, n_sub=n_sub),
        grid=(b, s // ATTN_TILE),
        in_specs=[cur(0, 0), cur(0, 1), cur(1, 0), cur(1, 1), prev(1, 0), prev(1, 1),
                  cur(2, 0), cur(2, 1), prev(2, 0), prev(2, 1),
                  pl.BlockSpec((HEADS_PER_GROUP * QBLK, 2 * QBLK), const2),
                  pl.BlockSpec((1, GROUP_W), const2), pl.BlockSpec((1, GROUP_W), const2),
                  pl.BlockSpec((GROUP_W, GROUP_W), const2)],
        out_specs=[out_spec] * 4,
        out_shape=[jax.ShapeDtypeStruct((b, s, PAIR), F32)] * 4,
        compiler_params=_cparams(("parallel", "arbitrary")),
        name=f"dilated_attn_g{gi}",
    )(*([z3] * 10), bias, qg, kg, ones)


def _t5_bucket(dist):
    max_exact = N_BUCKETS // 2
    d = jnp.maximum(dist, 0)
    large = max_exact + (jnp.log(jnp.maximum(d, 1).astype(F32) / max_exact)
                         / math.log(MAX_DISTANCE / max_exact) * (N_BUCKETS - max_exact)).astype(jnp.int32)
    large = jnp.minimum(large, N_BUCKETS - 1)
    return jnp.where(d < max_exact, d, large)


def _attn_bias_tiles(rel_bias):
    tiles = []
    kj = jnp.arange(2 * QBLK)[None, :]
    rel = (jnp.arange(QBLK)[:, None] + QBLK) - kj
    for gi, (window, dil) in enumerate(GROUPS):
        band = (rel >= 0) & (rel <= window // dil)
        tab = rel_bias.astype(F32)[:, gi * HEADS_PER_GROUP:(gi + 1) * HEADS_PER_GROUP]
        onehot = (_t5_bucket(rel * dil)[..., None] == jnp.arange(N_BUCKETS)).astype(F32)
        bias = jnp.einsum("qkn,nh->hqk", onehot, tab, precision=lax.Precision.HIGHEST)
        tiles.append(jnp.where(band[None], bias, NEG_INF).reshape(HEADS_PER_GROUP * QBLK, 2 * QBLK))
    return tiles


def _merge_kernel(x_ref, ya_ref, yb_ref, *rest):
    n_g = len(GROUPS)
    attn = rest[:4 * n_g]
    zg_ref, pa_ref, pb_ref, pc_ref, wo_ref, out_ref = rest[4 * n_g:]
    d = x_ref.shape[-1]
    outs = [jnp.concatenate([attn[4 * g][...], attn[4 * g + 1][...]], axis=1) for g in range(n_g)]
    lses = [jnp.concatenate([attn[4 * g + 2][...], attn[4 * g + 3][...]], axis=1) for g in range(n_g)]
    m = functools.reduce(jnp.maximum, lses)
    es = [jnp.exp(l - m) for l in lses]
    yc = sum(o * e for o, e in zip(outs, es)) / sum(es)
    gate = lambda n: _sigmoid(zg_ref[:, n * d:(n + 1) * d].astype(F32))
    merged = (gate(0) * _dot(ya_ref[...].astype(BF16), pa_ref[...])
              + gate(1) * _dot(yb_ref[...].astype(BF16), pb_ref[...])
              + gate(2) * _dot(yc.astype(BF16), pc_ref[...]))
    out_ref[...] = x_ref[...] + _dot(merged.astype(BF16), wo_ref[...])


def _merge(x2d, ya, yb, attn, z2d, gate_block, p, tm):
    m, d = x2d.shape
    row = lambda w: pl.BlockSpec((tm, w), lambda i: (i, 0))
    full = lambda a: pl.BlockSpec(a.shape, lambda i: (0, 0))
    return pl.pallas_call(
        _merge_kernel,
        grid=(m // tm,),
        in_specs=[row(d), row(WIDTH_A), row(WIDTH_B)] + [row(PAIR)] * len(attn)
                 + [pl.BlockSpec((tm, 3 * d), lambda i: (i, gate_block)),
                    full(p["proj_a"]), full(p["proj_b"]), full(p["proj_c"]), full(p["w_out"])],
        out_specs=row(d),
        out_shape=jax.ShapeDtypeStruct((m, d), F32),
        compiler_params=_cparams(("parallel",)),
        name="merge",
    )(x2d, ya, yb, *attn, z2d, p["proj_a"], p["proj_b"], p["proj_c"], p["w_out"])


def _mlp_kernel(x_ref, g_ref, wu_ref, wd_ref, o_ref, h_ref, acc_ref):
    j = pl.program_id(1)

    @pl.when(j == 0)
    def _():
        x = x_ref[...]
        ms = jnp.mean(x * x, axis=-1, keepdims=True)
        h_ref[...] = (x * lax.rsqrt(ms + RMS_EPS) * g_ref[...]).astype(BF16)
        acc_ref[...] = jnp.zeros_like(acc_ref)

    u = jnp.maximum(_dot(h_ref[...], wu_ref[...]), 0.0)
    acc_ref[...] += _dot((u * u).astype(BF16), wd_ref[...])

    @pl.when(j == pl.num_programs(1) - 1)
    def _():
        o_ref[...] = x_ref[...] + acc_ref[...]


def _mlp(x2d, g, wu, wd, tm, tf):
    m, d = x2d.shape
    f = wu.shape[1]
    return pl.pallas_call(
        _mlp_kernel,
        grid=(m // tm, f // tf),
        in_specs=[pl.BlockSpec((tm, d), lambda i, j: (i, 0)),
                  pl.BlockSpec((1, d), lambda i, j: (0, 0)),
                  pl.BlockSpec((d, tf), lambda i, j: (0, j)),
                  pl.BlockSpec((tf, d), lambda i, j: (j, 0))],
        out_specs=pl.BlockSpec((tm, d), lambda i, j: (i, 0)),
        out_shape=jax.ShapeDtypeStruct((m, d), F32),
        scratch_shapes=[pltpu.VMEM((tm, d), BF16), pltpu.VMEM((tm, d), F32)],
        compiler_params=_cparams(("parallel", "arbitrary")),
        name="mlp",
    )(x2d, g, wu, wd)


def _pad_rows(w, lo, total):
    return jnp.pad(w, ((lo, total - lo - w.shape[0]), (0, 0)))


def _block_diag_halves(w):
    n, bd, _ = w.shape
    per = n // 2
    out = jnp.zeros((2, per * bd, per * bd), w.dtype)
    for i in range(n):
        j, q = divmod(i, per)
        out = out.at[j, q * bd:(q + 1) * bd, q * bd:(q + 1) * bd].set(w[i])
    return out.astype(BF16)


def _layer(x, l, bias_tiles, proj, prm):
    (norm_mix_g, _, rwkv_mu, rwkv_w0, rwkv_w_up, rwkv_a0, rwkv_a_up, rwkv_g_up, rwkv_k_k, rwkv_k_a,
     rwkv_r_k, rwkv_ln_g, rwkv_ln_b, proj_a, conv_w, conv_b, lru_wa, lru_ba, lru_wx, lru_bx, lru_lambda,
     proj_b, q_norm_g, k_norm_g, proj_c, w_out, norm_mlp_g, mlp_up, mlp_down) = [t[l] for t in prm]
    b, s, d = x.shape
    x2d = x.reshape(b * s, d)
    row = lambda t: t.reshape(1, -1).astype(F32)

    wz_all, z_b, z_g, z_c = proj
    c_rkv, c_lora = 3 * WIDTH_A, LORA_W + LORA_A + LORA_G
    pb = dict(conv_w=conv_w.astype(F32), conv_b=row(conv_b), wa=_block_diag_halves(lru_wa), ba=row(lru_ba),
              wx=_block_diag_halves(lru_wx), bx=row(lru_bx), lam=row(lru_lambda))
    z2d, zc2d, yb = _projection(x2d, row(norm_mix_g), wz_all, l, pb, z_b, z_c, s, 2048, 768)
    z3 = z2d.reshape(b, s, z_c)
    zc3 = zc2d.reshape(b, s, 3 * WIDTH_C)

    ones_a = _head_ones(WIDTH_A)
    pa = dict(
        mu_r=row(rwkv_mu[0:c_rkv]),
        mu_l=jnp.pad(row(rwkv_mu[c_rkv:]), ((0, 0), (0, LORA_PAD - c_lora))),
        w0=row(rwkv_w0), a0=row(rwkv_a0),
        w_up=_pad_rows(rwkv_w_up, 0, LORA_W + LORA_A).astype(BF16),
        a_up=_pad_rows(rwkv_a_up, LORA_W, LORA_W + LORA_A).astype(BF16),
        g_up=_pad_rows(rwkv_g_up, 0, LORA_PAD - LORA_W - LORA_A).astype(BF16),
        k_k=row(rwkv_k_k), k_a=row(rwkv_k_a), r_k=row(rwkv_r_k),
        ln_g=row(rwkv_ln_g), ln_b=row(rwkv_ln_b), ones=ones_a)
    ya = _rwkv_mixer(z3, pa)

    qg = jnp.tile(row(q_norm_g), (1, HEADS_PER_GROUP))
    kg = jnp.tile(row(k_norm_g), (1, HEADS_PER_GROUP))
    ones_c = _head_ones(GROUP_W)
    attn = []
    for gi, (_, dil) in enumerate(GROUPS):
        parts = _attn_group(zc3, 0, bias_tiles[gi], qg, kg, ones_c, gi, dil)
        attn += [t.reshape(b * s, PAIR) for t in parts]

    pm = dict(proj_a=proj_a.astype(BF16), proj_b=proj_b.astype(BF16), proj_c=proj_c.astype(BF16),
              w_out=w_out.astype(BF16))
    x1 = _merge(x2d, ya.reshape(b * s, WIDTH_A), yb, attn, z2d, z_g // (3 * d), pm, 512)
    x2 = _mlp(x1, row(norm_mlp_g), mlp_up.astype(BF16), mlp_down.astype(BF16), 1024, 2048)
    return x2.reshape(b, s, d)


def kernel(x, rel_bias, norm_mix_g, w_in, rwkv_mu, rwkv_w0, rwkv_w_up, rwkv_a0, rwkv_a_up, rwkv_g_up, rwkv_k_k, rwkv_k_a, rwkv_r_k, rwkv_ln_g, rwkv_ln_b, proj_a, conv_w, conv_b, lru_wa, lru_ba, lru_wx, lru_bx, lru_lambda, proj_b, q_norm_g, k_norm_g, proj_c, w_out, norm_mlp_g, mlp_up, mlp_down):
    prm = (norm_mix_g, w_in, rwkv_mu, rwkv_w0, rwkv_w_up, rwkv_a0, rwkv_a_up, rwkv_g_up, rwkv_k_k, rwkv_k_a,
           rwkv_r_k, rwkv_ln_g, rwkv_ln_b, proj_a, conv_w, conv_b, lru_wa, lru_ba, lru_wx, lru_bx, lru_lambda,
           proj_b, q_norm_g, k_norm_g, proj_c, w_out, norm_mlp_g, mlp_up, mlp_down)
    bias_tiles = _attn_bias_tiles(rel_bias)
    proj = _projection_weights(jnp.swapaxes(w_in.astype(F32), 1, 2))
    x = x.astype(F32)
    for l in range(norm_mix_g.shape[0]):
        x = _layer(x, l, bias_tiles, proj, prm)
    return x
```
